```python
import jax, jax.numpy as jnp
from jax import lax
import numpy as np

D_MODEL = 2048
BATCH = 2
SEQ = 4096
DEPTH = 4
DEC_BATCH = 8
DEC_SEQ = 1
PAST_LEN = 16384
PAGE_SIZE = 128

N_MIXERS = 2
HEAD_DIM = 128
N_HEADS = D_MODEL // HEAD_DIM
Q_BLOCK = 128
NSA_KV_GROUPS = 4
NSA_GROUP = N_HEADS // NSA_KV_GROUPS
NSA_KV = NSA_KV_GROUPS * HEAD_DIM
NSA_BLOCK = 64
NSA_TOPN = 16
NSA_WINDOW = 512
NSA_MLP_HIDDEN = 256
NSA_IN = D_MODEL + 6 * NSA_KV + 3 * N_HEADS
SB_IN = 3 * D_MODEL
ROPE_THETA = 500000.0
ROPE_DIMS = HEAD_DIM // 4
PEER_HEADS = 8
PEER_NKEYS = 128
PEER_EXPERTS = PEER_NKEYS * PEER_NKEYS
PEER_DKEY = 256
PEER_TOPK = 16
DEEPNORM_ALPHA = (2 * DEPTH) ** 0.25
DEEPNORM_BETA = (8 * DEPTH) ** -0.25
LN_EPS = 1e-5
NEG_INF = -1e30
FORCE_SCORE = 1e9
N_SB_LAYERS = (DEPTH + 1) // 2
N_NSA_LAYERS = DEPTH // 2

kernel_name = 'stickbreak_nsa_peer_hybrid_step'


def _layernorm(x, g, b):
    xf = x.astype(jnp.float32)
    mu = jnp.mean(xf, -1, keepdims=True)
    var = jnp.mean(jnp.square(xf - mu), -1, keepdims=True)
    return ((xf - mu) * lax.rsqrt(var + LN_EPS) * g + b).astype(x.dtype)


def _modulation(c, w, b):
    return (jax.nn.silu(c) @ w + b).reshape(c.shape[0], 6, 1, D_MODEL)


def _rope(x, pos):
    half = ROPE_DIMS // 2
    inv = jnp.power(ROPE_THETA, -jnp.arange(half, dtype=jnp.float32) / half)
    ang = (pos.astype(jnp.float32)[:, None] * inv).reshape((pos.shape[0],) + (1,) * (x.ndim - 3) + (half,))
    cos, sin = jnp.cos(ang), jnp.sin(ang)
    x1 = x[..., :half].astype(jnp.float32)
    x2 = x[..., half:ROPE_DIMS].astype(jnp.float32)
    return jnp.concatenate([(x1 * cos - x2 * sin).astype(x.dtype), (x2 * cos + x1 * sin).astype(x.dtype), x[..., ROPE_DIMS:]], -1)


def _token_blocks(fn, h):
    n = h.shape[0]
    if n <= Q_BLOCK:
        return fn(h)
    nb = -(-n // Q_BLOCK)
    hp = jnp.pad(h, ((0, nb * Q_BLOCK - n), (0, 0)))
    out = lax.map(fn, hp.reshape(nb, Q_BLOCK, h.shape[1]))
    return out.reshape(nb * Q_BLOCK, -1)[:n]


def _sb_weights(z, allowed):
    log_keep = jnp.where(allowed, jax.nn.log_sigmoid(-z), 0.0)
    later = lax.cumsum(log_keep, axis=z.ndim - 1, reverse=True) - log_keep
    return jnp.where(allowed, jnp.exp(jax.nn.log_sigmoid(z) + later), 0.0)


def _sb_prompt(h, w_in, w_o):
    B, T, _ = h.shape
    qkv = (h @ w_in).reshape(B, T, 3, N_HEADS, HEAD_DIM)
    k, v = qkv[:, :, 1], qkv[:, :, 2]
    nb = T // Q_BLOCK
    pos = jnp.arange(T)
    qb = qkv[:, :, 0].reshape(B, nb, Q_BLOCK, N_HEADS, HEAD_DIM).swapaxes(0, 1)
    scale = HEAD_DIM ** -0.5

    def blk(args):
        qi, pi = args
        z = jnp.einsum('bqhd,bkhd->bhqk', qi, k, preferred_element_type=jnp.float32) * scale
        a = _sb_weights(z, pos[None, :] < pi[:, None])
        return jnp.einsum('bhqk,bkhd->bqhd', a.astype(v.dtype), v)

    o = lax.map(blk, (qb, pos.reshape(nb, Q_BLOCK)))
    o = o.swapaxes(0, 1).reshape(B, T, D_MODEL)
    return o @ w_o, qkv[:, :, 1:]


def _sb_sample(h, cache_kv, page_table, w_in, w_o):
    Bd, Td, _ = h.shape
    P = page_table.shape[1] * PAGE_SIZE
    qkv = (h @ w_in).reshape(Bd, Td, 3, N_HEADS, HEAD_DIM)
    q = qkv[:, :, 0]
    k_past = cache_kv[page_table, :, 0].reshape(Bd, P, N_HEADS, HEAD_DIM)
    v_past = cache_kv[page_table, :, 1].reshape(Bd, P, N_HEADS, HEAD_DIM)
    scale = HEAD_DIM ** -0.5
    z = jnp.concatenate([
        jnp.einsum('bqhd,bkhd->bhqk', q, k_past, preferred_element_type=jnp.float32),
        jnp.einsum('bqhd,bkhd->bhqk', q, qkv[:, :, 1], preferred_element_type=jnp.float32)], -1) * scale
    qpos = P + jnp.arange(Td)
    kpos = jnp.arange(P + Td)
    a = _sb_weights(z, kpos[None, :] < qpos[:, None]).astype(h.dtype)
    o = (jnp.einsum('bhqk,bkhd->bqhd', a[..., :P], v_past)
         + jnp.einsum('bhqk,bkhd->bqhd', a[..., P:], qkv[:, :, 2]))
    return o.reshape(Bd, Td, D_MODEL) @ w_o, qkv[:, :, 1:]


def _nsa_project(h, w_in, pos):
    B, T, _ = h.shape
    proj = h @ w_in
    q = _rope(proj[..., :D_MODEL].reshape(B, T, N_HEADS, HEAD_DIM), pos)
    kv = proj[..., D_MODEL:D_MODEL + 6 * NSA_KV].reshape(B, T, 3, 2, NSA_KV_GROUPS, HEAD_DIM)
    kv = jnp.stack([_rope(kv[:, :, :, 0], pos), kv[:, :, :, 1]], axis=3).reshape(B, T, 6, NSA_KV_GROUPS, HEAD_DIM)
    gates = jax.nn.sigmoid(proj[..., D_MODEL + 6 * NSA_KV:].reshape(B, T, 3, N_HEADS))
    return q, kv, gates


def _compress(blk, pe, w1, b1, w2, b2):
    B, nb = blk.shape[:2]
    flat = (blk + pe[:, None, :]).transpose(0, 1, 3, 2, 4).reshape(B, nb, NSA_KV_GROUPS, NSA_BLOCK * HEAD_DIM)
    return jax.nn.gelu(flat @ w1 + b1, approximate=False) @ w2 + b2


def _nsa_context(kv4, pe, w1, b1, w2, b2):
    B, L = kv4.shape[:2]
    nb = -(-L // NSA_BLOCK)
    if nb * NSA_BLOCK != L:
        kv4 = jnp.pad(kv4, ((0, 0), (0, nb * NSA_BLOCK - L), (0, 0), (0, 0), (0, 0)))
    blocks = kv4.reshape(B, nb, NSA_BLOCK, 4, NSA_KV_GROUPS, HEAD_DIM)
    kc = _compress(blocks[:, :, :, 0], pe[0], w1[0], b1[0], w2[0], b2[0])
    vc = _compress(blocks[:, :, :, 1], pe[1], w1[1], b1[1], w2[1], b2[1])
    sel = blocks[:, :, :, 2:].transpose(3, 0, 4, 1, 2, 5)
    return kc, vc, sel[0], sel[1]


def _nsa_block(q, qpos, gates, kc, vc, ksb, vsb, kw, vw, kwpos):
    B, Tq = q.shape[:2]
    nb = kc.shape[1]
    scale = HEAD_DIM ** -0.5
    qg = q.reshape(B, Tq, NSA_KV_GROUPS, NSA_GROUP, HEAD_DIM)
    blk_ids = jnp.arange(nb)
    zc = jnp.einsum('bqgrd,bngd->bgrqn', qg, kc, preferred_element_type=jnp.float32) * scale
    avail = (blk_ids[None, :] + 1) * NSA_BLOCK - 1 <= qpos[:, None]
    pc = jnp.where(avail, jax.nn.softmax(jnp.where(avail, zc, NEG_INF), axis=-1), 0.0)
    o_cmp = jnp.einsum('bgrqn,bngd->bqgrd', pc.astype(vc.dtype), vc)
    cur = qpos[:, None] // NSA_BLOCK
    forced = (blk_ids[None, :] == 0) | (blk_ids[None, :] == cur) | (blk_ids[None, :] == cur - 1)
    score = jnp.where(blk_ids[None, :] <= cur, jnp.where(forced, FORCE_SCORE, jnp.sum(pc, axis=2)), -1.0)
    top_s, top_i = lax.top_k(score, min(NSA_TOPN, nb))
    n_sel = top_i.shape[-1]
    bi = jnp.arange(B)[:, None, None, None]
    gi = jnp.arange(NSA_KV_GROUPS)[None, :, None, None]
    ks = ksb[bi, gi, top_i].reshape(B, NSA_KV_GROUPS, Tq, n_sel * NSA_BLOCK, HEAD_DIM)
    vs = vsb[bi, gi, top_i].reshape(B, NSA_KV_GROUPS, Tq, n_sel * NSA_BLOCK, HEAD_DIM)
    kpos = top_i[..., None] * NSA_BLOCK + jnp.arange(NSA_BLOCK)
    ok = ((top_s >= 0.0)[..., None] & (kpos <= qpos[:, None, None])).reshape(B, NSA_KV_GROUPS, 1, Tq, n_sel * NSA_BLOCK)
    zs = jnp.einsum('bqgrd,bgqkd->bgrqk', qg, ks, preferred_element_type=jnp.float32) * scale
    ps = jax.nn.softmax(jnp.where(ok, zs, NEG_INF), axis=-1)
    o_sel = jnp.einsum('bgrqk,bgqkd->bqgrd', ps.astype(vs.dtype), vs)
    dist = qpos[:, None] - kwpos[None, :]
    okw = (dist >= 0) & (dist <= NSA_WINDOW) & (kwpos[None, :] >= 0)
    zw = jnp.einsum('bqgrd,bkgd->bgrqk', qg, kw, preferred_element_type=jnp.float32) * scale
    pw = jax.nn.softmax(jnp.where(okw, zw, NEG_INF), axis=-1)
    o_win = jnp.einsum('bgrqk,bkgd->bqgrd', pw.astype(vw.dtype), vw)
    o = jnp.stack([o_cmp, o_sel, o_win], axis=2).reshape(B, Tq, 3, N_HEADS, HEAD_DIM)
    return jnp.einsum('bqch,bqchd->bqhd', gates.astype(o.dtype), o).reshape(B, Tq, D_MODEL)


def _nsa_prompt(h, win_buf, w_in, w_o, pe, w1, b1, w2, b2):
    B, T, _ = h.shape
    q, kv, gates = _nsa_project(h, w_in, jnp.arange(T))
    kc, vc, ksb, vsb = _nsa_context(kv[:, :, :4], pe, w1, b1, w2, b2)
    wrows = jnp.pad(kv[:, :, 4:], ((0, 0), (NSA_WINDOW, 0), (0, 0), (0, 0), (0, 0)))
    kwpos = jnp.arange(T + NSA_WINDOW) - NSA_WINDOW
    span = Q_BLOCK + NSA_WINDOW

    def blk(start):
        qb = lax.dynamic_slice_in_dim(q, start, Q_BLOCK, 1)
        gb = lax.dynamic_slice_in_dim(gates, start, Q_BLOCK, 1)
        wb = lax.dynamic_slice_in_dim(wrows, start, span, 1)
        kp = lax.dynamic_slice_in_dim(kwpos, start, span, 0)
        return _nsa_block(qb, start + jnp.arange(Q_BLOCK), gb, kc, vc, ksb, vsb, wb[:, :, 0], wb[:, :, 1], kp)

    o = lax.map(blk, jnp.arange(T // Q_BLOCK) * Q_BLOCK)
    o = o.swapaxes(0, 1).reshape(B, T, D_MODEL)
    win = kv[:, max(T - win_buf, 0):, 4:]
    if T < win_buf:
        win = jnp.pad(win, ((0, 0), (win_buf - T, 0), (0, 0), (0, 0), (0, 0)))
    return o @ w_o, kv[:, :, :4], win


def _nsa_sample(h, cache_kv, win_state, page_table, w_in, w_o, pe, w1, b1, w2, b2):
    Bd, Td, _ = h.shape
    P = page_table.shape[1] * PAGE_SIZE
    pos = P + jnp.arange(Td)
    q, kv, gates = _nsa_project(h, w_in, pos)
    past = cache_kv[page_table].reshape(Bd, P, 4, NSA_KV_GROUPS, HEAD_DIM)
    full = jnp.concatenate([past.astype(kv.dtype), kv[:, :, :4]], axis=1)
    kc, vc, ksb, vsb = _nsa_context(full, pe, w1, b1, w2, b2)
    wb = win_state.shape[1]
    wrows = jnp.concatenate([win_state.astype(kv.dtype), kv[:, :, 4:]], axis=1)
    kwpos = P - wb + jnp.arange(wb + Td)
    o = _nsa_block(q, pos, gates, kc, vc, ksb, vsb, wrows[:, :, 0], wrows[:, :, 1], kwpos)
    return o @ w_o, kv[:, :, :4], wrows[:, Td:]


def _peer_tokens(h, wq, k1, k2, u, v):
    t = h.shape[0]
    q = (h @ wq).reshape(t, PEER_HEADS, 2, PEER_DKEY // 2)
    s1 = jnp.einsum('thd,nd->thn', q[:, :, 0], k1, preferred_element_type=jnp.float32)
    s2 = jnp.einsum('thd,nd->thn', q[:, :, 1], k2, preferred_element_type=jnp.float32)
    v1, i1 = lax.top_k(s1, PEER_TOPK)
    v2, i2 = lax.top_k(s2, PEER_TOPK)
    cand = (v1[..., :, None] + v2[..., None, :]).reshape(t, PEER_HEADS, PEER_TOPK * PEER_TOPK)
    cidx = (i1[..., :, None] * PEER_NKEYS + i2[..., None, :]).reshape(t, PEER_HEADS, PEER_TOPK * PEER_TOPK)
    sc, j = lax.top_k(cand, PEER_TOPK)
    eidx = jnp.take_along_axis(cidx, j, axis=-1)
    g = jax.nn.softmax(sc, axis=-1)
    act = jax.nn.gelu(jnp.einsum('td,thkd->thk', h, u[eidx], preferred_element_type=jnp.float32), approximate=False)
    return jnp.einsum('thk,thkd->td', (g * act).astype(h.dtype), v[eidx])


def _peer(h, wq, k1, k2, u, v):
    out = _token_blocks(lambda z: _peer_tokens(z, wq, k1, k2, u, v), h.reshape(-1, D_MODEL))
    return out.reshape(h.shape)


def setup_inputs(seed: int = 0) -> dict:
    key = jax.random.key(seed)
    ks = jax.random.split(key, 40)
    f32 = jnp.float32

    def nrm(k, shape, std):
        return jax.random.normal(k, shape, f32) * std

    n_pages = PAST_LEN // PAGE_SIZE
    n_used = DEC_BATCH * n_pages
    n_pool = n_used + max(1, n_used // 4)
    win_buf = min(NSA_WINDOW, PAST_LEN)
    page_table = jax.random.permutation(ks[0], n_pool)[:n_used].reshape(DEC_BATCH, n_pages).astype(jnp.int32)
    dinv = D_MODEL ** -0.5
    sb_col = jnp.concatenate([jnp.ones((2 * D_MODEL,), f32), jnp.full((D_MODEL,), DEEPNORM_BETA, f32)]) * dinv
    kv_col = jnp.tile(jnp.repeat(jnp.array([1.0, DEEPNORM_BETA], f32), NSA_KV), 3)
    nsa_col = jnp.concatenate([jnp.ones((D_MODEL,), f32), kv_col, jnp.ones((3 * N_HEADS,), f32)]) * dinv
    return {
        'x_prompt': nrm(ks[1], (BATCH, SEQ, D_MODEL), 1.0),
        'x_sample': nrm(ks[2], (DEC_BATCH, DEC_SEQ, D_MODEL), 1.0),
        'c_prompt': nrm(ks[3], (BATCH, D_MODEL), 1.0),
        'c_sample': nrm(ks[4], (DEC_BATCH, D_MODEL), 1.0),
        'cache_l0_kv': nrm(ks[5], (n_pool, PAGE_SIZE, 2, N_HEADS, HEAD_DIM), 1.0),
        'cache_l1_kv': nrm(ks[6], (n_pool, PAGE_SIZE, 4, NSA_KV_GROUPS, HEAD_DIM), 1.0),
        'state_l1_win': nrm(ks[7], (DEC_BATCH, win_buf, 2, NSA_KV_GROUPS, HEAD_DIM), 1.0),
        'cache_l2_kv': nrm(ks[8], (n_pool, PAGE_SIZE, 2, N_HEADS, HEAD_DIM), 1.0),
        'cache_l3_kv': nrm(ks[9], (n_pool, PAGE_SIZE, 4, NSA_KV_GROUPS, HEAD_DIM), 1.0),
        'state_l3_win': nrm(ks[10], (DEC_BATCH, win_buf, 2, NSA_KV_GROUPS, HEAD_DIM), 1.0),
        'page_table': page_table,
        'ada_w': nrm(ks[11], (DEPTH, D_MODEL, 6 * D_MODEL), dinv),
        'ada_b': nrm(ks[12], (DEPTH, 6 * D_MODEL), 0.02),
        'ln1_g': 1.0 + nrm(ks[13], (DEPTH, D_MODEL), 0.02),
        'ln1_b': nrm(ks[14], (DEPTH, D_MODEL), 0.02),
        'ln2_g': 1.0 + nrm(ks[15], (DEPTH, D_MODEL), 0.02),
        'ln2_b': nrm(ks[16], (DEPTH, D_MODEL), 0.02),
        'sb_w_in': jax.random.normal(ks[17], (N_SB_LAYERS, D_MODEL, SB_IN), f32) * sb_col,
        'sb_w_o': nrm(ks[18], (N_SB_LAYERS, D_MODEL, D_MODEL), dinv * DEEPNORM_BETA),
        'nsa_w_in': jax.random.normal(ks[19], (N_NSA_LAYERS, D_MODEL, NSA_IN), f32) * nsa_col,
        'nsa_w_o': nrm(ks[20], (N_NSA_LAYERS, D_MODEL, D_MODEL), dinv * DEEPNORM_BETA),
        'nsa_cmp_pe': nrm(ks[21], (N_NSA_LAYERS, 2, NSA_BLOCK, HEAD_DIM), 0.02),
        'nsa_cmp_w1': nrm(ks[22], (N_NSA_LAYERS, 2, NSA_BLOCK * HEAD_DIM, NSA_MLP_HIDDEN), (NSA_BLOCK * HEAD_DIM) ** -0.5),
        'nsa_cmp_b1': nrm(ks[23], (N_NSA_LAYERS, 2, NSA_MLP_HIDDEN), 0.02),
        'nsa_cmp_w2': nrm(ks[24], (N_NSA_LAYERS, 2, NSA_MLP_HIDDEN, HEAD_DIM), NSA_MLP_HIDDEN ** -0.5),
        'nsa_cmp_b2': nrm(ks[25], (N_NSA_LAYERS, 2, HEAD_DIM), 0.02),
        'peer_wq': nrm(ks[26], (DEPTH, D_MODEL, PEER_HEADS * PEER_DKEY), dinv),
        'peer_k1': nrm(ks[27], (DEPTH, PEER_NKEYS, PEER_DKEY // 2), (PEER_DKEY // 2) ** -0.5),
        'peer_k2': nrm(ks[28], (DEPTH, PEER_NKEYS, PEER_DKEY // 2), (PEER_DKEY // 2) ** -0.5),
        'peer_u': nrm(ks[29], (DEPTH, PEER_EXPERTS, D_MODEL), dinv),
        'peer_v': nrm(ks[30], (DEPTH, PEER_EXPERTS, D_MODEL), DEEPNORM_BETA),
    }


def reference(x_prompt, x_sample, c_prompt, c_sample, cache_l0_kv, cache_l1_kv, state_l1_win, cache_l2_kv, cache_l3_kv,
              state_l3_win, page_table, ada_w, ada_b, ln1_g, ln1_b, ln2_g, ln2_b, sb_w_in, sb_w_o, nsa_w_in, nsa_w_o,
              nsa_cmp_pe, nsa_cmp_w1, nsa_cmp_b1, nsa_cmp_w2, nsa_cmp_b2, peer_wq, peer_k1, peer_k2, peer_u, peer_v):
    sb_caches = (cache_l0_kv, cache_l2_kv)
    nsa_caches = (cache_l1_kv, cache_l3_kv)
    nsa_wins = (state_l1_win, state_l3_win)
    xp, xs = x_prompt, x_sample
    new = []
    for i in range(DEPTH):
        j = i // N_MIXERS
        mp = _modulation(c_prompt, ada_w[i], ada_b[i])
        ms = _modulation(c_sample, ada_w[i], ada_b[i])
        hp = xp * (1.0 + mp[:, 1]) + mp[:, 0]
        hs = xs * (1.0 + ms[:, 1]) + ms[:, 0]
        if i % N_MIXERS == 0:
            yp, rows_p = _sb_prompt(hp, sb_w_in[j], sb_w_o[j])
            ys, rows_s = _sb_sample(hs, sb_caches[j], page_table, sb_w_in[j], sb_w_o[j])
            new.append((rows_p, rows_s))
        else:
            cparams = (nsa_cmp_pe[j], nsa_cmp_w1[j], nsa_cmp_b1[j], nsa_cmp_w2[j], nsa_cmp_b2[j])
            yp, rows_p, win_p = _nsa_prompt(hp, nsa_wins[j].shape[1], nsa_w_in[j], nsa_w_o[j], *cparams)
            ys, rows_s, win_s = _nsa_sample(hs, nsa_caches[j], nsa_wins[j], page_table, nsa_w_in[j], nsa_w_o[j], *cparams)
            new.append((rows_p, rows_s, win_p, win_s))
        xp = _layernorm(DEEPNORM_ALPHA * xp + mp[:, 2] * yp, ln1_g[i], ln1_b[i])
        xs = _layernorm(DEEPNORM_ALPHA * xs + ms[:, 2] * ys, ln1_g[i], ln1_b[i])
        hp = xp * (1.0 + mp[:, 4]) + mp[:, 3]
        hs = xs * (1.0 + ms[:, 4]) + ms[:, 3]
        fp = _peer(hp, peer_wq[i], peer_k1[i], peer_k2[i], peer_u[i], peer_v[i])
        fs = _peer(hs, peer_wq[i], peer_k1[i], peer_k2[i], peer_u[i], peer_v[i])
        xp = _layernorm(DEEPNORM_ALPHA * xp + mp[:, 5] * fp, ln2_g[i], ln2_b[i])
        xs = _layernorm(DEEPNORM_ALPHA * xs + ms[:, 5] * fs, ln2_g[i], ln2_b[i])
    return (xp, xs, new[0][0], new[0][1], new[1][0], new[1][1], new[1][2], new[1][3],
            new[2][0], new[2][1], new[3][0], new[3][1], new[3][2], new[3][3])
```

```python
import functools

import jax
import jax.numpy as jnp
import numpy as np
from jax import lax
from jax.experimental import pallas as pl
from jax.experimental.pallas import tpu as pltpu

F32 = jnp.float32
BF16 = jnp.bfloat16

D_MODEL = 2048
HEAD_DIM = 128
N_HEADS = D_MODEL // HEAD_DIM
PAGE_SIZE = 128
NSA_KV_GROUPS = 4
NSA_GROUP = N_HEADS // NSA_KV_GROUPS
NSA_KV = NSA_KV_GROUPS * HEAD_DIM
NSA_BLOCK = 64
NSA_TOPN = 16
NSA_WINDOW = 512
ROPE_THETA = 500000.0
ROPE_DIMS = HEAD_DIM // 4
PEER_HEADS = 8
PEER_NKEYS = 128
PEER_TOPK = 16
DEPTH = 4
DEEPNORM_ALPHA = (2 * DEPTH) ** 0.25
LN_EPS = 1e-5
NEG_INF = -1e30
FORCE_SCORE = 1e9
ATTN_SCALE = HEAD_DIM ** -0.5
SQRT_HALF = float(np.sqrt(0.5))

LANES = 128
VMEM_LIMIT_CAP = 56 * 1024 * 1024
SB_LOG_UNDERFLOW = -104.0


def _params(sem, vmem_mb):
    return pltpu.CompilerParams(
        dimension_semantics=sem, vmem_limit_bytes=min(vmem_mb * 1024 * 1024, VMEM_LIMIT_CAP))


def _dot(a, b):
    return jnp.dot(a, b, preferred_element_type=F32)


def _dot_nt(a, b):
    return lax.dot_general(a, b, (((1,), (1,)), ((), ())), preferred_element_type=F32)


def _gelu(x):
    return 0.5 * x * (1.0 + lax.erf(x * SQRT_HALF))


def _log_sigmoid_pair(z):
    l1p = jnp.log1p(jnp.exp(-jnp.abs(z)))
    return jnp.minimum(z, 0.0) - l1p, jnp.minimum(-z, 0.0) - l1p


def _ln_res(y, xres, gate, lng, lnb, scale, shift):
    v = DEEPNORM_ALPHA * xres + gate * y
    mu = jnp.mean(v, axis=-1, keepdims=True)
    d = v - mu
    var = jnp.mean(d * d, axis=-1, keepdims=True)
    xn = d * lax.rsqrt(var + LN_EPS) * lng + lnb
    return xn, (xn * (1.0 + scale) + shift).astype(BF16)


def _mod_spec(arr, blocks_per_batch):
    return pl.BlockSpec((None,) + arr.shape[1:], lambda i, *_: (i // blocks_per_batch, 0, 0))


def _modulation_kernel(c_ref, w_ref, b_ref, o_ref):
    c = c_ref[...]
    s = (c * jax.nn.sigmoid(c)).astype(BF16)
    o_ref[...] = _dot(s, w_ref[...].astype(BF16)) + b_ref[...]


def _modulation(c_all, ada_w, ada_b):
    depth, d, n = ada_w.shape
    r = c_all.shape[0]
    tn = 1024
    return pl.pallas_call(
        _modulation_kernel,
        grid=(depth, n // tn),
        in_specs=[
            pl.BlockSpec((r, d), lambda l, j: (0, 0)),
            pl.BlockSpec((None, d, tn), lambda l, j: (l, 0, j)),
            pl.BlockSpec((None, 1, tn), lambda l, j: (l, 0, j)),
        ],
        out_specs=pl.BlockSpec((None, r, tn), lambda l, j: (l, 0, j)),
        out_shape=jax.ShapeDtypeStruct((depth, r, n), F32),
        compiler_params=_params(("arbitrary", "arbitrary"), 32),
        name="modulation",
    )(c_all, ada_w, ada_b.reshape(depth, 1, n))


def _modulate_kernel(x_ref, scale_ref, shift_ref, o_ref):
    o_ref[...] = (x_ref[...] * (1.0 + scale_ref[...]) + shift_ref[...]).astype(BF16)


def _modulate(x, scale, shift, tm):
    m, d = x.shape
    bpb = (m // tm) // scale.shape[0]
    return pl.pallas_call(
        _modulate_kernel,
        grid=(m // tm,),
        in_specs=[pl.BlockSpec((tm, d), lambda i: (i, 0)), _mod_spec(scale, bpb), _mod_spec(shift, bpb)],
        out_specs=pl.BlockSpec((tm, d), lambda i: (i, 0)),
        out_shape=jax.ShapeDtypeStruct((m, d), BF16),
        compiler_params=_params(("arbitrary",), 32),
        name="modulate",
    )(x, scale, shift)


def _mm_kernel(a_ref, w_ref, o_ref):
    o_ref[...] = _dot(a_ref[...], w_ref[...]).astype(o_ref.dtype)


def _mm_sigmoid_kernel(a_ref, w_ref, o_ref):
    o_ref[...] = jax.nn.sigmoid(_dot(a_ref[...], w_ref[...]))


def _mm_rope_kernel(a_ref, w_ref, cos_ref, sa_ref, sb_ref, o_ref, *, even_only):
    acc = _dot(a_ref[...], w_ref[...])
    tn = acc.shape[1]
    reps = tn // HEAD_DIM
    cos = jnp.concatenate([cos_ref[...]] * reps, axis=1)
    sa = jnp.concatenate([sa_ref[...]] * reps, axis=1)
    sb = jnp.concatenate([sb_ref[...]] * reps, axis=1)
    half = ROPE_DIMS // 2
    rot = acc * cos + pltpu.roll(acc, tn - half, 1) * sa + pltpu.roll(acc, half, 1) * sb
    if even_only:
        rot = jnp.where(pl.program_id(1) % 2 == 0, rot, acc)
    o_ref[...] = rot


def _matmul(a, w, tm, tn, *, epilogue=None, rope=None, out_dtype=F32, name="matmul"):
    m, k = a.shape
    n = w.shape[1]
    tm = min(tm, m)
    in_specs = [pl.BlockSpec((tm, k), lambda i, j: (i, 0)), pl.BlockSpec((k, tn), lambda i, j: (0, j))]
    args = [a, w]
    if epilogue in ("rope_all", "rope_even"):
        assert epilogue == "rope_all" or tn == NSA_KV
        t_tab = rope[0].shape[0]
        nt = t_tab // tm
        in_specs += [pl.BlockSpec((tm, HEAD_DIM), lambda i, j: (i % nt, 0))] * 3
        args += list(rope)
        kern = functools.partial(_mm_rope_kernel, even_only=(epilogue == "rope_even"))
    elif epilogue == "sigmoid":
        kern = _mm_sigmoid_kernel
    else:
        kern = _mm_kernel
    return pl.pallas_call(
        kern,
        grid=(m // tm, n // tn),
        in_specs=in_specs,
        out_specs=pl.BlockSpec((tm, tn), lambda i, j: (i, j)),
        out_shape=jax.ShapeDtypeStruct((m, n), out_dtype),
        compiler_params=_params(("arbitrary", "arbitrary"), 40),
        name=name,
    )(*args)


def _mm_ln_kernel(a_ref, w_ref, x_ref, gate_ref, lng_ref, lnb_ref, scale_ref, shift_ref, xo_ref, ho_ref):
    y = _dot(a_ref[...].astype(BF16), w_ref[...])
    xn, h = _ln_res(y, x_ref[...], gate_ref[...], lng_ref[...], lnb_ref[...], scale_ref[...], shift_ref[...])
    xo_ref[...] = xn
    ho_ref[...] = h


def _matmul_ln(a, w, xres, gate, lng, lnb, scale, shift, tm):
    m, k = a.shape
    d = w.shape[1]
    tm = min(tm, m)
    bpb = (m // tm) // gate.shape[0]
    row = pl.BlockSpec((tm, d), lambda i: (i, 0))
    vec = pl.BlockSpec((1, d), lambda i: (0, 0))
    return pl.pallas_call(
        _mm_ln_kernel,
        grid=(m // tm,),
        in_specs=[pl.BlockSpec((tm, k), lambda i: (i, 0)), pl.BlockSpec((k, d), lambda i: (0, 0)), row,
                  _mod_spec(gate, bpb), vec, vec, _mod_spec(scale, bpb), _mod_spec(shift, bpb)],
        out_specs=[row, row],
        out_shape=[jax.ShapeDtypeStruct((m, d), F32), jax.ShapeDtypeStruct((m, d), BF16)],
        compiler_params=_params(("arbitrary",), 48),
        name="matmul_ln",
    )(a, w, xres, gate, lng, lnb, scale, shift)


def _suffix_matrix(n):
    return (lax.broadcasted_iota(jnp.int32, (n, n), 0) > lax.broadcasted_iota(jnp.int32, (n, n), 1)).astype(BF16)


def _suffix_sum(x, lmat):
    hi = x.astype(BF16)
    lo = (x - hi.astype(F32)).astype(BF16)
    return _dot(hi, lmat) + _dot(lo, lmat)


def _sb_prompt_kernel(q_ref, k_ref, v_ref, o_ref, carry_ref, acc_ref, *, tq):
    i = pl.program_id(2)
    q = q_ref[...].astype(BF16)
    carry_ref[...] = jnp.zeros_like(carry_ref)
    acc_ref[...] = jnp.zeros_like(acc_ref)
    qpos = i * tq + lax.broadcasted_iota(jnp.int32, (tq, 1), 0)
    lmat = _suffix_matrix(tq)

    def body(state):
        j, _ = state
        k0 = pl.multiple_of(j * tq, tq)
        k = k_ref[pl.ds(k0, tq), :].astype(BF16)
        v = v_ref[pl.ds(k0, tq), :].astype(BF16)
        z = _dot_nt(q, k) * ATTN_SCALE
        allowed = (k0 + lax.broadcasted_iota(jnp.int32, (1, tq), 1)) < qpos
        ls, lsn = _log_sigmoid_pair(z)
        log_keep = jnp.where(allowed, lsn, 0.0)
        later = _suffix_sum(log_keep, lmat) + carry_ref[...]
        a = jnp.where(allowed, jnp.exp(ls + later), 0.0)
        acc_ref[...] += _dot(a.astype(BF16), v)
        carry = carry_ref[...] + jnp.sum(log_keep, axis=1, keepdims=True)
        carry_ref[...] = carry
        return j - 1, (jnp.max(carry) > SB_LOG_UNDERFLOW).astype(jnp.int32)

    lax.while_loop(lambda s: jnp.logical_and(s[0] >= 0, s[1] > 0), body, (i, jnp.int32(1)))
    o_ref[...] = acc_ref[...]


def _sb_prompt_attention(q, kv, batch, seq, tq=256):
    nq = seq // tq
    return pl.pallas_call(
        functools.partial(_sb_prompt_kernel, tq=tq),
        grid=(batch, N_HEADS, nq),
        in_specs=[
            pl.BlockSpec((tq, HEAD_DIM), lambda b, h, i: (b * nq + i, h)),
            pl.BlockSpec((seq, HEAD_DIM), lambda b, h, i: (b, h)),
            pl.BlockSpec((seq, HEAD_DIM), lambda b, h, i: (b, N_HEADS + h)),
        ],
        out_specs=pl.BlockSpec((tq, HEAD_DIM), lambda b, h, i: (b * nq + i, h)),
        out_shape=jax.ShapeDtypeStruct(q.shape, F32),
        scratch_shapes=[pltpu.VMEM((tq, 1), F32), pltpu.VMEM((tq, HEAD_DIM), F32)],
        compiler_params=_params(("arbitrary", "arbitrary", "arbitrary"), 32),
        name="sb_prompt_attention",
    )(q, kv, kv)


def _head_mask():
    rows = lax.broadcasted_iota(jnp.int32, (N_HEADS, D_MODEL), 0)
    lanes = lax.broadcasted_iota(jnp.int32, (N_HEADS, D_MODEL), 1)
    return lanes // HEAD_DIM == rows


def _sb_sample_kernel(pt_ref, q_ref, cache_ref, o_ref, kbuf, vbuf, sem, carry_ref, acc_ref, *, n_pages):
    b = pl.program_id(0)
    hmask = _head_mask()
    qbd = jnp.where(hmask, jnp.broadcast_to(q_ref[...], (N_HEADS, D_MODEL)), 0.0).astype(BF16)
    lmat = _suffix_matrix(PAGE_SIZE)
    carry_ref[...] = jnp.zeros_like(carry_ref)
    acc_ref[...] = jnp.zeros_like(acc_ref)

    def page_copies(page):
        return (pltpu.make_async_copy(cache_ref.at[page, :, pl.ds(0, D_MODEL)], kbuf, sem.at[0]),
                pltpu.make_async_copy(cache_ref.at[page, :, pl.ds(D_MODEL, D_MODEL)], vbuf, sem.at[1]))

    def body(state):
        p, _ = state
        ck, cv = page_copies(pt_ref[b * n_pages + p])
        ck.start()
        cv.start()
        ck.wait()
        cv.wait()
        z = _dot_nt(qbd, kbuf[...].astype(BF16)) * ATTN_SCALE
        ls, lsn = _log_sigmoid_pair(z)
        later = _suffix_sum(lsn, lmat) + carry_ref[...]
        a = jnp.exp(ls + later)
        acc_ref[...] += _dot(a.astype(BF16), vbuf[...].astype(BF16))
        carry = carry_ref[...] + jnp.sum(lsn, axis=1, keepdims=True)
        carry_ref[...] = carry
        return p - 1, (jnp.max(carry) > SB_LOG_UNDERFLOW).astype(jnp.int32)

    lax.while_loop(lambda s: jnp.logical_and(s[0] >= 0, s[1] > 0), body, (jnp.int32(n_pages - 1), jnp.int32(1)))
    o_ref[...] = jnp.sum(jnp.where(hmask, acc_ref[...], 0.0), axis=0, keepdims=True)


def _sb_sample_attention(q, cache_kv, page_table):
    bd, n_pages = page_table.shape
    cache = cache_kv.reshape(cache_kv.shape[0], PAGE_SIZE, 2 * D_MODEL)
    out = pl.pallas_call(
        functools.partial(_sb_sample_kernel, n_pages=n_pages),
        grid_spec=pltpu.PrefetchScalarGridSpec(
            num_scalar_prefetch=1,
            grid=(bd,),
            in_specs=[pl.BlockSpec((None, 1, D_MODEL), lambda b, pt: (b, 0, 0)),
                      pl.BlockSpec(memory_space=pl.ANY)],
            out_specs=pl.BlockSpec((None, 1, D_MODEL), lambda b, pt: (b, 0, 0)),
            scratch_shapes=[pltpu.VMEM((PAGE_SIZE, D_MODEL), F32), pltpu.VMEM((PAGE_SIZE, D_MODEL), F32),
                            pltpu.SemaphoreType.DMA((2,)),
                            pltpu.VMEM((N_HEADS, 1), F32), pltpu.VMEM((N_HEADS, D_MODEL), F32)]),
        out_shape=jax.ShapeDtypeStruct((bd, 1, D_MODEL), F32),
        compiler_params=_params(("arbitrary",), 16),
        name="sb_sample_attention",
    )(page_table.reshape(-1), q.reshape(bd, 1, D_MODEL), cache)
    return out.reshape(bd, D_MODEL)


def _compress_kernel(x_ref, pe_ref, w1_ref, b1_ref, w2_ref, b2_ref, o_ref):
    acc = jnp.zeros((x_ref.shape[0], w1_ref.shape[1]), F32)
    for t in range(NSA_BLOCK):
        xt = (x_ref[:, t, :] + pe_ref[t:t + 1, :]).astype(BF16)
        acc = acc + _dot(xt, w1_ref[t * HEAD_DIM:(t + 1) * HEAD_DIM, :])
    hid = _gelu(acc + b1_ref[...])
    o_ref[...] = _dot(hid.astype(BF16), w2_ref[...].astype(BF16)) + b2_ref[...]


def _compress(x, pe, w1, b1, w2, b2, tr):
    r = x.shape[0]
    tr = min(tr, r)
    hidden = w1.shape[2]
    return pl.pallas_call(
        _compress_kernel,
        grid=(2, NSA_KV_GROUPS, r // tr),
        in_specs=[
            pl.BlockSpec((tr, NSA_BLOCK, HEAD_DIM), lambda s, g, i: (i, 0, s * NSA_KV_GROUPS + g)),
            pl.BlockSpec((None, NSA_BLOCK, HEAD_DIM), lambda s, g, i: (s, 0, 0)),
            pl.BlockSpec((None, NSA_BLOCK * HEAD_DIM, hidden), lambda s, g, i: (s, 0, 0)),
            pl.BlockSpec((None, 1, hidden), lambda s, g, i: (s, 0, 0)),
            pl.BlockSpec((None, hidden, HEAD_DIM), lambda s, g, i: (s, 0, 0)),
            pl.BlockSpec((None, 1, HEAD_DIM), lambda s, g, i: (s, 0, 0)),
        ],
        out_specs=pl.BlockSpec((None, tr, HEAD_DIM), lambda s, g, i: (s, i, g)),
        out_shape=jax.ShapeDtypeStruct((2, r, NSA_KV), F32),
        compiler_params=_params(("arbitrary", "arbitrary", "arbitrary"), 48),
        name="nsa_compress",
    )(x, pe, w1, b1.reshape(2, 1, hidden), w2, b2.reshape(2, 1, HEAD_DIM))


def _page_gather_kernel(pt_ref, *refs):
    o_ref = refs[-1]
    for k, r in enumerate(refs[:-1]):
        o_ref[k * PAGE_SIZE:(k + 1) * PAGE_SIZE, :] = r[...]


def _gather_cmp_pages(cache_kv, page_table, pages_per_step=8):
    bd, n_pages = page_table.shape
    width = 2 * NSA_KV
    cache = cache_kv.reshape(cache_kv.shape[0], PAGE_SIZE, 2 * width)
    pps = pages_per_step

    def in_map(k):
        return lambda b, c, pt: (pt[b * n_pages + c * pps + k], 0, 0)

    return pl.pallas_call(
        _page_gather_kernel,
        grid_spec=pltpu.PrefetchScalarGridSpec(
            num_scalar_prefetch=1,
            grid=(bd, n_pages // pps),
            in_specs=[pl.BlockSpec((None, PAGE_SIZE, width), in_map(k)) for k in range(pps)],
            out_specs=pl.BlockSpec((None, pps * PAGE_SIZE, width), lambda b, c, pt: (b, c, 0))),
        out_shape=jax.ShapeDtypeStruct((bd, n_pages * PAGE_SIZE, width), F32),
        compiler_params=_params(("arbitrary", "arbitrary"), 32),
        name="nsa_page_gather",
    )(page_table.reshape(-1), *([cache] * pps))


def _online_softmax_step(z, valid, v_t, m_ref, l_ref, acc_ref):
    zm = jnp.where(valid, z, NEG_INF)
    m_new = jnp.maximum(m_ref[...], jnp.max(zm, axis=0, keepdims=True))
    alpha = jnp.exp(m_ref[...] - m_new)
    p = jnp.where(valid, jnp.exp(zm - m_new), 0.0)
    l_ref[...] = alpha * l_ref[...] + jnp.sum(p, axis=0, keepdims=True)
    acc_ref[...] = alpha * acc_ref[...] + _dot(v_t, p.astype(BF16))
    m_ref[...] = m_new


def _nsa_prompt_kernel(q_ref, kc_ref, vc_ref, ks_ref, vs_ref, kw_ref, vw_ref, gate_ref, o_ref,
                       m_ref, l_ref, acc_ref, osel_ref, gate_t_ref, *, nb, tq, tk):
    g = pl.program_id(1)
    i = pl.program_id(2)
    nq = NSA_GROUP * tq
    nbp = kc_ref.shape[0]
    qblk = q_ref[...]
    qg = jnp.concatenate([qblk[:, r * HEAD_DIM:(r + 1) * HEAD_DIM] for r in range(NSA_GROUP)], axis=0).astype(BF16)
    qpos = i * tq + lax.broadcasted_iota(jnp.int32, (1, tq), 1)
    tile4 = lambda x: jnp.concatenate([x] * NSA_GROUP, axis=1)

    blk = lax.broadcasted_iota(jnp.int32, (nbp, 1), 0)
    zc = _dot_nt(kc_ref[...].astype(BF16), qg) * ATTN_SCALE
    avail = tile4(jnp.logical_and((blk + 1) * NSA_BLOCK - 1 <= qpos, blk < nb))
    zm = jnp.where(avail, zc, NEG_INF)
    e = jnp.where(avail, jnp.exp(zm - jnp.max(zm, axis=0, keepdims=True)), 0.0)
    den = jnp.sum(e, axis=0, keepdims=True)
    pc = e / jnp.where(den > 0.0, den, 1.0)
    o_cmp = _dot(jnp.transpose(vc_ref[...]).astype(BF16), pc.astype(BF16))

    score = pc[:, 0:tq]
    for r in range(1, NSA_GROUP):
        score = score + pc[:, r * tq:(r + 1) * tq]
    cur = qpos // NSA_BLOCK
    forced = jnp.logical_or(blk == 0, jnp.logical_or(blk == cur, blk == cur - 1))
    s = jnp.where(blk <= cur, jnp.where(forced, FORCE_SCORE, score), -1.0)
    rank = jnp.zeros((nbp, tq), F32)
    for mth in range(nb):
        sm = s[mth:mth + 1, :]
        ahead = jnp.logical_or(sm > s, jnp.logical_and(sm == s, mth < blk))
        rank = rank + jnp.where(ahead, 1.0, 0.0)
    sel = jnp.where(jnp.logical_and(rank < float(min(NSA_TOPN, nb)), s >= 0.0), 1.0, 0.0).astype(BF16)

    def reset():
        m_ref[...] = jnp.full_like(m_ref, NEG_INF)
        l_ref[...] = jnp.zeros_like(l_ref)
        acc_ref[...] = jnp.zeros_like(acc_ref)

    reset()
    bpt = tk // NSA_BLOCK

    def sel_body(kt, carry):
        k0 = pl.multiple_of(kt * tk, tk)
        z = _dot_nt(ks_ref[pl.ds(k0, tk), :].astype(BF16), qg) * ATTN_SCALE
        key = lax.broadcasted_iota(jnp.int32, (tk, nbp), 0)
        col = lax.broadcasted_iota(jnp.int32, (tk, nbp), 1)
        expand = (col == kt * bpt + key // NSA_BLOCK).astype(BF16)
        chosen = _dot(expand, sel) > 0.5
        kpos = k0 + lax.broadcasted_iota(jnp.int32, (tk, 1), 0)
        valid = tile4(jnp.logical_and(chosen, kpos <= qpos))
        v_t = jnp.transpose(vs_ref[pl.ds(k0, tk), :]).astype(BF16)
        _online_softmax_step(z, valid, v_t, m_ref, l_ref, acc_ref)
        return carry

    lax.fori_loop(0, (i * tq + tq - 1) // tk + 1, sel_body, 0)
    osel_ref[...] = acc_ref[...] / l_ref[...]

    reset()

    def win_body(kt, carry):
        k0 = pl.multiple_of(kt * tq, tq)
        z = _dot_nt(kw_ref[pl.ds(k0, tq), :].astype(BF16), qg) * ATTN_SCALE
        dist = qpos - (k0 + lax.broadcasted_iota(jnp.int32, (tq, 1), 0))
        valid = tile4(jnp.logical_and(dist >= 0, dist <= NSA_WINDOW))
        v_t = jnp.transpose(vw_ref[pl.ds(k0, tq), :]).astype(BF16)
        _online_softmax_step(z, valid, v_t, m_ref, l_ref, acc_ref)
        return carry

    lax.fori_loop(jnp.maximum(i - NSA_WINDOW // tq, 0), i + 1, win_body, 0)
    o_win = acc_ref[...] / l_ref[...]
    o_sel = osel_ref[...]

    gate_t_ref[...] = jnp.transpose(gate_ref[...])
    for r in range(NSA_GROUP):
        lanes = slice(r * tq, (r + 1) * tq)
        out_t = jnp.zeros((HEAD_DIM, tq), F32)
        for c, o_c in enumerate((o_cmp, o_sel, o_win)):
            gate = gate_t_ref[pl.ds(c * N_HEADS + g * NSA_GROUP + r, 1), :]
            out_t = out_t + gate * o_c[:, lanes]
        o_ref[:, r * HEAD_DIM:(r + 1) * HEAD_DIM] = jnp.transpose(out_t)


def _nsa_prompt_attention(q, kv4, kwin, gates, kc, vc, batch, seq, nb):
    tq = 128
    tk = 256
    nq = seq // tq
    g4 = NSA_KV_GROUPS
    col = lambda off: pl.BlockSpec((seq, HEAD_DIM), lambda b, g, i: (b, off + g))
    cmp_spec = pl.BlockSpec((None, kc.shape[1], HEAD_DIM), lambda b, g, i: (b, 0, g))
    return pl.pallas_call(
        functools.partial(_nsa_prompt_kernel, nb=nb, tq=tq, tk=tk),
        grid=(batch, g4, nq),
        in_specs=[
            pl.BlockSpec((tq, NSA_KV), lambda b, g, i: (b * nq + i, g)),
            cmp_spec, cmp_spec,
            col(2 * g4), col(3 * g4),
            col(0), col(g4),
            pl.BlockSpec((tq, LANES), lambda b, g, i: (b * nq + i, 0)),
        ],
        out_specs=pl.BlockSpec((tq, NSA_KV), lambda b, g, i: (b * nq + i, g)),
        out_shape=jax.ShapeDtypeStruct(q.shape, F32),
        scratch_shapes=[pltpu.VMEM((1, NSA_GROUP * tq), F32), pltpu.VMEM((1, NSA_GROUP * tq), F32),
                        pltpu.VMEM((HEAD_DIM, NSA_GROUP * tq), F32), pltpu.VMEM((HEAD_DIM, NSA_GROUP * tq), F32),
                        pltpu.VMEM((LANES, tq), F32)],
        compiler_params=_params(("arbitrary", "arbitrary", "arbitrary"), 40),
        name="nsa_prompt_attention",
    )(q, kc, vc, kv4, kv4, kwin, kwin, gates)


def _nsa_sample_select_kernel(q_ref, kc_ref, vc_ref, ocmp_ref, idx_ref, *, nb, qpos):
    nbp = kc_ref.shape[0]
    q = q_ref[...]
    lane_blk = lax.broadcasted_iota(jnp.int32, (1, nbp), 1)
    row_i = lax.broadcasted_iota(jnp.int32, (nbp, nbp), 0)
    col_i = lax.broadcasted_iota(jnp.int32, (nbp, nbp), 1)
    eye = row_i == col_i
    cur = qpos // NSA_BLOCK
    avail = jnp.logical_and((lane_blk + 1) * NSA_BLOCK - 1 <= qpos, lane_blk < nb)
    forced = jnp.logical_or(lane_blk == 0, jnp.logical_or(lane_blk == cur, lane_blk == cur - 1))
    for g in range(NSA_KV_GROUPS):
        qg = jnp.concatenate(
            [q[:, (g * NSA_GROUP + r) * HEAD_DIM:(g * NSA_GROUP + r + 1) * HEAD_DIM] for r in range(NSA_GROUP)],
            axis=0).astype(BF16)
        lanes = slice(g * HEAD_DIM, (g + 1) * HEAD_DIM)
        zc = _dot_nt(qg, kc_ref[:, lanes].astype(BF16)) * ATTN_SCALE
        zm = jnp.where(avail, zc, NEG_INF)
        e = jnp.where(avail, jnp.exp(zm - jnp.max(zm, axis=1, keepdims=True)), 0.0)
        den = jnp.sum(e, axis=1, keepdims=True)
        pc = e / jnp.where(den > 0.0, den, 1.0)
        o = _dot(pc.astype(BF16), vc_ref[:, lanes].astype(BF16))
        for r in range(NSA_GROUP):
            h = g * NSA_GROUP + r
            ocmp_ref[:, h * HEAD_DIM:(h + 1) * HEAD_DIM] = o[r:r + 1, :]
        score = jnp.sum(pc, axis=0, keepdims=True)
        s_row = jnp.where(lane_blk <= cur, jnp.where(forced, FORCE_SCORE, score), -1.0)
        s_mat = jnp.broadcast_to(s_row, (nbp, nbp))
        s_col = jnp.sum(jnp.where(eye, s_mat, 0.0), axis=1, keepdims=True)
        ahead = jnp.logical_or(s_mat > s_col, jnp.logical_and(s_mat == s_col, col_i < row_i))
        rank = jnp.sum(jnp.where(ahead, 1.0, 0.0), axis=1, keepdims=True)
        sel_col = jnp.where(jnp.logical_and(rank < float(min(NSA_TOPN, nb)), s_col >= 0.0), 1.0, 0.0)
        sel_row = jnp.sum(jnp.where(eye, jnp.broadcast_to(sel_col, (nbp, nbp)), 0.0), axis=0, keepdims=True)
        slot = jnp.sum(jnp.where(col_i < row_i, jnp.broadcast_to(sel_row, (nbp, nbp)), 0.0), axis=1, keepdims=True)
        k_i = lax.broadcasted_iota(jnp.int32, (nbp, LANES), 1).astype(F32)
        n_i = lax.broadcasted_iota(jnp.int32, (nbp, LANES), 0).astype(F32)
        hit = jnp.logical_and(sel_col > 0.5, slot == k_i)
        idx_ref[g:g + 1, :] = jnp.sum(jnp.where(hit, n_i, 0.0), axis=0, keepdims=True).astype(jnp.int32)


def _nsa_sample_select(q, kc, vc, nb, qpos):
    bd = q.shape[0]
    nbp = kc.shape[1]
    cmp_spec = pl.BlockSpec((None, nbp, NSA_KV), lambda b: (b, 0, 0))
    ocmp, idx = pl.pallas_call(
        functools.partial(_nsa_sample_select_kernel, nb=nb, qpos=qpos),
        grid=(bd,),
        in_specs=[pl.BlockSpec((None, 1, D_MODEL), lambda b: (b, 0, 0)), cmp_spec, cmp_spec],
        out_specs=[pl.BlockSpec((None, 1, D_MODEL), lambda b: (b, 0, 0)),
                   pl.BlockSpec((None, NSA_KV_GROUPS, LANES), lambda b: (b, 0, 0))],
        out_shape=[jax.ShapeDtypeStruct((bd, 1, D_MODEL), F32),
                   jax.ShapeDtypeStruct((bd, NSA_KV_GROUPS, LANES), jnp.int32)],
        compiler_params=_params(("arbitrary",), 32),
        name="nsa_sample_select",
    )(q.reshape(bd, 1, D_MODEL), kc, vc)
    return ocmp, idx[:, :, :NSA_TOPN]


def _nsa_sample_attend_kernel(idx_ref, pt_ref, q_ref, ocmp_ref, gate_ref, kv_ref, kwin_ref, win_ref, cache_ref,
                              o_ref, kbuf, vbuf, sem, *, n_pages, n_past_blocks):
    b = pl.program_id(0)
    g4 = NSA_KV_GROUPS
    bpp = PAGE_SIZE // NSA_BLOCK

    def block_copies(g, k):
        blk = jnp.minimum(idx_ref[(b * g4 + g) * NSA_TOPN + k], n_past_blocks - 1)
        page = pt_ref[b * n_pages + blk // bpp]
        rows = pl.ds((blk % bpp) * NSA_BLOCK, NSA_BLOCK)
        dst = pl.ds(k * NSA_BLOCK, NSA_BLOCK)
        return (pltpu.make_async_copy(cache_ref.at[page, rows, pl.ds((2 * g4 + g) * HEAD_DIM, HEAD_DIM)],
                                      kbuf.at[g, dst, :], sem.at[0, g, k]),
                pltpu.make_async_copy(cache_ref.at[page, rows, pl.ds((3 * g4 + g) * HEAD_DIM, HEAD_DIM)],
                                      vbuf.at[g, dst, :], sem.at[1, g, k]))

    for g in range(g4):
        for k in range(NSA_TOPN):
            ck, cv = block_copies(g, k)
            ck.start()
            cv.start()
    for g in range(g4):
        for k in range(NSA_TOPN):
            ck, cv = block_copies(g, k)
            ck.wait()
            cv.wait()

    q = q_ref[...]
    ocmp = ocmp_ref[...]
    gates = gate_ref[...]
    kv_new = kv_ref[...]
    kwin_new = kwin_ref[...]
    n_sel = NSA_TOPN * NSA_BLOCK
    key_slot = lax.broadcasted_iota(jnp.int32, (1, n_sel), 1) // NSA_BLOCK
    lane = lax.broadcasted_iota(jnp.int32, (NSA_GROUP, LANES), 1)
    sub = lax.broadcasted_iota(jnp.int32, (NSA_GROUP, LANES), 0)
    bf = lambda x: x.astype(BF16).astype(F32)

    def attend(qg, keys, vals, valid, k_new, v_new, new_on):
        z = _dot_nt(qg, keys.astype(BF16)) * ATTN_SCALE
        z_new = jnp.sum(bf(qg) * bf(k_new), axis=1, keepdims=True) * ATTN_SCALE
        zm = z if valid is None else jnp.where(valid, z, NEG_INF)
        zn = jnp.where(new_on, z_new, NEG_INF)
        m = jnp.maximum(jnp.max(zm, axis=1, keepdims=True), zn)
        p = jnp.exp(zm - m) if valid is None else jnp.where(valid, jnp.exp(zm - m), 0.0)
        p_new = jnp.where(new_on, jnp.exp(zn - m), 0.0)
        den = jnp.sum(p, axis=1, keepdims=True) + p_new
        return (_dot(p.astype(BF16), vals.astype(BF16)) + bf(p_new) * bf(v_new)) / den

    for g in range(g4):
        qg = jnp.concatenate(
            [q[:, (g * NSA_GROUP + r) * HEAD_DIM:(g * NSA_GROUP + r + 1) * HEAD_DIM] for r in range(NSA_GROUP)],
            axis=0).astype(BF16)
        o_cmp = jnp.concatenate(
            [ocmp[:, (g * NSA_GROUP + r) * HEAD_DIM:(g * NSA_GROUP + r + 1) * HEAD_DIM] for r in range(NSA_GROUP)],
            axis=0)
        valid = jnp.zeros((1, n_sel), jnp.int32)
        has_new = jnp.int32(0)
        for k in range(NSA_TOPN):
            is_new = (idx_ref[(b * g4 + g) * NSA_TOPN + k] >= n_past_blocks).astype(jnp.int32)
            valid = jnp.where(key_slot == k, 1 - is_new, valid)
            has_new = jnp.maximum(has_new, is_new)
        lanes = lambda slot: slice((slot * g4 + g) * HEAD_DIM, (slot * g4 + g + 1) * HEAD_DIM)
        o_sel = attend(qg, kbuf[g], vbuf[g], valid > 0, kv_new[:, lanes(2)], kv_new[:, lanes(3)], has_new > 0)
        o_win = attend(qg, win_ref[:, lanes(0)], win_ref[:, lanes(1)], None,
                       kwin_new[:, lanes(0)], kwin_new[:, lanes(1)], True)
        out = jnp.zeros((NSA_GROUP, HEAD_DIM), F32)
        for c, o_c in enumerate((o_cmp, o_sel, o_win)):
            pick = lane == c * N_HEADS + g * NSA_GROUP + sub
            gate = jnp.sum(jnp.where(pick, jnp.broadcast_to(gates, (NSA_GROUP, LANES)), 0.0), axis=1, keepdims=True)
            out = out + gate * o_c
        for r in range(NSA_GROUP):
            h = g * NSA_GROUP + r
            o_ref[:, h * HEAD_DIM:(h + 1) * HEAD_DIM] = out[r:r + 1, :]


def _nsa_sample_attend(idx, page_table, q, ocmp, gates, kv4, kwin, win_state, cache_kv):
    bd, n_pages = page_table.shape
    g4 = NSA_KV_GROUPS
    cache = cache_kv.reshape(cache_kv.shape[0], PAGE_SIZE, 4 * NSA_KV)
    win = win_state.reshape(bd, win_state.shape[1], 2 * NSA_KV)
    row = lambda w: pl.BlockSpec((None, 1, w), lambda b, *_: (b, 0, 0))
    out = pl.pallas_call(
        functools.partial(_nsa_sample_attend_kernel, n_pages=n_pages, n_past_blocks=n_pages * PAGE_SIZE // NSA_BLOCK),
        grid_spec=pltpu.PrefetchScalarGridSpec(
            num_scalar_prefetch=2,
            grid=(bd,),
            in_specs=[row(D_MODEL), row(D_MODEL), row(LANES), row(4 * NSA_KV), row(2 * NSA_KV),
                      pl.BlockSpec((None, win.shape[1], 2 * NSA_KV), lambda b, *_: (b, 0, 0)),
                      pl.BlockSpec(memory_space=pl.ANY)],
            out_specs=row(D_MODEL),
            scratch_shapes=[pltpu.VMEM((g4, NSA_TOPN * NSA_BLOCK, HEAD_DIM), F32),
                            pltpu.VMEM((g4, NSA_TOPN * NSA_BLOCK, HEAD_DIM), F32),
                            pltpu.SemaphoreType.DMA((2, g4, NSA_TOPN))]),
        out_shape=jax.ShapeDtypeStruct((bd, 1, D_MODEL), F32),
        compiler_params=_params(("arbitrary",), 32),
        name="nsa_sample_attend",
    )(idx.reshape(-1), page_table.reshape(-1), q.reshape(bd, 1, D_MODEL), ocmp, gates.reshape(bd, 1, LANES),
      kv4.reshape(bd, 1, 4 * NSA_KV), kwin.reshape(bd, 1, 2 * NSA_KV), win, cache)
    return out.reshape(bd, D_MODEL)


def _topk_rows(s, k):
    n = s.shape[0]
    rows = lax.broadcasted_iota(jnp.int32, s.shape, 0).astype(F32)
    vals, ids = [], []
    for _ in range(k):
        m = jnp.max(s, axis=0, keepdims=True)
        first = jnp.min(jnp.where(s == m, rows, float(n)), axis=0, keepdims=True)
        s = jnp.where(rows == first, -jnp.inf, s)
        vals.append(m)
        ids.append(first)
    return jnp.concatenate(vals, axis=0), jnp.concatenate(ids, axis=0)


def _pick_rows(sel, table):
    out = jnp.zeros_like(sel)
    for k in range(table.shape[0]):
        out = jnp.where(sel == float(k), table[k:k + 1, :], out)
    return out


def _peer_route_kernel(h_ref, wq_ref, k1_ref, k2_ref, i1_ref, i2_ref, g_ref, qh_ref, i1_t, i2_t, g_t):
    qh_ref[...] = _dot(h_ref[...], wq_ref[...])
    k1 = k1_ref[...].astype(BF16)
    k2 = k2_ref[...].astype(BF16)
    half = PEER_NKEYS

    def head(h, carry):
        c0 = pl.multiple_of(h * 2 * half, 2 * half)
        s1 = _dot_nt(k1, qh_ref[:, pl.ds(c0, half)].astype(BF16))
        s2 = _dot_nt(k2, qh_ref[:, pl.ds(c0 + half, half)].astype(BF16))
        v1, i1 = _topk_rows(s1, PEER_TOPK)
        v2, i2 = _topk_rows(s2, PEER_TOPK)
        cand = jnp.concatenate([v1[k:k + 1, :] + v2 for k in range(PEER_TOPK)], axis=0)
        sc, j = _topk_rows(cand, PEER_TOPK)
        ja = jnp.floor(j * (1.0 / PEER_TOPK))
        jb = j - ja * PEER_TOPK
        e = jnp.exp(sc - jnp.max(sc, axis=0, keepdims=True))
        r0 = pl.multiple_of(h * PEER_TOPK, PEER_TOPK)
        i1_t[pl.ds(r0, PEER_TOPK), :] = _pick_rows(ja, i1)
        i2_t[pl.ds(r0, PEER_TOPK), :] = _pick_rows(jb, i2)
        g_t[pl.ds(r0, PEER_TOPK), :] = e / jnp.sum(e, axis=0, keepdims=True)
        return carry

    lax.fori_loop(0, PEER_HEADS, head, 0)
    i1_ref[...] = jnp.transpose(i1_t[...])
    i2_ref[...] = jnp.transpose(i2_t[...])
    g_ref[...] = jnp.transpose(g_t[...])


def _peer_route(h, wq, k1, k2, tm):
    m, d = h.shape
    tm = min(tm, m)
    hk = PEER_HEADS * PEER_TOPK
    out = pl.BlockSpec((tm, hk), lambda i: (i, 0))
    keys = pl.BlockSpec(k1.shape, lambda i: (0, 0))
    return pl.pallas_call(
        _peer_route_kernel,
        grid=(m // tm,),
        in_specs=[pl.BlockSpec((tm, d), lambda i: (i, 0)), pl.BlockSpec(wq.shape, lambda i: (0, 0)), keys, keys],
        out_specs=[out, out, out],
        out_shape=[jax.ShapeDtypeStruct((m, hk), F32)] * 3,
        scratch_shapes=[pltpu.VMEM((tm, wq.shape[1]), F32)] + [pltpu.VMEM((hk, tm), F32)] * 3,
        compiler_params=_params(("arbitrary",), 40),
        name="peer_route",
    )(h, wq, k1, k2)


def _peer_weights_kernel(i1_ref, i2_ref, g_ref, w_ref):
    tm = i1_ref.shape[0]
    hk = i1_ref.shape[1]
    key = lax.broadcasted_iota(jnp.int32, (PEER_NKEYS, hk), 0).astype(F32)

    def body(t, carry):
        i1 = jnp.broadcast_to(i1_ref[pl.ds(t, 1), :], (PEER_NKEYS, hk))
        i2 = jnp.broadcast_to(i2_ref[pl.ds(t, 1), :], (PEER_NKEYS, hk))
        gt = jnp.broadcast_to(g_ref[pl.ds(t, 1), :], (PEER_NKEYS, hk))
        p = jnp.where(i1 == key, 1.0, 0.0).astype(BF16)
        q = jnp.where(i2 == key, gt, 0.0).astype(BF16)
        w_ref[t] = _dot_nt(p, q).astype(BF16)
        return carry

    lax.fori_loop(0, tm, body, 0)


def _peer_weights(i1, i2, g, tm):
    m, hk = i1.shape
    tm = min(tm, m)
    spec = pl.BlockSpec((tm, hk), lambda i: (i, 0))
    return pl.pallas_call(
        _peer_weights_kernel,
        grid=(m // tm,),
        in_specs=[spec, spec, spec],
        out_specs=pl.BlockSpec((tm, PEER_NKEYS, PEER_NKEYS), lambda i: (i, 0, 0)),
        out_shape=jax.ShapeDtypeStruct((m, PEER_NKEYS, PEER_NKEYS), BF16),
        compiler_params=_params(("arbitrary",), 32),
        name="peer_weights",
    )(i1, i2, g)


def _peer_dense_kernel(h_ref, u_ref, v_ref, w_ref, x_ref, gate_ref, lng_ref, lnb_ref, scale_ref, shift_ref,
                       xo_ref, ho_ref, acc_ref):
    c = pl.program_id(1)

    @pl.when(c == 0)
    def _():
        acc_ref[...] = jnp.zeros_like(acc_ref)

    act = _gelu(_dot_nt(h_ref[...], u_ref[...]))
    acc_ref[...] += _dot((act * w_ref[...].astype(F32)).astype(BF16), v_ref[...])

    @pl.when(c == pl.num_programs(1) - 1)
    def _():
        xn, h = _ln_res(acc_ref[...], x_ref[...], gate_ref[...], lng_ref[...], lnb_ref[...],
                        scale_ref[...], shift_ref[...])
        xo_ref[...] = xn
        ho_ref[...] = h


def _peer_dense(h, u, v, w, xres, gate, lng, lnb, scale, shift, tm, te):
    m, d = h.shape
    n_exp = u.shape[0]
    tm = min(tm, m)
    bpb = (m // tm) // gate.shape[0]
    row = pl.BlockSpec((tm, d), lambda i, c: (i, 0))
    tab = pl.BlockSpec((te, d), lambda i, c: (c, 0))
    vec = pl.BlockSpec((1, d), lambda i, c: (0, 0))
    return pl.pallas_call(
        _peer_dense_kernel,
        grid=(m // tm, n_exp // te),
        in_specs=[row, tab, tab, pl.BlockSpec((tm, te), lambda i, c: (i, c)), row,
                  _mod_spec(gate, bpb), vec, vec, _mod_spec(scale, bpb), _mod_spec(shift, bpb)],
        out_specs=[row, row],
        out_shape=[jax.ShapeDtypeStruct((m, d), F32), jax.ShapeDtypeStruct((m, d), BF16)],
        scratch_shapes=[pltpu.VMEM((tm, d), F32)],
        compiler_params=_params(("arbitrary", "arbitrary"), 56),
        name="peer_dense",
    )(h, u, v, w, xres, gate, lng, lnb, scale, shift)


def _peer(h, xres, wq, k1, k2, u, v, gate, lng, lnb, scale, shift, tm_route, tm_dense, te):
    i1, i2, g = _peer_route(h, wq, k1, k2, tm_route)
    w = _peer_weights(i1, i2, g, tm_route).reshape(h.shape[0], PEER_NKEYS * PEER_NKEYS)
    return _peer_dense(h, u, v, w, xres, gate, lng, lnb, scale, shift, tm_dense, te)


def _rope_tables(pos):
    half = ROPE_DIMS // 2
    inv = jnp.power(ROPE_THETA, -jnp.arange(half, dtype=F32) / half)
    ang = pos.astype(F32)[:, None] * inv
    cos, sin = jnp.cos(ang), jnp.sin(ang)
    n = pos.shape[0]
    ones = jnp.ones((n, HEAD_DIM - ROPE_DIMS), F32)
    zeros = jnp.zeros((n, HEAD_DIM - ROPE_DIMS), F32)
    zh = jnp.zeros((n, half), F32)
    return (jnp.concatenate([cos, cos, ones], axis=1),
            jnp.concatenate([-sin, zh, zeros], axis=1),
            jnp.concatenate([zh, sin, zeros], axis=1))


def _pad_rows(x, rows):
    return jnp.pad(x, ((0, rows - x.shape[0]),) + ((0, 0),) * (x.ndim - 1))


def kernel(x_prompt, x_sample, c_prompt, c_sample, cache_l0_kv, cache_l1_kv, state_l1_win, cache_l2_kv, cache_l3_kv,
           state_l3_win, page_table, ada_w, ada_b, ln1_g, ln1_b, ln2_g, ln2_b, sb_w_in, sb_w_o, nsa_w_in, nsa_w_o,
           nsa_cmp_pe, nsa_cmp_w1, nsa_cmp_b1, nsa_cmp_w2, nsa_cmp_b2, peer_wq, peer_k1, peer_k2, peer_u, peer_v):
    batch, seq, d = x_prompt.shape
    bd, dec_seq, _ = x_sample.shape
    assert dec_seq == 1 and d == D_MODEL
    depth = ada_w.shape[0]
    n_pages = page_table.shape[1]
    past = n_pages * PAGE_SIZE
    sb_caches = (cache_l0_kv, cache_l2_kv)
    nsa_caches = (cache_l1_kv, cache_l3_kv)
    nsa_wins = (state_l1_win, state_l3_win)
    win_buf = state_l1_win.shape[1]
    assert win_buf == NSA_WINDOW and past % NSA_BLOCK == 0 and seq >= win_buf
    mp_rows = batch * seq
    tm = 512

    n_c = batch + bd
    c_all = _pad_rows(jnp.concatenate([c_prompt, c_sample], axis=0), -(-n_c // 8) * 8)
    mod = _modulation(c_all, ada_w, ada_b).reshape(depth, c_all.shape[0], 6, d)
    mod_p = lambda i, k: mod[i, :batch, k].reshape(batch, 1, d)
    mod_s = lambda i, k: mod[i, batch:n_c, k].reshape(1, bd, d)

    xp = x_prompt.reshape(mp_rows, d)
    xs = x_sample.reshape(bd, d)
    hp = _modulate(xp, mod_p(0, 1), mod_p(0, 0), tm)
    hs = _modulate(xs, mod_s(0, 1), mod_s(0, 0), bd)

    rope_p = _rope_tables(jnp.arange(seq))
    rope_s = _rope_tables(jnp.full((bd,), past))
    nb_p = -(-seq // NSA_BLOCK)
    assert nb_p * NSA_BLOCK == seq
    nb_s = past // NSA_BLOCK + 1

    new = []
    for i in range(depth):
        j = i // 2
        vec = lambda a: a[i].reshape(1, d)
        if i % 2 == 0:
            w_in = sb_w_in[j].astype(BF16)
            w_o = sb_w_o[j].astype(BF16)
            q_p = _matmul(hp, w_in[:, :d], tm, 512, name="sb_q")
            kv_p = _matmul(hp, w_in[:, d:], tm, 512, name="sb_kv")
            o_p = _sb_prompt_attention(q_p, kv_p, batch, seq)
            q_s = _matmul(hs, w_in[:, :d], bd, 512, name="sb_q")
            kv_s = _matmul(hs, w_in[:, d:], bd, 512, name="sb_kv")
            o_s = _sb_sample_attention(q_s, sb_caches[j], page_table)
            new.append((kv_p.reshape(batch, seq, 2, N_HEADS, HEAD_DIM), kv_s.reshape(bd, 1, 2, N_HEADS, HEAD_DIM)))
        else:
            w_in = nsa_w_in[j].astype(BF16)
            w_o = nsa_w_o[j].astype(BF16)
            w_q = w_in[:, :d]
            w_kv4 = w_in[:, d:d + 4 * NSA_KV]
            w_kw = w_in[:, d + 4 * NSA_KV:d + 6 * NSA_KV]
            w_g = jnp.pad(w_in[:, d + 6 * NSA_KV:], ((0, 0), (0, LANES - 3 * N_HEADS)))
            cmp_w = (nsa_cmp_pe[j], nsa_cmp_w1[j].astype(BF16), nsa_cmp_b1[j], nsa_cmp_w2[j], nsa_cmp_b2[j])

            def project(h, rope, rows):
                return (_matmul(h, w_q, rows, 512, epilogue="rope_all", rope=rope, name="nsa_q"),
                        _matmul(h, w_kv4, rows, NSA_KV, epilogue="rope_even", rope=rope, name="nsa_kv4"),
                        _matmul(h, w_kw, rows, NSA_KV, epilogue="rope_even", rope=rope, name="nsa_kwin"),
                        _matmul(h, w_g, rows, LANES, epilogue="sigmoid", name="nsa_gates"))

            q_p, kv4_p, kwin_p, gates_p = project(hp, rope_p, tm)
            cmp_p = _compress(kv4_p.reshape(batch * nb_p, NSA_BLOCK, 4 * NSA_KV), *cmp_w, 256)
            nbp = -(-nb_p // LANES) * LANES
            cmp_p = jnp.pad(cmp_p.reshape(2, batch, nb_p, NSA_KV), ((0, 0), (0, 0), (0, nbp - nb_p), (0, 0)))
            o_p = _nsa_prompt_attention(q_p, kv4_p, kwin_p, gates_p, cmp_p[0], cmp_p[1], batch, seq, nb_p)
            win_p = kwin_p.reshape(batch, seq, 2, NSA_KV_GROUPS, HEAD_DIM)[:, seq - win_buf:]
            q_s, kv4_s, kwin_s, gates_s = project(hs, rope_s, bd)
            x_past = _gather_cmp_pages(nsa_caches[j], page_table)
            cmp_past = _compress(x_past.reshape(bd * (nb_s - 1), NSA_BLOCK, 2 * NSA_KV), *cmp_w, 256)
            x_last = jnp.pad(kv4_s[:, None, :2 * NSA_KV], ((0, 0), (0, NSA_BLOCK - 1), (0, 0)))
            cmp_last = _compress(x_last, *cmp_w, bd)
            nbs = -(-nb_s // LANES) * LANES
            cmp_s = jnp.concatenate([cmp_past.reshape(2, bd, nb_s - 1, NSA_KV), cmp_last.reshape(2, bd, 1, NSA_KV)], 2)
            cmp_s = jnp.pad(cmp_s, ((0, 0), (0, 0), (0, nbs - nb_s), (0, 0)))
            ocmp_s, idx_s = _nsa_sample_select(q_s, cmp_s[0], cmp_s[1], nb_s, past)
            o_s = _nsa_sample_attend(idx_s, page_table, q_s, ocmp_s, gates_s, kv4_s, kwin_s, nsa_wins[j], nsa_caches[j])
            win_s = jnp.concatenate([nsa_wins[j][:, 1:], kwin_s.reshape(bd, 1, 2, NSA_KV_GROUPS, HEAD_DIM)], axis=1)
            new.append((kv4_p.reshape(batch, seq, 4, NSA_KV_GROUPS, HEAD_DIM),
                        kv4_s.reshape(bd, 1, 4, NSA_KV_GROUPS, HEAD_DIM), win_p, win_s))

        xp, hp = _matmul_ln(o_p, w_o, xp, mod_p(i, 2), vec(ln1_g), vec(ln1_b), mod_p(i, 4), mod_p(i, 3), tm)
        xs, hs = _matmul_ln(o_s, w_o, xs, mod_s(i, 2), vec(ln1_g), vec(ln1_b), mod_s(i, 4), mod_s(i, 3), bd)

        nxt = min(i + 1, depth - 1)
        wq = peer_wq[i].astype(BF16)
        u = peer_u[i].astype(BF16)
        v = peer_v[i].astype(BF16)
        xp, hp = _peer(hp, xp, wq, peer_k1[i], peer_k2[i], u, v, mod_p(i, 5), vec(ln2_g), vec(ln2_b),
                       mod_p(nxt, 1), mod_p(nxt, 0), 256, 512, 512)
        pad = LANES
        hs_pad = _pad_rows(hs, pad)
        gate_s = _pad_rows(mod_s(i, 5)[0], pad)[None]
        scale_s = _pad_rows(mod_s(nxt, 1)[0], pad)[None]
        shift_s = _pad_rows(mod_s(nxt, 0)[0], pad)[None]
        xs_pad, hs_pad = _peer(hs_pad, _pad_rows(xs, pad), wq, peer_k1[i], peer_k2[i], u, v, gate_s,
                               vec(ln2_g), vec(ln2_b), scale_s, shift_s, pad, pad, 512)
        xs, hs = xs_pad[:bd], hs_pad[:bd]

    return (xp.reshape(batch, seq, d), xs.reshape(bd, 1, d), new[0][0], new[0][1], new[1][0], new[1][1], new[1][2],
            new[1][3], new[2][0], new[2][1], new[3][0], new[3][1], new[3][2], new[3][3])
```

```python
import functools

import jax
import jax.numpy as jnp
import numpy as np
from jax import lax
from jax.experimental import pallas as pl
from jax.experimental.pallas import tpu as pltpu

F32 = jnp.float32
BF16 = jnp.bfloat16

D_MODEL = 2048
HEAD_DIM = 128
N_HEADS = D_MODEL // HEAD_DIM
PAGE_SIZE = 128
NSA_KV_GROUPS = 4
NSA_GROUP = N_HEADS // NSA_KV_GROUPS
NSA_KV = NSA_KV_GROUPS * HEAD_DIM
NSA_BLOCK = 64
NSA_TOPN = 16
NSA_WINDOW = 512
ROPE_THETA = 500000.0
ROPE_DIMS = HEAD_DIM // 4
PEER_HEADS = 8
PEER_NKEYS = 128
PEER_TOPK = 16
DEPTH = 4
DEEPNORM_ALPHA = (2 * DEPTH) ** 0.25
LN_EPS = 1e-5
NEG_INF = -1e30
FORCE_SCORE = 1e9
ATTN_SCALE = HEAD_DIM ** -0.5
SQRT_HALF = float(np.sqrt(0.5))

LANES = 128
VMEM_LIMIT_CAP = 56 * 1024 * 1024
SB_LOG_UNDERFLOW = -104.0


def _params(sem, vmem_mb):
    return pltpu.CompilerParams(
        dimension_semantics=sem, vmem_limit_bytes=min(vmem_mb * 1024 * 1024, VMEM_LIMIT_CAP))


def _dot(a, b):
    return jnp.dot(a, b, preferred_element_type=F32)


def _dot_nt(a, b):
    return lax.dot_general(a, b, (((1,), (1,)), ((), ())), preferred_element_type=F32)


def _gelu(x):
    return 0.5 * x * (1.0 + lax.erf(x * SQRT_HALF))


def _log_sigmoid_pair(z):
    l1p = jnp.log(1.0 + jnp.exp(-jnp.abs(z)))
    return jnp.minimum(z, 0.0) - l1p, jnp.minimum(-z, 0.0) - l1p


def _ln_res(y, xres, gate, lng, lnb, scale, shift):
    v = DEEPNORM_ALPHA * xres + gate * y
    mu = jnp.mean(v, axis=-1, keepdims=True)
    d = v - mu
    var = jnp.mean(d * d, axis=-1, keepdims=True)
    xn = d * lax.rsqrt(var + LN_EPS) * lng + lnb
    return xn, (xn * (1.0 + scale) + shift).astype(BF16)


def _mod_spec(arr, blocks_per_batch):
    return pl.BlockSpec((None,) + arr.shape[1:], lambda i, *_: (i // blocks_per_batch, 0, 0))


def _modulation_kernel(c_ref, w_ref, b_ref, o_ref):
    c = c_ref[...]
    s = (c * jax.nn.sigmoid(c)).astype(BF16)
    o_ref[...] = _dot(s, w_ref[...].astype(BF16)) + b_ref[...]


def _modulation(c_all, ada_w, ada_b):
    depth, d, n = ada_w.shape
    r = c_all.shape[0]
    tn = 1024
    return pl.pallas_call(
        _modulation_kernel,
        grid=(depth, n // tn),
        in_specs=[
            pl.BlockSpec((r, d), lambda l, j: (0, 0)),
            pl.BlockSpec((None, d, tn), lambda l, j: (l, 0, j)),
            pl.BlockSpec((None, 1, tn), lambda l, j: (l, 0, j)),
        ],
        out_specs=pl.BlockSpec((None, r, tn), lambda l, j: (l, 0, j)),
        out_shape=jax.ShapeDtypeStruct((depth, r, n), F32),
        compiler_params=_params(("arbitrary", "arbitrary"), 32),
        name="modulation",
    )(c_all, ada_w, ada_b.reshape(depth, 1, n))


def _modulate_kernel(x_ref, scale_ref, shift_ref, o_ref):
    o_ref[...] = (x_ref[...] * (1.0 + scale_ref[...]) + shift_ref[...]).astype(BF16)


def _modulate(x, scale, shift, tm):
    m, d = x.shape
    bpb = (m // tm) // scale.shape[0]
    return pl.pallas_call(
        _modulate_kernel,
        grid=(m // tm,),
        in_specs=[pl.BlockSpec((tm, d), lambda i: (i, 0)), _mod_spec(scale, bpb), _mod_spec(shift, bpb)],
        out_specs=pl.BlockSpec((tm, d), lambda i: (i, 0)),
        out_shape=jax.ShapeDtypeStruct((m, d), BF16),
        compiler_params=_params(("arbitrary",), 32),
        name="modulate",
    )(x, scale, shift)


def _store_tile(y, o_refs):
    o_refs[0][...] = y.astype(o_refs[0].dtype)
    if len(o_refs) > 1:
        for h in range(o_refs[1].shape[1]):
            o_refs[1][:, h, :] = y[:, h * HEAD_DIM:(h + 1) * HEAD_DIM]


def _mm_kernel(a_ref, w_ref, *o_refs):
    _store_tile(_dot(a_ref[...], w_ref[...]), o_refs)


def _mm_sigmoid_kernel(a_ref, w_ref, o_ref):
    o_ref[...] = jax.nn.sigmoid(_dot(a_ref[...], w_ref[...]))


def _mm_rope_kernel(a_ref, w_ref, cos_ref, sa_ref, sb_ref, *o_refs, even_only):
    acc = _dot(a_ref[...], w_ref[...])
    tn = acc.shape[1]
    reps = tn // HEAD_DIM
    cos = jnp.concatenate([cos_ref[...]] * reps, axis=1)
    sa = jnp.concatenate([sa_ref[...]] * reps, axis=1)
    sb = jnp.concatenate([sb_ref[...]] * reps, axis=1)
    half = ROPE_DIMS // 2
    rot = acc * cos + pltpu.roll(acc, tn - half, 1) * sa + pltpu.roll(acc, half, 1) * sb
    if even_only:
        rot = jnp.where(pl.program_id(1) % 2 == 0, rot, acc)
    _store_tile(rot, o_refs)


def _matmul(a, w, tm, tn, *, epilogue=None, rope=None, out_dtype=F32, rows_out=None, name="matmul"):
    m, k = a.shape
    n = w.shape[1]
    tm = min(tm, m)
    in_specs = [pl.BlockSpec((tm, k), lambda i, j: (i, 0)), pl.BlockSpec((k, tn), lambda i, j: (0, j))]
    args = [a, w]
    if epilogue in ("rope_all", "rope_even"):
        assert epilogue == "rope_all" or tn == NSA_KV
        t_tab = rope[0].shape[0]
        nt = t_tab // tm
        in_specs += [pl.BlockSpec((tm, HEAD_DIM), lambda i, j: (i % nt, 0))] * 3
        args += list(rope)
        kern = functools.partial(_mm_rope_kernel, even_only=(epilogue == "rope_even"))
    elif epilogue == "sigmoid":
        kern = _mm_sigmoid_kernel
    else:
        kern = _mm_kernel
    out_specs = pl.BlockSpec((tm, tn), lambda i, j: (i, j))
    out_shape = jax.ShapeDtypeStruct((m, n), out_dtype)
    if rows_out is not None:
        slots, heads = rows_out
        tiles_per_slot = heads * HEAD_DIM // tn
        out_specs = [out_specs, pl.BlockSpec((tm, None, heads // tiles_per_slot, HEAD_DIM),
                                             lambda i, j: (i, j // tiles_per_slot, j % tiles_per_slot, 0))]
        out_shape = [out_shape, jax.ShapeDtypeStruct((m, slots, heads, HEAD_DIM), F32)]
    return pl.pallas_call(
        kern,
        grid=(m // tm, n // tn),
        in_specs=in_specs,
        out_specs=out_specs,
        out_shape=out_shape,
        compiler_params=_params(("arbitrary", "arbitrary"), 40),
        name=name,
    )(*args)


def _mm_ln_kernel(a_ref, w_ref, x_ref, gate_ref, lng_ref, lnb_ref, scale_ref, shift_ref, xo_ref, ho_ref):
    y = _dot(a_ref[...].astype(BF16), w_ref[...])
    xn, h = _ln_res(y, x_ref[...], gate_ref[...], lng_ref[...], lnb_ref[...], scale_ref[...], shift_ref[...])
    xo_ref[...] = xn
    ho_ref[...] = h


def _matmul_ln(a, w, xres, gate, lng, lnb, scale, shift, tm):
    m, k = a.shape
    d = w.shape[1]
    tm = min(tm, m)
    bpb = (m // tm) // gate.shape[0]
    row = pl.BlockSpec((tm, d), lambda i: (i, 0))
    vec = pl.BlockSpec((1, d), lambda i: (0, 0))
    return pl.pallas_call(
        _mm_ln_kernel,
        grid=(m // tm,),
        in_specs=[pl.BlockSpec((tm, k), lambda i: (i, 0)), pl.BlockSpec((k, d), lambda i: (0, 0)), row,
                  _mod_spec(gate, bpb), vec, vec, _mod_spec(scale, bpb), _mod_spec(shift, bpb)],
        out_specs=[row, row],
        out_shape=[jax.ShapeDtypeStruct((m, d), F32), jax.ShapeDtypeStruct((m, d), BF16)],
        compiler_params=_params(("arbitrary",), 48),
        name="matmul_ln",
    )(a, w, xres, gate, lng, lnb, scale, shift)


def _suffix_matrix(n):
    return (lax.broadcasted_iota(jnp.int32, (n, n), 0) > lax.broadcasted_iota(jnp.int32, (n, n), 1)).astype(BF16)


def _suffix_sum(x, lmat):
    hi = x.astype(BF16)
    lo = (x - hi.astype(F32)).astype(BF16)
    return _dot(hi, lmat) + _dot(lo, lmat)


def _sb_prompt_kernel(q_ref, k_ref, v_ref, o_ref, carry_ref, acc_ref, *, tq, tk):
    i = pl.program_id(2)
    q = q_ref[...].astype(BF16)
    carry_ref[...] = jnp.zeros_like(carry_ref)
    acc_ref[...] = jnp.zeros_like(acc_ref)
    qpos = i * tq + lax.broadcasted_iota(jnp.int32, (tq, 1), 0)
    lmat = _suffix_matrix(tk)

    def body(state):
        j, _ = state
        k0 = pl.multiple_of(j * tk, tk)
        k = k_ref[pl.ds(k0, tk), :].astype(BF16)
        v = v_ref[pl.ds(k0, tk), :].astype(BF16)
        z = _dot_nt(q, k) * ATTN_SCALE
        allowed = (k0 + lax.broadcasted_iota(jnp.int32, (1, tk), 1)) < qpos
        ls, lsn = _log_sigmoid_pair(z)
        log_keep = jnp.where(allowed, lsn, 0.0)
        later = _suffix_sum(log_keep, lmat) + carry_ref[...]
        a = jnp.where(allowed, jnp.exp(ls + later), 0.0)
        acc_ref[...] += _dot(a.astype(BF16), v)
        carry = carry_ref[...] + jnp.sum(log_keep, axis=1, keepdims=True)
        carry_ref[...] = carry
        return j - 1, (jnp.max(carry) > SB_LOG_UNDERFLOW).astype(jnp.int32)

    lax.while_loop(lambda s: jnp.logical_and(s[0] >= 0, s[1] > 0), body, ((i * tq + tq - 1) // tk, jnp.int32(1)))
    o_ref[...] = acc_ref[...]


def _sb_prompt_attention(q, kv, batch, seq, tq=256, tk=128):
    nq = seq // tq
    return pl.pallas_call(
        functools.partial(_sb_prompt_kernel, tq=tq, tk=tk),
        grid=(batch, N_HEADS, nq),
        in_specs=[
            pl.BlockSpec((tq, HEAD_DIM), lambda b, h, i: (b * nq + i, h)),
            pl.BlockSpec((seq, HEAD_DIM), lambda b, h, i: (b, h)),
            pl.BlockSpec((seq, HEAD_DIM), lambda b, h, i: (b, N_HEADS + h)),
        ],
        out_specs=pl.BlockSpec((tq, HEAD_DIM), lambda b, h, i: (b * nq + i, h)),
        out_shape=jax.ShapeDtypeStruct(q.shape, F32),
        scratch_shapes=[pltpu.VMEM((tq, 1), F32), pltpu.VMEM((tq, HEAD_DIM), F32)],
        compiler_params=_params(("arbitrary", "arbitrary", "arbitrary"), 32),
        name="sb_prompt_attention",
    )(q, kv, kv)


SB_ROW_REP = 8


def _sb_sample_kernel(pt_ref, q_ref, cache_ref, o_ref, kbuf, vbuf, sem, carry_ref, acc_ref, *, n_pages):
    b = pl.program_id(0)
    rep = SB_ROW_REP
    q = q_ref[...]
    qh = [jnp.broadcast_to(q[:, h * HEAD_DIM:(h + 1) * HEAD_DIM], (rep, HEAD_DIM)).astype(BF16)
          for h in range(N_HEADS)]
    lmat = _suffix_matrix(PAGE_SIZE)
    carry_ref[...] = jnp.zeros_like(carry_ref)
    acc_ref[...] = jnp.zeros_like(acc_ref)

    def page_copies(page):
        return (pltpu.make_async_copy(cache_ref.at[page, :, 0], kbuf, sem.at[0]),
                pltpu.make_async_copy(cache_ref.at[page, :, 1], vbuf, sem.at[1]))

    def body(state):
        p, _ = state
        ck, cv = page_copies(pt_ref[b * n_pages + p])
        ck.start()
        cv.start()
        ck.wait()
        cv.wait()
        z = jnp.concatenate([_dot_nt(qh[h], kbuf[:, h, :].astype(BF16)) for h in range(N_HEADS)], axis=0)
        ls, lsn = _log_sigmoid_pair(z * ATTN_SCALE)
        later = _suffix_sum(lsn, lmat) + carry_ref[...]
        a = jnp.exp(ls + later)
        pv = [_dot(a[h * rep:(h + 1) * rep, :].astype(BF16), vbuf[:, h, :].astype(BF16)) for h in range(N_HEADS)]
        acc_ref[...] += jnp.concatenate(pv, axis=0)
        carry = carry_ref[...] + jnp.sum(lsn, axis=1, keepdims=True)
        carry_ref[...] = carry
        return p - 1, (jnp.max(carry) > SB_LOG_UNDERFLOW).astype(jnp.int32)

    lax.while_loop(lambda s: jnp.logical_and(s[0] >= 0, s[1] > 0), body, (jnp.int32(n_pages - 1), jnp.int32(1)))
    for h in range(N_HEADS):
        o_ref[:, h * HEAD_DIM:(h + 1) * HEAD_DIM] = acc_ref[h * rep:h * rep + 1, :]


def _sb_sample_attention(q, cache_kv, page_table):
    bd, n_pages = page_table.shape
    page_buf = pltpu.VMEM((PAGE_SIZE, N_HEADS, HEAD_DIM), F32)
    out = pl.pallas_call(
        functools.partial(_sb_sample_kernel, n_pages=n_pages),
        grid_spec=pltpu.PrefetchScalarGridSpec(
            num_scalar_prefetch=1,
            grid=(bd,),
            in_specs=[pl.BlockSpec((None, 1, D_MODEL), lambda b, pt: (b, 0, 0)),
                      pl.BlockSpec(memory_space=pl.ANY)],
            out_specs=pl.BlockSpec((None, 1, D_MODEL), lambda b, pt: (b, 0, 0)),
            scratch_shapes=[page_buf, page_buf, pltpu.SemaphoreType.DMA((2,)),
                            pltpu.VMEM((N_HEADS * SB_ROW_REP, 1), F32),
                            pltpu.VMEM((N_HEADS * SB_ROW_REP, HEAD_DIM), F32)]),
        out_shape=jax.ShapeDtypeStruct((bd, 1, D_MODEL), F32),
        compiler_params=_params(("arbitrary",), 16),
        name="sb_sample_attention",
    )(page_table.reshape(-1), q.reshape(bd, 1, D_MODEL), cache_kv)
    return out.reshape(bd, D_MODEL)


def _compress_kernel(x_ref, pe_ref, w1_ref, b1_ref, w2_ref, b2_ref, o_ref):
    acc = jnp.zeros((x_ref.shape[0], w1_ref.shape[1]), F32)
    for t in range(NSA_BLOCK):
        xt = (x_ref[:, t, :] + pe_ref[t:t + 1, :]).astype(BF16)
        acc = acc + _dot(xt, w1_ref[t * HEAD_DIM:(t + 1) * HEAD_DIM, :])
    hid = _gelu(acc + b1_ref[...])
    o_ref[...] = _dot(hid.astype(BF16), w2_ref[...].astype(BF16)) + b2_ref[...]


def _compress(x, pe, w1, b1, w2, b2, tr):
    r = x.shape[0]
    tr = min(tr, r)
    hidden = w1.shape[2]
    return pl.pallas_call(
        _compress_kernel,
        grid=(2, NSA_KV_GROUPS, r // tr),
        in_specs=[
            pl.BlockSpec((tr, NSA_BLOCK, HEAD_DIM), lambda s, g, i: (i, 0, s * NSA_KV_GROUPS + g)),
            pl.BlockSpec((None, NSA_BLOCK, HEAD_DIM), lambda s, g, i: (s, 0, 0)),
            pl.BlockSpec((None, NSA_BLOCK * HEAD_DIM, hidden), lambda s, g, i: (s, 0, 0)),
            pl.BlockSpec((None, 1, hidden), lambda s, g, i: (s, 0, 0)),
            pl.BlockSpec((None, hidden, HEAD_DIM), lambda s, g, i: (s, 0, 0)),
            pl.BlockSpec((None, 1, HEAD_DIM), lambda s, g, i: (s, 0, 0)),
        ],
        out_specs=pl.BlockSpec((None, tr, HEAD_DIM), lambda s, g, i: (s, i, g)),
        out_shape=jax.ShapeDtypeStruct((2, r, NSA_KV), F32),
        compiler_params=_params(("arbitrary", "arbitrary", "arbitrary"), 48),
        name="nsa_compress",
    )(x, pe, w1, b1.reshape(2, 1, hidden), w2, b2.reshape(2, 1, HEAD_DIM))


def _page_gather_kernel(pt_ref, *refs):
    o_ref = refs[-1]
    for k, r in enumerate(refs[:-1]):
        for s in range(2):
            for g in range(NSA_KV_GROUPS):
                lanes = slice((s * NSA_KV_GROUPS + g) * HEAD_DIM, (s * NSA_KV_GROUPS + g + 1) * HEAD_DIM)
                o_ref[k * PAGE_SIZE:(k + 1) * PAGE_SIZE, lanes] = r[:, s, g, :]


def _gather_cmp_pages(cache_kv, page_table, pages_per_step=8):
    bd, n_pages = page_table.shape
    width = 2 * NSA_KV
    pps = pages_per_step

    def in_map(k):
        return lambda b, c, pt: (pt[b * n_pages + c * pps + k], 0, 0, 0, 0)

    return pl.pallas_call(
        _page_gather_kernel,
        grid_spec=pltpu.PrefetchScalarGridSpec(
            num_scalar_prefetch=1,
            grid=(bd, n_pages // pps),
            in_specs=[pl.BlockSpec((None, PAGE_SIZE, 2, NSA_KV_GROUPS, HEAD_DIM), in_map(k)) for k in range(pps)],
            out_specs=pl.BlockSpec((None, pps * PAGE_SIZE, width), lambda b, c, pt: (b, c, 0))),
        out_shape=jax.ShapeDtypeStruct((bd, n_pages * PAGE_SIZE, width), F32),
        compiler_params=_params(("arbitrary", "arbitrary"), 40),
        name="nsa_page_gather",
    )(page_table.reshape(-1), *([cache_kv] * pps))


def _online_softmax_step(z, v_t, m_ref, l_ref, acc_ref):
    m_new = jnp.maximum(m_ref[...], jnp.max(z, axis=0, keepdims=True))
    alpha = jnp.exp(m_ref[...] - m_new)
    p = jnp.exp(z - m_new)
    l_ref[...] = alpha * l_ref[...] + jnp.sum(p, axis=0, keepdims=True)
    acc_ref[...] = alpha * acc_ref[...] + _dot(v_t, p.astype(BF16))
    m_ref[...] = m_new


def _nsa_prompt_kernel(q_ref, kc_ref, vc_ref, ks_ref, vs_ref, kw_ref, vw_ref, gate_ref, o_ref,
                       m_ref, l_ref, acc_ref, osel_ref, gate_t_ref, *, nb, tq, tk):
    g = pl.program_id(1)
    i = pl.program_id(2)
    nq = NSA_GROUP * tq
    nbp = kc_ref.shape[0]
    qblk = q_ref[...]
    qg = jnp.concatenate([qblk[:, r * HEAD_DIM:(r + 1) * HEAD_DIM] for r in range(NSA_GROUP)], axis=0).astype(BF16)
    qpos = i * tq + lax.broadcasted_iota(jnp.int32, (1, tq), 1)
    tile4 = lambda x: jnp.concatenate([x] * NSA_GROUP, axis=1)

    blk = lax.broadcasted_iota(jnp.int32, (nbp, 1), 0)
    zc = _dot_nt(kc_ref[...].astype(BF16), qg) * ATTN_SCALE
    avail = tile4(jnp.logical_and((blk + 1) * NSA_BLOCK - 1 <= qpos, blk < nb))
    zm = jnp.where(avail, zc, NEG_INF)
    e = jnp.where(avail, jnp.exp(zm - jnp.max(zm, axis=0, keepdims=True)), 0.0)
    den = jnp.sum(e, axis=0, keepdims=True)
    pc = e / jnp.where(den > 0.0, den, 1.0)
    o_cmp = _dot(jnp.transpose(vc_ref[...]).astype(BF16), pc.astype(BF16))

    score = pc[:, 0:tq]
    for r in range(1, NSA_GROUP):
        score = score + pc[:, r * tq:(r + 1) * tq]
    nbr = -(-nb // 8) * 8
    blk = blk[0:nbr]
    cur = qpos // NSA_BLOCK
    forced = jnp.logical_or(blk == 0, jnp.logical_or(blk == cur, blk == cur - 1))
    s = jnp.where(blk <= cur, jnp.where(forced, FORCE_SCORE, score[0:nbr]), -1.0)
    rank = jnp.zeros((nbr, tq), F32)
    for mth in range(nb):
        sm = s[mth:mth + 1, :]
        ahead = jnp.logical_or(sm > s, jnp.logical_and(sm == s, blk > mth))
        rank = rank + jnp.where(ahead, 1.0, 0.0)
    sel = jnp.where(jnp.logical_and(rank < float(min(NSA_TOPN, nb)), s >= 0.0), 1.0, 0.0).astype(BF16)

    def reset():
        m_ref[...] = jnp.full_like(m_ref, NEG_INF)
        l_ref[...] = jnp.zeros_like(l_ref)
        acc_ref[...] = jnp.zeros_like(acc_ref)

    reset()
    bpt = tk // NSA_BLOCK

    def sel_body(kt, carry):
        k0 = pl.multiple_of(kt * tk, tk)
        z = _dot_nt(ks_ref[pl.ds(k0, tk), :].astype(BF16), qg) * ATTN_SCALE
        key = lax.broadcasted_iota(jnp.int32, (tk, nbr), 0)
        col = lax.broadcasted_iota(jnp.int32, (tk, nbr), 1)
        expand = (col == kt * bpt + key // NSA_BLOCK).astype(BF16)
        chosen = _dot(expand, sel) > 0.5
        kpos = k0 + lax.broadcasted_iota(jnp.int32, (tk, 1), 0)
        bias = jnp.where(jnp.logical_and(chosen, kpos <= qpos), 0.0, NEG_INF)
        v_t = jnp.transpose(vs_ref[pl.ds(k0, tk), :]).astype(BF16)
        _online_softmax_step(z + tile4(bias), v_t, m_ref, l_ref, acc_ref)
        return carry

    lax.fori_loop(0, (i * tq + tq - 1) // tk + 1, sel_body, 0)
    osel_ref[...] = acc_ref[...] / l_ref[...]

    reset()

    def win_body(kt, carry):
        k0 = pl.multiple_of(kt * tq, tq)
        z = _dot_nt(kw_ref[pl.ds(k0, tq), :].astype(BF16), qg) * ATTN_SCALE
        dist = qpos - (k0 + lax.broadcasted_iota(jnp.int32, (tq, 1), 0))
        bias = jnp.where(jnp.logical_and(dist >= 0, dist <= NSA_WINDOW), 0.0, NEG_INF)
        v_t = jnp.transpose(vw_ref[pl.ds(k0, tq), :]).astype(BF16)
        _online_softmax_step(z + tile4(bias), v_t, m_ref, l_ref, acc_ref)
        return carry

    lax.fori_loop(jnp.maximum(i - NSA_WINDOW // tq, 0), i + 1, win_body, 0)
    o_win = acc_ref[...] / l_ref[...]
    o_sel = osel_ref[...]

    gate_t_ref[...] = jnp.transpose(gate_ref[...])
    for r in range(NSA_GROUP):
        lanes = slice(r * tq, (r + 1) * tq)
        out_t = jnp.zeros((HEAD_DIM, tq), F32)
        for c, o_c in enumerate((o_cmp, o_sel, o_win)):
            gate = gate_t_ref[pl.ds(c * N_HEADS + g * NSA_GROUP + r, 1), :]
            out_t = out_t + gate * o_c[:, lanes]
        o_ref[:, r * HEAD_DIM:(r + 1) * HEAD_DIM] = jnp.transpose(out_t)


def _nsa_prompt_attention(q, kv4, kwin, gates, kc, vc, batch, seq, nb):
    tq = 128
    tk = 256
    nq = seq // tq
    g4 = NSA_KV_GROUPS
    col = lambda off: pl.BlockSpec((seq, HEAD_DIM), lambda b, g, i: (b, off + g))
    cmp_spec = pl.BlockSpec((None, kc.shape[1], HEAD_DIM), lambda b, g, i: (b, 0, g))
    return pl.pallas_call(
        functools.partial(_nsa_prompt_kernel, nb=nb, tq=tq, tk=tk),
        grid=(batch, g4, nq),
        in_specs=[
            pl.BlockSpec((tq, NSA_KV), lambda b, g, i: (b * nq + i, g)),
            cmp_spec, cmp_spec,
            col(2 * g4), col(3 * g4),
            col(0), col(g4),
            pl.BlockSpec((tq, LANES), lambda b, g, i: (b * nq + i, 0)),
        ],
        out_specs=pl.BlockSpec((tq, NSA_KV), lambda b, g, i: (b * nq + i, g)),
        out_shape=jax.ShapeDtypeStruct(q.shape, F32),
        scratch_shapes=[pltpu.VMEM((1, NSA_GROUP * tq), F32), pltpu.VMEM((1, NSA_GROUP * tq), F32),
                        pltpu.VMEM((HEAD_DIM, NSA_GROUP * tq), F32), pltpu.VMEM((HEAD_DIM, NSA_GROUP * tq), F32),
                        pltpu.VMEM((LANES, tq), F32)],
        compiler_params=_params(("arbitrary", "arbitrary", "arbitrary"), 40),
        name="nsa_prompt_attention",
    )(q, kc, vc, kv4, kv4, kwin, kwin, gates)


def _nsa_sample_select_kernel(q_ref, kc_ref, vc_ref, ocmp_ref, idx_ref, *, nb, qpos):
    nbp = kc_ref.shape[0]
    q = q_ref[...]
    lane_blk = lax.broadcasted_iota(jnp.int32, (1, nbp), 1)
    row_i = lax.broadcasted_iota(jnp.int32, (nbp, nbp), 0)
    col_i = lax.broadcasted_iota(jnp.int32, (nbp, nbp), 1)
    eye = row_i == col_i
    cur = qpos // NSA_BLOCK
    avail = jnp.logical_and((lane_blk + 1) * NSA_BLOCK - 1 <= qpos, lane_blk < nb)
    forced = jnp.logical_or(lane_blk == 0, jnp.logical_or(lane_blk == cur, lane_blk == cur - 1))
    for g in range(NSA_KV_GROUPS):
        qg = jnp.concatenate(
            [q[:, (g * NSA_GROUP + r) * HEAD_DIM:(g * NSA_GROUP + r + 1) * HEAD_DIM] for r in range(NSA_GROUP)],
            axis=0).astype(BF16)
        lanes = slice(g * HEAD_DIM, (g + 1) * HEAD_DIM)
        zc = _dot_nt(qg, kc_ref[:, lanes].astype(BF16)) * ATTN_SCALE
        zm = jnp.where(avail, zc, NEG_INF)
        e = jnp.where(avail, jnp.exp(zm - jnp.max(zm, axis=1, keepdims=True)), 0.0)
        den = jnp.sum(e, axis=1, keepdims=True)
        pc = e / jnp.where(den > 0.0, den, 1.0)
        o = _dot(pc.astype(BF16), vc_ref[:, lanes].astype(BF16))
        for r in range(NSA_GROUP):
            h = g * NSA_GROUP + r
            ocmp_ref[:, h * HEAD_DIM:(h + 1) * HEAD_DIM] = o[r:r + 1, :]
        score = jnp.sum(pc, axis=0, keepdims=True)
        s_row = jnp.where(lane_blk <= cur, jnp.where(forced, FORCE_SCORE, score), -1.0)
        s_mat = jnp.broadcast_to(s_row, (nbp, nbp))
        s_col = jnp.sum(jnp.where(eye, s_mat, 0.0), axis=1, keepdims=True)
        ahead = jnp.logical_or(s_mat > s_col, jnp.logical_and(s_mat == s_col, col_i < row_i))
        rank = jnp.sum(jnp.where(ahead, 1.0, 0.0), axis=1, keepdims=True)
        sel_col = jnp.where(jnp.logical_and(rank < float(min(NSA_TOPN, nb)), s_col >= 0.0), 1.0, 0.0)
        sel_row = jnp.sum(jnp.where(eye, jnp.broadcast_to(sel_col, (nbp, nbp)), 0.0), axis=0, keepdims=True)
        slot = jnp.sum(jnp.where(col_i < row_i, jnp.broadcast_to(sel_row, (nbp, nbp)), 0.0), axis=1, keepdims=True)
        k_i = lax.broadcasted_iota(jnp.int32, (nbp, LANES), 1).astype(F32)
        n_i = lax.broadcasted_iota(jnp.int32, (nbp, LANES), 0).astype(F32)
        hit = jnp.logical_and(sel_col > 0.5, slot == k_i)
        idx_ref[g:g + 1, :] = jnp.sum(jnp.where(hit, n_i, 0.0), axis=0, keepdims=True).astype(jnp.int32)


def _nsa_sample_select(q, kc, vc, nb, qpos):
    bd = q.shape[0]
    nbp = kc.shape[1]
    cmp_spec = pl.BlockSpec((None, nbp, NSA_KV), lambda b: (b, 0, 0))
    ocmp, idx = pl.pallas_call(
        functools.partial(_nsa_sample_select_kernel, nb=nb, qpos=qpos),
        grid=(bd,),
        in_specs=[pl.BlockSpec((None, 1, D_MODEL), lambda b: (b, 0, 0)), cmp_spec, cmp_spec],
        out_specs=[pl.BlockSpec((None, 1, D_MODEL), lambda b: (b, 0, 0)),
                   pl.BlockSpec((None, NSA_KV_GROUPS, LANES), lambda b: (b, 0, 0))],
        out_shape=[jax.ShapeDtypeStruct((bd, 1, D_MODEL), F32),
                   jax.ShapeDtypeStruct((bd, NSA_KV_GROUPS, LANES), jnp.int32)],
        compiler_params=_params(("arbitrary",), 32),
        name="nsa_sample_select",
    )(q.reshape(bd, 1, D_MODEL), kc, vc)
    return ocmp, idx[:, :, :NSA_TOPN]


def _nsa_sample_attend_kernel(idx_ref, pt_ref, q_ref, ocmp_ref, gate_ref, kv_ref, kwin_ref, win_ref, cache_ref,
                              o_ref, kbuf, vbuf, sem, *, n_pages, n_past_blocks):
    b = pl.program_id(0)
    g4 = NSA_KV_GROUPS
    bpp = PAGE_SIZE // NSA_BLOCK

    def block_copies(g, k):
        blk = jnp.minimum(idx_ref[(b * g4 + g) * NSA_TOPN + k], n_past_blocks - 1)
        page = pt_ref[b * n_pages + blk // bpp]
        rows = pl.ds((blk % bpp) * NSA_BLOCK, NSA_BLOCK)
        dst = pl.ds(k * NSA_BLOCK, NSA_BLOCK)
        return (pltpu.make_async_copy(cache_ref.at[page, rows, 2], kbuf.at[g % 2, dst], sem.at[0, g % 2, k]),
                pltpu.make_async_copy(cache_ref.at[page, rows, 3], vbuf.at[g % 2, dst], sem.at[1, g % 2, k]))

    def start_group(g):
        for k in range(NSA_TOPN):
            ck, cv = block_copies(g, k)
            ck.start()
            cv.start()

    def wait_group(g):
        for k in range(NSA_TOPN):
            ck, cv = block_copies(g, k)
            ck.wait()
            cv.wait()

    start_group(0)

    q = q_ref[...]
    ocmp = ocmp_ref[...]
    gates = gate_ref[...]
    kv_new = kv_ref[...]
    kwin_new = kwin_ref[...]
    n_sel = NSA_TOPN * NSA_BLOCK
    key_slot = lax.broadcasted_iota(jnp.int32, (1, n_sel), 1) // NSA_BLOCK
    lane = lax.broadcasted_iota(jnp.int32, (NSA_GROUP, LANES), 1)
    sub = lax.broadcasted_iota(jnp.int32, (NSA_GROUP, LANES), 0)
    bf = lambda x: x.astype(BF16).astype(F32)

    def attend(qg, keys, vals, valid, k_new, v_new, new_on):
        z = _dot_nt(qg, keys.astype(BF16)) * ATTN_SCALE
        z_new = jnp.sum(bf(qg) * bf(k_new), axis=1, keepdims=True) * ATTN_SCALE
        zm = z if valid is None else jnp.where(valid, z, NEG_INF)
        zn = jnp.where(new_on, z_new, NEG_INF)
        m = jnp.maximum(jnp.max(zm, axis=1, keepdims=True), zn)
        p = jnp.exp(zm - m) if valid is None else jnp.where(valid, jnp.exp(zm - m), 0.0)
        p_new = jnp.where(new_on, jnp.exp(zn - m), 0.0)
        den = jnp.sum(p, axis=1, keepdims=True) + p_new
        return (_dot(p.astype(BF16), vals.astype(BF16)) + bf(p_new) * bf(v_new)) / den

    for g in range(g4):
        if g + 1 < g4:
            start_group(g + 1)
        wait_group(g)
        qg = jnp.concatenate(
            [q[:, (g * NSA_GROUP + r) * HEAD_DIM:(g * NSA_GROUP + r + 1) * HEAD_DIM] for r in range(NSA_GROUP)],
            axis=0).astype(BF16)
        o_cmp = jnp.concatenate(
            [ocmp[:, (g * NSA_GROUP + r) * HEAD_DIM:(g * NSA_GROUP + r + 1) * HEAD_DIM] for r in range(NSA_GROUP)],
            axis=0)
        valid = jnp.zeros((1, n_sel), jnp.int32)
        has_new = jnp.int32(0)
        for k in range(NSA_TOPN):
            is_new = (idx_ref[(b * g4 + g) * NSA_TOPN + k] >= n_past_blocks).astype(jnp.int32)
            valid = jnp.where(key_slot == k, 1 - is_new, valid)
            has_new = jnp.maximum(has_new, is_new)
        lanes = lambda slot: slice((slot * g4 + g) * HEAD_DIM, (slot * g4 + g + 1) * HEAD_DIM)
        o_sel = attend(qg, kbuf[g % 2, :, g, :], vbuf[g % 2, :, g, :], valid > 0,
                       kv_new[:, lanes(2)], kv_new[:, lanes(3)], has_new > 0)
        o_win = attend(qg, win_ref[:, 0, g, :], win_ref[:, 1, g, :], None,
                       kwin_new[:, lanes(0)], kwin_new[:, lanes(1)], True)
        out = jnp.zeros((NSA_GROUP, HEAD_DIM), F32)
        for c, o_c in enumerate((o_cmp, o_sel, o_win)):
            pick = lane == c * N_HEADS + g * NSA_GROUP + sub
            gate = jnp.sum(jnp.where(pick, jnp.broadcast_to(gates, (NSA_GROUP, LANES)), 0.0), axis=1, keepdims=True)
            out = out + gate * o_c
        for r in range(NSA_GROUP):
            h = g * NSA_GROUP + r
            o_ref[:, h * HEAD_DIM:(h + 1) * HEAD_DIM] = out[r:r + 1, :]


def _nsa_sample_attend(idx, page_table, q, ocmp, gates, kv4, kwin, win_state, cache_kv):
    bd, n_pages = page_table.shape
    g4 = NSA_KV_GROUPS
    sel_buf = pltpu.VMEM((2, NSA_TOPN * NSA_BLOCK, g4, HEAD_DIM), F32)
    row = lambda w: pl.BlockSpec((None, 1, w), lambda b, *_: (b, 0, 0))
    out = pl.pallas_call(
        functools.partial(_nsa_sample_attend_kernel, n_pages=n_pages, n_past_blocks=n_pages * PAGE_SIZE // NSA_BLOCK),
        grid_spec=pltpu.PrefetchScalarGridSpec(
            num_scalar_prefetch=2,
            grid=(bd,),
            in_specs=[row(D_MODEL), row(D_MODEL), row(LANES), row(4 * NSA_KV), row(2 * NSA_KV),
                      pl.BlockSpec((None,) + win_state.shape[1:], lambda b, *_: (b, 0, 0, 0, 0)),
                      pl.BlockSpec(memory_space=pl.ANY)],
            out_specs=row(D_MODEL),
            scratch_shapes=[sel_buf, sel_buf, pltpu.SemaphoreType.DMA((2, 2, NSA_TOPN))]),
        out_shape=jax.ShapeDtypeStruct((bd, 1, D_MODEL), F32),
        compiler_params=_params(("arbitrary",), 40),
        name="nsa_sample_attend",
    )(idx.reshape(-1), page_table.reshape(-1), q.reshape(bd, 1, D_MODEL), ocmp, gates.reshape(bd, 1, LANES),
      kv4.reshape(bd, 1, 4 * NSA_KV), kwin.reshape(bd, 1, 2 * NSA_KV), win_state, cache_kv)
    return out.reshape(bd, D_MODEL)


def _topk_rows(s, k, ids=None):
    if ids is None:
        ids = lax.broadcasted_iota(jnp.int32, s.shape, 0).astype(F32)
    vals, picked = [], []
    for _ in range(k):
        m = jnp.max(s, axis=0, keepdims=True)
        first = jnp.min(jnp.where(s == m, ids, jnp.inf), axis=0, keepdims=True)
        s = jnp.where(ids == first, -jnp.inf, s)
        vals.append(m)
        picked.append(first)
    return jnp.concatenate(vals, axis=0), jnp.concatenate(picked, axis=0)


def _pair_candidates(v1, v2):
    tokens = v1.shape[1]
    row = lambda n: lax.broadcasted_iota(jnp.int32, (n, tokens), 0).astype(F32)
    vals, ids = [], []
    for k1 in range(4):
        n = 16 if k1 == 0 else 8
        k2 = row(n)
        vals.append(jnp.where((k1 + 1) * (k2 + 1.0) <= PEER_TOPK, v1[k1:k1 + 1, :] + v2[0:n, :], -jnp.inf))
        ids.append(k1 * PEER_TOPK + k2)
    for k2 in range(3):
        n = 16 if k2 == 0 else 8
        k1 = row(n)
        keep = jnp.logical_and(k1 >= 4.0, (k1 + 1.0) * (k2 + 1) <= PEER_TOPK)
        vals.append(jnp.where(keep, v1[0:n, :] + v2[k2:k2 + 1, :], -jnp.inf))
        ids.append(k1 * PEER_TOPK + k2)
    return jnp.concatenate(vals, axis=0), jnp.concatenate(ids, axis=0)


def _pick_rows(sel, table):
    out = jnp.zeros_like(sel)
    for k in range(table.shape[0]):
        out = jnp.where(sel == float(k), table[k:k + 1, :], out)
    return out


def _peer_route_kernel(h_ref, wq_ref, k1_ref, k2_ref, i1_ref, i2_ref, g_ref, qh_ref, i1_t, i2_t, g_t):
    qh_ref[...] = _dot(h_ref[...], wq_ref[...])
    k1 = k1_ref[...].astype(BF16)
    k2 = k2_ref[...].astype(BF16)
    half = PEER_NKEYS

    def head(h, carry):
        c0 = pl.multiple_of(h * 2 * half, 2 * half)
        s1 = _dot_nt(k1, qh_ref[:, pl.ds(c0, half)].astype(BF16))
        s2 = _dot_nt(k2, qh_ref[:, pl.ds(c0 + half, half)].astype(BF16))
        v1, i1 = _topk_rows(s1, PEER_TOPK)
        v2, i2 = _topk_rows(s2, PEER_TOPK)
        cand, cand_id = _pair_candidates(v1, v2)
        sc, j = _topk_rows(cand, PEER_TOPK, cand_id)
        ja = jnp.floor(j * (1.0 / PEER_TOPK))
        jb = j - ja * PEER_TOPK
        e = jnp.exp(sc - jnp.max(sc, axis=0, keepdims=True))
        r0 = pl.multiple_of(h * PEER_TOPK, PEER_TOPK)
        i1_t[pl.ds(r0, PEER_TOPK), :] = _pick_rows(ja, i1)
        i2_t[pl.ds(r0, PEER_TOPK), :] = _pick_rows(jb, i2)
        g_t[pl.ds(r0, PEER_TOPK), :] = e / jnp.sum(e, axis=0, keepdims=True)
        return carry

    lax.fori_loop(0, PEER_HEADS, head, 0)
    i1_ref[...] = jnp.transpose(i1_t[...])
    i2_ref[...] = jnp.transpose(i2_t[...])
    g_ref[...] = jnp.transpose(g_t[...])


def _peer_route(h, wq, k1, k2, tm):
    m, d = h.shape
    tm = min(tm, m)
    hk = PEER_HEADS * PEER_TOPK
    out = pl.BlockSpec((tm, hk), lambda i: (i, 0))
    keys = pl.BlockSpec(k1.shape, lambda i: (0, 0))
    return pl.pallas_call(
        _peer_route_kernel,
        grid=(m // tm,),
        in_specs=[pl.BlockSpec((tm, d), lambda i: (i, 0)), pl.BlockSpec(wq.shape, lambda i: (0, 0)), keys, keys],
        out_specs=[out, out, out],
        out_shape=[jax.ShapeDtypeStruct((m, hk), F32)] * 3,
        scratch_shapes=[pltpu.VMEM((tm, wq.shape[1]), F32)] + [pltpu.VMEM((hk, tm), F32)] * 3,
        compiler_params=_params(("arbitrary",), 40),
        name="peer_route",
    )(h, wq, k1, k2)


PEER_TOKENS_PER_TRIP = 8


def _peer_weights_kernel(i1_ref, i2_ref, g_ref, w_ref):
    tm = i1_ref.shape[0]
    hk = i1_ref.shape[1]
    key = lax.broadcasted_iota(jnp.int32, (PEER_NKEYS, hk), 0).astype(F32)

    def body(step, carry):
        for r in range(PEER_TOKENS_PER_TRIP):
            t = step * PEER_TOKENS_PER_TRIP + r
            i1 = jnp.broadcast_to(i1_ref[pl.ds(t, 1), :], (PEER_NKEYS, hk))
            i2 = jnp.broadcast_to(i2_ref[pl.ds(t, 1), :], (PEER_NKEYS, hk))
            gt = jnp.broadcast_to(g_ref[pl.ds(t, 1), :], (PEER_NKEYS, hk))
            p = jnp.where(i1 == key, 1.0, 0.0).astype(BF16)
            q = jnp.where(i2 == key, gt, 0.0).astype(BF16)
            w_ref[t] = _dot_nt(p, q).astype(BF16)
        return carry

    lax.fori_loop(0, tm // PEER_TOKENS_PER_TRIP, body, 0)


def _peer_weights(i1, i2, g, tm):
    m, hk = i1.shape
    tm = min(tm, m)
    spec = pl.BlockSpec((tm, hk), lambda i: (i, 0))
    return pl.pallas_call(
        _peer_weights_kernel,
        grid=(m // tm,),
        in_specs=[spec, spec, spec],
        out_specs=pl.BlockSpec((tm, PEER_NKEYS, PEER_NKEYS), lambda i: (i, 0, 0)),
        out_shape=jax.ShapeDtypeStruct((m, PEER_NKEYS, PEER_NKEYS), BF16),
        compiler_params=_params(("arbitrary",), 32),
        name="peer_weights",
    )(i1, i2, g)


def _peer_dense_kernel(h_ref, u_ref, v_ref, w_ref, x_ref, gate_ref, lng_ref, lnb_ref, scale_ref, shift_ref,
                       xo_ref, ho_ref, acc_ref):
    c = pl.program_id(1)

    @pl.when(c == 0)
    def _():
        acc_ref[...] = jnp.zeros_like(acc_ref)

    act = _gelu(_dot_nt(h_ref[...], u_ref[...]))
    acc_ref[...] += _dot((act * w_ref[...].astype(F32)).astype(BF16), v_ref[...])

    @pl.when(c == pl.num_programs(1) - 1)
    def _():
        xn, h = _ln_res(acc_ref[...], x_ref[...], gate_ref[...], lng_ref[...], lnb_ref[...],
                        scale_ref[...], shift_ref[...])
        xo_ref[...] = xn
        ho_ref[...] = h


def _peer_dense(h, u, v, w, xres, gate, lng, lnb, scale, shift, tm, te):
    m, d = h.shape
    n_exp = u.shape[0]
    tm = min(tm, m)
    bpb = (m // tm) // gate.shape[0]
    row = pl.BlockSpec((tm, d), lambda i, c: (i, 0))
    tab = pl.BlockSpec((te, d), lambda i, c: (c, 0))
    vec = pl.BlockSpec((1, d), lambda i, c: (0, 0))
    return pl.pallas_call(
        _peer_dense_kernel,
        grid=(m // tm, n_exp // te),
        in_specs=[row, tab, tab, pl.BlockSpec((tm, te), lambda i, c: (i, c)), row,
                  _mod_spec(gate, bpb), vec, vec, _mod_spec(scale, bpb), _mod_spec(shift, bpb)],
        out_specs=[row, row],
        out_shape=[jax.ShapeDtypeStruct((m, d), F32), jax.ShapeDtypeStruct((m, d), BF16)],
        scratch_shapes=[pltpu.VMEM((tm, d), F32)],
        compiler_params=_params(("arbitrary", "arbitrary"), 56),
        name="peer_dense",
    )(h, u, v, w, xres, gate, lng, lnb, scale, shift)


def _peer(h, xres, wq, k1, k2, u, v, gate, lng, lnb, scale, shift, tm_route, tm_dense, te):
    i1, i2, g = _peer_route(h, wq, k1, k2, tm_route)
    w = _peer_weights(i1, i2, g, tm_route).reshape(h.shape[0], PEER_NKEYS * PEER_NKEYS)
    return _peer_dense(h, u, v, w, xres, gate, lng, lnb, scale, shift, tm_dense, te)


def _rope_tables(pos):
    half = ROPE_DIMS // 2
    inv = jnp.power(ROPE_THETA, -jnp.arange(half, dtype=F32) / half)
    ang = pos.astype(F32)[:, None] * inv
    cos, sin = jnp.cos(ang), jnp.sin(ang)
    n = pos.shape[0]
    ones = jnp.ones((n, HEAD_DIM - ROPE_DIMS), F32)
    zeros = jnp.zeros((n, HEAD_DIM - ROPE_DIMS), F32)
    zh = jnp.zeros((n, half), F32)
    return (jnp.concatenate([cos, cos, ones], axis=1),
            jnp.concatenate([-sin, zh, zeros], axis=1),
            jnp.concatenate([zh, sin, zeros], axis=1))


def _pad_rows(x, rows):
    return jnp.pad(x, ((0, rows - x.shape[0]),) + ((0, 0),) * (x.ndim - 1))


def kernel(x_prompt, x_sample, c_prompt, c_sample, cache_l0_kv, cache_l1_kv, state_l1_win, cache_l2_kv, cache_l3_kv,
           state_l3_win, page_table, ada_w, ada_b, ln1_g, ln1_b, ln2_g, ln2_b, sb_w_in, sb_w_o, nsa_w_in, nsa_w_o,
           nsa_cmp_pe, nsa_cmp_w1, nsa_cmp_b1, nsa_cmp_w2, nsa_cmp_b2, peer_wq, peer_k1, peer_k2, peer_u, peer_v):
    batch, seq, d = x_prompt.shape
    bd, dec_seq, _ = x_sample.shape
    assert dec_seq == 1 and d == D_MODEL
    depth = ada_w.shape[0]
    n_pages = page_table.shape[1]
    past = n_pages * PAGE_SIZE
    sb_caches = (cache_l0_kv, cache_l2_kv)
    nsa_caches = (cache_l1_kv, cache_l3_kv)
    nsa_wins = (state_l1_win, state_l3_win)
    win_buf = state_l1_win.shape[1]
    assert win_buf == NSA_WINDOW and past % NSA_BLOCK == 0 and seq >= win_buf
    mp_rows = batch * seq
    tm = 512

    n_c = batch + bd
    c_all = _pad_rows(jnp.concatenate([c_prompt, c_sample], axis=0), -(-n_c // 8) * 8)
    mod = _modulation(c_all, ada_w, ada_b).reshape(depth, c_all.shape[0], 6, d)
    mod_p = lambda i, k: mod[i, :batch, k].reshape(batch, 1, d)
    mod_s = lambda i, k: mod[i, batch:n_c, k].reshape(1, bd, d)

    xp = x_prompt.reshape(mp_rows, d)
    xs = x_sample.reshape(bd, d)
    hp = _modulate(xp, mod_p(0, 1), mod_p(0, 0), tm)
    hs = _modulate(xs, mod_s(0, 1), mod_s(0, 0), bd)

    rope_p = _rope_tables(jnp.arange(seq))
    rope_s = _rope_tables(jnp.full((bd,), past))
    nb_p = -(-seq // NSA_BLOCK)
    assert nb_p * NSA_BLOCK == seq
    nb_s = past // NSA_BLOCK + 1

    new = []
    for i in range(depth):
        j = i // 2
        vec = lambda a: a[i].reshape(1, d)
        if i % 2 == 0:
            w_in = sb_w_in[j].astype(BF16)
            w_o = sb_w_o[j].astype(BF16)
            sb_rows = (2, N_HEADS)
            q_p = _matmul(hp, w_in[:, :d], tm, 512, out_dtype=BF16, name="sb_q")
            kv_p, kv_rows_p = _matmul(hp, w_in[:, d:], tm, 1024, out_dtype=BF16, rows_out=sb_rows, name="sb_kv")
            o_p = _sb_prompt_attention(q_p, kv_p, batch, seq)
            q_s = _matmul(hs, w_in[:, :d], bd, 512, out_dtype=BF16, name="sb_q")
            _, kv_rows_s = _matmul(hs, w_in[:, d:], bd, 1024, out_dtype=BF16, rows_out=sb_rows, name="sb_kv")
            o_s = _sb_sample_attention(q_s, sb_caches[j], page_table)
            new.append((kv_rows_p.reshape(batch, seq, 2, N_HEADS, HEAD_DIM),
                        kv_rows_s.reshape(bd, 1, 2, N_HEADS, HEAD_DIM)))
        else:
            w_in = nsa_w_in[j].astype(BF16)
            w_o = nsa_w_o[j].astype(BF16)
            w_q = w_in[:, :d]
            w_kv4 = w_in[:, d:d + 4 * NSA_KV]
            w_kw = w_in[:, d + 4 * NSA_KV:d + 6 * NSA_KV]
            w_g = jnp.pad(w_in[:, d + 6 * NSA_KV:], ((0, 0), (0, LANES - 3 * N_HEADS)))
            cmp_w = (nsa_cmp_pe[j], nsa_cmp_w1[j].astype(BF16), nsa_cmp_b1[j], nsa_cmp_w2[j], nsa_cmp_b2[j])

            def project(h, rope, rows):
                kv4, kv4_rows = _matmul(h, w_kv4, rows, NSA_KV, epilogue="rope_even", rope=rope,
                                        rows_out=(4, NSA_KV_GROUPS), name="nsa_kv4")
                return (_matmul(h, w_q, rows, 512, epilogue="rope_all", rope=rope, name="nsa_q"), kv4,
                        _matmul(h, w_kw, rows, NSA_KV, epilogue="rope_even", rope=rope, name="nsa_kwin"),
                        _matmul(h, w_g, rows, LANES, epilogue="sigmoid", name="nsa_gates"), kv4_rows)

            q_p, kv4_p, kwin_p, gates_p, kv4_rows_p = project(hp, rope_p, tm)
            cmp_p = _compress(kv4_p.reshape(batch * nb_p, NSA_BLOCK, 4 * NSA_KV), *cmp_w, 256)
            nbp = -(-nb_p // LANES) * LANES
            cmp_p = jnp.pad(cmp_p.reshape(2, batch, nb_p, NSA_KV), ((0, 0), (0, 0), (0, nbp - nb_p), (0, 0)))
            o_p = _nsa_prompt_attention(q_p, kv4_p, kwin_p, gates_p, cmp_p[0], cmp_p[1], batch, seq, nb_p)
            win_p = kwin_p.reshape(batch, seq, 2, NSA_KV_GROUPS, HEAD_DIM)[:, seq - win_buf:]
            q_s, kv4_s, kwin_s, gates_s, kv4_rows_s = project(hs, rope_s, bd)
            x_past = _gather_cmp_pages(nsa_caches[j], page_table)
            cmp_past = _compress(x_past.reshape(bd * (nb_s - 1), NSA_BLOCK, 2 * NSA_KV), *cmp_w, 256)
            x_last = jnp.pad(kv4_s[:, None, :2 * NSA_KV], ((0, 0), (0, NSA_BLOCK - 1), (0, 0)))
            cmp_last = _compress(x_last, *cmp_w, bd)
            nbs = -(-nb_s // LANES) * LANES
            cmp_s = jnp.concatenate([cmp_past.reshape(2, bd, nb_s - 1, NSA_KV), cmp_last.reshape(2, bd, 1, NSA_KV)], 2)
            cmp_s = jnp.pad(cmp_s, ((0, 0), (0, 0), (0, nbs - nb_s), (0, 0)))
            ocmp_s, idx_s = _nsa_sample_select(q_s, cmp_s[0], cmp_s[1], nb_s, past)
            o_s = _nsa_sample_attend(idx_s, page_table, q_s, ocmp_s, gates_s, kv4_s, kwin_s, nsa_wins[j], nsa_caches[j])
            win_s = jnp.concatenate([nsa_wins[j][:, 1:], kwin_s.reshape(bd, 1, 2, NSA_KV_GROUPS, HEAD_DIM)], axis=1)
            new.append((kv4_rows_p.reshape(batch, seq, 4, NSA_KV_GROUPS, HEAD_DIM),
                        kv4_rows_s.reshape(bd, 1, 4, NSA_KV_GROUPS, HEAD_DIM), win_p, win_s))

        xp, hp = _matmul_ln(o_p, w_o, xp, mod_p(i, 2), vec(ln1_g), vec(ln1_b), mod_p(i, 4), mod_p(i, 3), tm)
        xs, hs = _matmul_ln(o_s, w_o, xs, mod_s(i, 2), vec(ln1_g), vec(ln1_b), mod_s(i, 4), mod_s(i, 3), bd)

        nxt = min(i + 1, depth - 1)
        wq = peer_wq[i].astype(BF16)
        u = peer_u[i].astype(BF16)
        v = peer_v[i].astype(BF16)
        xp, hp = _peer(hp, xp, wq, peer_k1[i], peer_k2[i], u, v, mod_p(i, 5), vec(ln2_g), vec(ln2_b),
                       mod_p(nxt, 1), mod_p(nxt, 0), 256, 512, 512)
        pad = LANES
        hs_pad = _pad_rows(hs, pad)
        gate_s = _pad_rows(mod_s(i, 5)[0], pad)[None]
        scale_s = _pad_rows(mod_s(nxt, 1)[0], pad)[None]
        shift_s = _pad_rows(mod_s(nxt, 0)[0], pad)[None]
        xs_pad, hs_pad = _peer(hs_pad, _pad_rows(xs, pad), wq, peer_k1[i], peer_k2[i], u, v, gate_s,
                               vec(ln2_g), vec(ln2_b), scale_s, shift_s, pad, pad, 512)
        xs, hs = xs_pad[:bd], hs_pad[:bd]

    return (xp.reshape(batch, seq, d), xs.reshape(bd, 1, d), new[0][0], new[0][1], new[1][0], new[1][1], new[1][2],
            new[1][3], new[2][0], new[2][1], new[3][0], new[3][1], new[3][2], new[3][3])
```

```python
import functools

import jax
import jax.numpy as jnp
import numpy as np
from jax import lax
from jax.experimental import pallas as pl
from jax.experimental.pallas import tpu as pltpu

F32 = jnp.float32
BF16 = jnp.bfloat16

D_MODEL = 2048
HEAD_DIM = 128
N_HEADS = D_MODEL // HEAD_DIM
PAGE_SIZE = 128
NSA_KV_GROUPS = 4
NSA_GROUP = N_HEADS // NSA_KV_GROUPS
NSA_KV = NSA_KV_GROUPS * HEAD_DIM
NSA_BLOCK = 64
NSA_TOPN = 16
NSA_WINDOW = 512
ROPE_THETA = 500000.0
ROPE_DIMS = HEAD_DIM // 4
PEER_HEADS = 8
PEER_NKEYS = 128
PEER_TOPK = 16
DEPTH = 4
DEEPNORM_ALPHA = (2 * DEPTH) ** 0.25
LN_EPS = 1e-5
NEG_INF = -1e30
FORCE_SCORE = 1e9
ATTN_SCALE = HEAD_DIM ** -0.5
SQRT_HALF = float(np.sqrt(0.5))

LANES = 128
VMEM_LIMIT_CAP = 56 * 1024 * 1024
SB_LOG_UNDERFLOW = -104.0


def _params(sem, vmem_mb):
    return pltpu.CompilerParams(
        dimension_semantics=sem, vmem_limit_bytes=min(vmem_mb * 1024 * 1024, VMEM_LIMIT_CAP))


def _dot(a, b):
    return jnp.dot(a, b, preferred_element_type=F32)


def _dot_nt(a, b):
    return lax.dot_general(a, b, (((1,), (1,)), ((), ())), preferred_element_type=F32)


def _gelu(x):
    return 0.5 * x * (1.0 + lax.erf(x * SQRT_HALF))


def _log_sigmoid_pair(z):
    l1p = jnp.log(1.0 + jnp.exp(-jnp.abs(z)))
    return jnp.minimum(z, 0.0) - l1p, jnp.minimum(-z, 0.0) - l1p


def _ln_res(y, xres, gate, lng, lnb, scale, shift):
    v = DEEPNORM_ALPHA * xres + gate * y
    mu = jnp.mean(v, axis=-1, keepdims=True)
    d = v - mu
    var = jnp.mean(d * d, axis=-1, keepdims=True)
    xn = d * lax.rsqrt(var + LN_EPS) * lng + lnb
    return xn, (xn * (1.0 + scale) + shift).astype(BF16)


def _mod_spec(arr, blocks_per_batch):
    return pl.BlockSpec((None,) + arr.shape[1:], lambda i, *_: (i // blocks_per_batch, 0, 0))


def _modulation_kernel(c_ref, w_ref, b_ref, o_ref):
    c = c_ref[...]
    s = (c * jax.nn.sigmoid(c)).astype(BF16)
    o_ref[...] = _dot(s, w_ref[...].astype(BF16)) + b_ref[...]


def _modulation(c_all, ada_w, ada_b):
    depth, d, n = ada_w.shape
    r = c_all.shape[0]
    tn = 1024
    return pl.pallas_call(
        _modulation_kernel,
        grid=(depth, n // tn),
        in_specs=[
            pl.BlockSpec((r, d), lambda l, j: (0, 0)),
            pl.BlockSpec((None, d, tn), lambda l, j: (l, 0, j)),
            pl.BlockSpec((None, 1, tn), lambda l, j: (l, 0, j)),
        ],
        out_specs=pl.BlockSpec((None, r, tn), lambda l, j: (l, 0, j)),
        out_shape=jax.ShapeDtypeStruct((depth, r, n), F32),
        compiler_params=_params(("arbitrary", "arbitrary"), 32),
        name="modulation",
    )(c_all, ada_w, ada_b.reshape(depth, 1, n))


def _modulate_kernel(x_ref, scale_ref, shift_ref, o_ref):
    o_ref[...] = (x_ref[...] * (1.0 + scale_ref[...]) + shift_ref[...]).astype(BF16)


def _modulate(x, scale, shift, tm):
    m, d = x.shape
    bpb = (m // tm) // scale.shape[0]
    return pl.pallas_call(
        _modulate_kernel,
        grid=(m // tm,),
        in_specs=[pl.BlockSpec((tm, d), lambda i: (i, 0)), _mod_spec(scale, bpb), _mod_spec(shift, bpb)],
        out_specs=pl.BlockSpec((tm, d), lambda i: (i, 0)),
        out_shape=jax.ShapeDtypeStruct((m, d), BF16),
        compiler_params=_params(("arbitrary",), 32),
        name="modulate",
    )(x, scale, shift)


def _store_tile(y, o_refs):
    o_refs[0][...] = y.astype(o_refs[0].dtype)
    if len(o_refs) > 1:
        for h in range(o_refs[1].shape[1]):
            o_refs[1][:, h, :] = y[:, h * HEAD_DIM:(h + 1) * HEAD_DIM]


def _mm_kernel(a_ref, w_ref, *o_refs):
    _store_tile(_dot(a_ref[...], w_ref[...]), o_refs)


def _mm_sigmoid_kernel(a_ref, w_ref, o_ref):
    o_ref[...] = jax.nn.sigmoid(_dot(a_ref[...], w_ref[...]))


def _mm_rope_kernel(a_ref, w_ref, cos_ref, sa_ref, sb_ref, *o_refs, even_only):
    acc = _dot(a_ref[...], w_ref[...])
    tn = acc.shape[1]
    reps = tn // HEAD_DIM
    cos = jnp.concatenate([cos_ref[...]] * reps, axis=1)
    sa = jnp.concatenate([sa_ref[...]] * reps, axis=1)
    sb = jnp.concatenate([sb_ref[...]] * reps, axis=1)
    half = ROPE_DIMS // 2
    rot = acc * cos + pltpu.roll(acc, tn - half, 1) * sa + pltpu.roll(acc, half, 1) * sb
    if even_only:
        rot = jnp.where(pl.program_id(1) % 2 == 0, rot, acc)
    _store_tile(rot, o_refs)


def _matmul(a, w, tm, tn, *, epilogue=None, rope=None, out_dtype=F32, rows_out=None, name="matmul"):
    m, k = a.shape
    n = w.shape[1]
    tm = min(tm, m)
    in_specs = [pl.BlockSpec((tm, k), lambda i, j: (i, 0)), pl.BlockSpec((k, tn), lambda i, j: (0, j))]
    args = [a, w]
    if epilogue in ("rope_all", "rope_even"):
        assert epilogue == "rope_all" or tn == NSA_KV
        t_tab = rope[0].shape[0]
        nt = t_tab // tm
        in_specs += [pl.BlockSpec((tm, HEAD_DIM), lambda i, j: (i % nt, 0))] * 3
        args += list(rope)
        kern = functools.partial(_mm_rope_kernel, even_only=(epilogue == "rope_even"))
    elif epilogue == "sigmoid":
        kern = _mm_sigmoid_kernel
    else:
        kern = _mm_kernel
    out_specs = pl.BlockSpec((tm, tn), lambda i, j: (i, j))
    out_shape = jax.ShapeDtypeStruct((m, n), out_dtype)
    if rows_out is not None:
        slots, heads = rows_out
        tiles_per_slot = heads * HEAD_DIM // tn
        out_specs = [out_specs, pl.BlockSpec((tm, None, heads // tiles_per_slot, HEAD_DIM),
                                             lambda i, j: (i, j // tiles_per_slot, j % tiles_per_slot, 0))]
        out_shape = [out_shape, jax.ShapeDtypeStruct((m, slots, heads, HEAD_DIM), F32)]
    return pl.pallas_call(
        kern,
        grid=(m // tm, n // tn),
        in_specs=in_specs,
        out_specs=out_specs,
        out_shape=out_shape,
        compiler_params=_params(("arbitrary", "arbitrary"), 40),
        name=name,
    )(*args)


def _mm_ln_kernel(a_ref, w_ref, x_ref, gate_ref, lng_ref, lnb_ref, scale_ref, shift_ref, xo_ref, ho_ref):
    y = _dot(a_ref[...].astype(BF16), w_ref[...])
    xn, h = _ln_res(y, x_ref[...], gate_ref[...], lng_ref[...], lnb_ref[...], scale_ref[...], shift_ref[...])
    xo_ref[...] = xn
    ho_ref[...] = h


def _matmul_ln(a, w, xres, gate, lng, lnb, scale, shift, tm):
    m, k = a.shape
    d = w.shape[1]
    tm = min(tm, m)
    bpb = (m // tm) // gate.shape[0]
    row = pl.BlockSpec((tm, d), lambda i: (i, 0))
    vec = pl.BlockSpec((1, d), lambda i: (0, 0))
    return pl.pallas_call(
        _mm_ln_kernel,
        grid=(m // tm,),
        in_specs=[pl.BlockSpec((tm, k), lambda i: (i, 0)), pl.BlockSpec((k, d), lambda i: (0, 0)), row,
                  _mod_spec(gate, bpb), vec, vec, _mod_spec(scale, bpb), _mod_spec(shift, bpb)],
        out_specs=[row, row],
        out_shape=[jax.ShapeDtypeStruct((m, d), F32), jax.ShapeDtypeStruct((m, d), BF16)],
        compiler_params=_params(("arbitrary",), 48),
        name="matmul_ln",
    )(a, w, xres, gate, lng, lnb, scale, shift)


def _suffix_matrix(n):
    return (lax.broadcasted_iota(jnp.int32, (n, n), 0) > lax.broadcasted_iota(jnp.int32, (n, n), 1)).astype(BF16)


def _suffix_sum(x, lmat):
    hi = x.astype(BF16)
    lo = (x - hi.astype(F32)).astype(BF16)
    return _dot(hi, lmat) + _dot(lo, lmat)


SB_HEADS_PER_STEP = 8


def _sb_prompt_kernel(q_ref, k_ref, v_ref, o_ref, carry_ref, acc_ref, *, tq, tk):
    i = pl.program_id(2)
    carry_ref[...] = jnp.zeros_like(carry_ref)
    acc_ref[...] = jnp.zeros_like(acc_ref)
    qpos = i * tq + lax.broadcasted_iota(jnp.int32, (tq, 1), 0)
    lmat = jnp.concatenate([_suffix_matrix(tk), jnp.ones((tk, tk), BF16)], axis=1)

    def body(state):
        j, _ = state
        k0 = pl.multiple_of(j * tk, tk)
        allowed = (k0 + lax.broadcasted_iota(jnp.int32, (1, tk), 1)) < qpos
        heads = range(SB_HEADS_PER_STEP)
        lanes = [slice(hh * HEAD_DIM, (hh + 1) * HEAD_DIM) for hh in heads]
        zs = [_dot_nt(q_ref[:, lanes[hh]].astype(BF16), k_ref[pl.ds(k0, tk), lanes[hh]].astype(BF16)) * ATTN_SCALE
              for hh in heads]
        pairs = [_log_sigmoid_pair(z) for z in zs]
        sums = [_suffix_sum(jnp.where(allowed, lsn, 0.0), lmat) for _, lsn in pairs]
        carries = [carry_ref[hh] for hh in heads]
        probs = [jnp.where(allowed, jnp.exp(pairs[hh][0] + sums[hh][:, :tk] + carries[hh]), 0.0).astype(BF16)
                 for hh in heads]
        pvs = [_dot(probs[hh], v_ref[pl.ds(k0, tk), lanes[hh]].astype(BF16)) for hh in heads]
        top = jnp.full((tq, tk), -jnp.inf, F32)
        for hh in heads:
            acc_ref[:, lanes[hh]] += pvs[hh]
            carry = carries[hh] + sums[hh][:, tk:]
            carry_ref[hh] = carry
            top = jnp.maximum(top, carry)
        return j - 1, (jnp.max(top) > SB_LOG_UNDERFLOW).astype(jnp.int32)

    lax.while_loop(lambda s: jnp.logical_and(s[0] >= 0, s[1] > 0), body, ((i * tq + tq - 1) // tk, jnp.int32(1)))
    o_ref[...] = acc_ref[...]


def _sb_prompt_attention(q, kv, batch, seq, tq=256, tk=128):
    nq = seq // tq
    hb = SB_HEADS_PER_STEP
    width = hb * HEAD_DIM
    n_hb = N_HEADS // hb
    return pl.pallas_call(
        functools.partial(_sb_prompt_kernel, tq=tq, tk=tk),
        grid=(batch, n_hb, nq),
        in_specs=[
            pl.BlockSpec((tq, width), lambda b, h, i: (b * nq + i, h)),
            pl.BlockSpec((seq, width), lambda b, h, i: (b, h)),
            pl.BlockSpec((seq, width), lambda b, h, i: (b, n_hb + h)),
        ],
        out_specs=pl.BlockSpec((tq, width), lambda b, h, i: (b * nq + i, h)),
        out_shape=jax.ShapeDtypeStruct(q.shape, F32),
        scratch_shapes=[pltpu.VMEM((hb, tq, tk), F32), pltpu.VMEM((tq, width), F32)],
        compiler_params=_params(("arbitrary", "arbitrary", "arbitrary"), 48),
        name="sb_prompt_attention",
    )(q, kv, kv)


SB_ROW_REP = 8


def _sb_sample_kernel(pt_ref, q_ref, cache_ref, o_ref, kbuf, vbuf, sem, carry_ref, acc_ref, *, n_pages):
    b = pl.program_id(0)
    rep = SB_ROW_REP
    q = q_ref[...]
    qh = [jnp.broadcast_to(q[:, h * HEAD_DIM:(h + 1) * HEAD_DIM], (rep, HEAD_DIM)).astype(BF16)
          for h in range(N_HEADS)]
    lmat = _suffix_matrix(PAGE_SIZE)
    carry_ref[...] = jnp.zeros_like(carry_ref)
    acc_ref[...] = jnp.zeros_like(acc_ref)

    def page_copies(page):
        return (pltpu.make_async_copy(cache_ref.at[page, :, 0], kbuf, sem.at[0]),
                pltpu.make_async_copy(cache_ref.at[page, :, 1], vbuf, sem.at[1]))

    def body(state):
        p, _ = state
        ck, cv = page_copies(pt_ref[b * n_pages + p])
        ck.start()
        cv.start()
        ck.wait()
        cv.wait()
        z = jnp.concatenate([_dot_nt(qh[h], kbuf[:, h, :].astype(BF16)) for h in range(N_HEADS)], axis=0)
        ls, lsn = _log_sigmoid_pair(z * ATTN_SCALE)
        later = _suffix_sum(lsn, lmat) + carry_ref[...]
        a = jnp.exp(ls + later)
        pv = [_dot(a[h * rep:(h + 1) * rep, :].astype(BF16), vbuf[:, h, :].astype(BF16)) for h in range(N_HEADS)]
        acc_ref[...] += jnp.concatenate(pv, axis=0)
        carry = carry_ref[...] + jnp.sum(lsn, axis=1, keepdims=True)
        carry_ref[...] = carry
        return p - 1, (jnp.max(carry) > SB_LOG_UNDERFLOW).astype(jnp.int32)

    lax.while_loop(lambda s: jnp.logical_and(s[0] >= 0, s[1] > 0), body, (jnp.int32(n_pages - 1), jnp.int32(1)))
    for h in range(N_HEADS):
        o_ref[:, h * HEAD_DIM:(h + 1) * HEAD_DIM] = acc_ref[h * rep:h * rep + 1, :]


def _sb_sample_attention(q, cache_kv, page_table):
    bd, n_pages = page_table.shape
    page_buf = pltpu.VMEM((PAGE_SIZE, N_HEADS, HEAD_DIM), F32)
    out = pl.pallas_call(
        functools.partial(_sb_sample_kernel, n_pages=n_pages),
        grid_spec=pltpu.PrefetchScalarGridSpec(
            num_scalar_prefetch=1,
            grid=(bd,),
            in_specs=[pl.BlockSpec((None, 1, D_MODEL), lambda b, pt: (b, 0, 0)),
                      pl.BlockSpec(memory_space=pl.ANY)],
            out_specs=pl.BlockSpec((None, 1, D_MODEL), lambda b, pt: (b, 0, 0)),
            scratch_shapes=[page_buf, page_buf, pltpu.SemaphoreType.DMA((2,)),
                            pltpu.VMEM((N_HEADS * SB_ROW_REP, 1), F32),
                            pltpu.VMEM((N_HEADS * SB_ROW_REP, HEAD_DIM), F32)]),
        out_shape=jax.ShapeDtypeStruct((bd, 1, D_MODEL), F32),
        compiler_params=_params(("arbitrary",), 16),
        name="sb_sample_attention",
    )(page_table.reshape(-1), q.reshape(bd, 1, D_MODEL), cache_kv)
    return out.reshape(bd, D_MODEL)


def _compress_kernel(x_ref, pe_ref, w1_ref, b1_ref, w2_ref, b2_ref, o_ref):
    acc = jnp.zeros((x_ref.shape[0], w1_ref.shape[1]), F32)
    for t in range(NSA_BLOCK):
        xt = (x_ref[:, t, :] + pe_ref[t:t + 1, :]).astype(BF16)
        acc = acc + _dot(xt, w1_ref[t * HEAD_DIM:(t + 1) * HEAD_DIM, :])
    hid = _gelu(acc + b1_ref[...])
    o_ref[...] = _dot(hid.astype(BF16), w2_ref[...].astype(BF16)) + b2_ref[...]


def _compress(x, pe, w1, b1, w2, b2, tr):
    r = x.shape[0]
    tr = min(tr, r)
    hidden = w1.shape[2]
    return pl.pallas_call(
        _compress_kernel,
        grid=(2, NSA_KV_GROUPS, r // tr),
        in_specs=[
            pl.BlockSpec((tr, NSA_BLOCK, HEAD_DIM), lambda s, g, i: (i, 0, s * NSA_KV_GROUPS + g)),
            pl.BlockSpec((None, NSA_BLOCK, HEAD_DIM), lambda s, g, i: (s, 0, 0)),
            pl.BlockSpec((None, NSA_BLOCK * HEAD_DIM, hidden), lambda s, g, i: (s, 0, 0)),
            pl.BlockSpec((None, 1, hidden), lambda s, g, i: (s, 0, 0)),
            pl.BlockSpec((None, hidden, HEAD_DIM), lambda s, g, i: (s, 0, 0)),
            pl.BlockSpec((None, 1, HEAD_DIM), lambda s, g, i: (s, 0, 0)),
        ],
        out_specs=pl.BlockSpec((None, tr, HEAD_DIM), lambda s, g, i: (s, i, g)),
        out_shape=jax.ShapeDtypeStruct((2, r, NSA_KV), F32),
        compiler_params=_params(("arbitrary", "arbitrary", "arbitrary"), 48),
        name="nsa_compress",
    )(x, pe, w1, b1.reshape(2, 1, hidden), w2, b2.reshape(2, 1, HEAD_DIM))


def _page_gather_kernel(pt_ref, *refs):
    o_ref = refs[-1]
    for k, r in enumerate(refs[:-1]):
        for s in range(2):
            for g in range(NSA_KV_GROUPS):
                lanes = slice((s * NSA_KV_GROUPS + g) * HEAD_DIM, (s * NSA_KV_GROUPS + g + 1) * HEAD_DIM)
                o_ref[k * PAGE_SIZE:(k + 1) * PAGE_SIZE, lanes] = r[:, s, g, :]


def _gather_cmp_pages(cache_kv, page_table, pages_per_step=8):
    bd, n_pages = page_table.shape
    width = 2 * NSA_KV
    pps = pages_per_step

    def in_map(k):
        return lambda b, c, pt: (pt[b * n_pages + c * pps + k], 0, 0, 0, 0)

    return pl.pallas_call(
        _page_gather_kernel,
        grid_spec=pltpu.PrefetchScalarGridSpec(
            num_scalar_prefetch=1,
            grid=(bd, n_pages // pps),
            in_specs=[pl.BlockSpec((None, PAGE_SIZE, 2, NSA_KV_GROUPS, HEAD_DIM), in_map(k)) for k in range(pps)],
            out_specs=pl.BlockSpec((None, pps * PAGE_SIZE, width), lambda b, c, pt: (b, c, 0))),
        out_shape=jax.ShapeDtypeStruct((bd, n_pages * PAGE_SIZE, width), F32),
        compiler_params=_params(("arbitrary", "arbitrary"), 40),
        name="nsa_page_gather",
    )(page_table.reshape(-1), *([cache_kv] * pps))


def _online_softmax_step(z, v_t, m_ref, l_ref, acc_ref):
    m_new = jnp.maximum(m_ref[...], jnp.max(z, axis=0, keepdims=True))
    alpha = jnp.exp(m_ref[...] - m_new)
    p = jnp.exp(z - m_new)
    l_ref[...] = alpha * l_ref[...] + jnp.sum(p, axis=0, keepdims=True)
    acc_ref[...] = alpha * acc_ref[...] + _dot(v_t, p.astype(BF16))
    m_ref[...] = m_new


def _nsa_prompt_kernel(q_ref, kc_ref, vc_ref, ks_ref, vs_ref, kw_ref, vw_ref, gate_ref, o_ref,
                       m_ref, l_ref, acc_ref, osel_ref, gate_t_ref, *, nb, tq, tk):
    g = pl.program_id(1)
    i = pl.program_id(2)
    nq = NSA_GROUP * tq
    nbp = kc_ref.shape[0]
    qblk = q_ref[...]
    qg = jnp.concatenate([qblk[:, r * HEAD_DIM:(r + 1) * HEAD_DIM] for r in range(NSA_GROUP)], axis=0).astype(BF16)
    qpos = i * tq + lax.broadcasted_iota(jnp.int32, (1, tq), 1)
    tile4 = lambda x: jnp.concatenate([x] * NSA_GROUP, axis=1)

    blk = lax.broadcasted_iota(jnp.int32, (nbp, 1), 0)
    zc = _dot_nt(kc_ref[...].astype(BF16), qg) * ATTN_SCALE
    avail = tile4(jnp.logical_and((blk + 1) * NSA_BLOCK - 1 <= qpos, blk < nb))
    zm = jnp.where(avail, zc, NEG_INF)
    e = jnp.where(avail, jnp.exp(zm - jnp.max(zm, axis=0, keepdims=True)), 0.0)
    den = jnp.sum(e, axis=0, keepdims=True)
    pc = e / jnp.where(den > 0.0, den, 1.0)
    o_cmp = _dot(jnp.transpose(vc_ref[...]).astype(BF16), pc.astype(BF16))

    score = pc[:, 0:tq]
    for r in range(1, NSA_GROUP):
        score = score + pc[:, r * tq:(r + 1) * tq]
    nbr = -(-nb // 8) * 8
    blk = blk[0:nbr]
    cur = qpos // NSA_BLOCK
    forced = jnp.logical_or(blk == 0, jnp.logical_or(blk == cur, blk == cur - 1))
    s = jnp.where(blk <= cur, jnp.where(forced, FORCE_SCORE, score[0:nbr]), -1.0)
    rank = jnp.zeros((nbr, tq), F32)
    for mth in range(nb):
        sm = s[mth:mth + 1, :]
        ahead = jnp.logical_or(sm > s, jnp.logical_and(sm == s, blk > mth))
        rank = rank + jnp.where(ahead, 1.0, 0.0)
    sel = jnp.where(jnp.logical_and(rank < float(min(NSA_TOPN, nb)), s >= 0.0), 1.0, 0.0).astype(BF16)

    def reset():
        m_ref[...] = jnp.full_like(m_ref, NEG_INF)
        l_ref[...] = jnp.zeros_like(l_ref)
        acc_ref[...] = jnp.zeros_like(acc_ref)

    reset()
    bpt = tk // NSA_BLOCK

    def sel_body(kt, carry):
        k0 = pl.multiple_of(kt * tk, tk)
        z = _dot_nt(ks_ref[pl.ds(k0, tk), :].astype(BF16), qg) * ATTN_SCALE
        key = lax.broadcasted_iota(jnp.int32, (tk, nbr), 0)
        col = lax.broadcasted_iota(jnp.int32, (tk, nbr), 1)
        expand = (col == kt * bpt + key // NSA_BLOCK).astype(BF16)
        chosen = _dot(expand, sel) > 0.5
        kpos = k0 + lax.broadcasted_iota(jnp.int32, (tk, 1), 0)
        bias = jnp.where(jnp.logical_and(chosen, kpos <= qpos), 0.0, NEG_INF)
        v_t = jnp.transpose(vs_ref[pl.ds(k0, tk), :]).astype(BF16)
        _online_softmax_step(z + tile4(bias), v_t, m_ref, l_ref, acc_ref)
        return carry

    lax.fori_loop(0, (i * tq + tq - 1) // tk + 1, sel_body, 0)
    osel_ref[...] = acc_ref[...] / l_ref[...]

    reset()

    def win_body(kt, carry):
        k0 = pl.multiple_of(kt * tq, tq)
        z = _dot_nt(kw_ref[pl.ds(k0, tq), :].astype(BF16), qg) * ATTN_SCALE
        dist = qpos - (k0 + lax.broadcasted_iota(jnp.int32, (tq, 1), 0))
        bias = jnp.where(jnp.logical_and(dist >= 0, dist <= NSA_WINDOW), 0.0, NEG_INF)
        v_t = jnp.transpose(vw_ref[pl.ds(k0, tq), :]).astype(BF16)
        _online_softmax_step(z + tile4(bias), v_t, m_ref, l_ref, acc_ref)
        return carry

    lax.fori_loop(jnp.maximum(i - NSA_WINDOW // tq, 0), i + 1, win_body, 0)
    o_win = acc_ref[...] / l_ref[...]
    o_sel = osel_ref[...]

    gate_t_ref[...] = jnp.transpose(gate_ref[...])
    for r in range(NSA_GROUP):
        lanes = slice(r * tq, (r + 1) * tq)
        out_t = jnp.zeros((HEAD_DIM, tq), F32)
        for c, o_c in enumerate((o_cmp, o_sel, o_win)):
            gate = gate_t_ref[pl.ds(c * N_HEADS + g * NSA_GROUP + r, 1), :]
            out_t = out_t + gate * o_c[:, lanes]
        o_ref[:, r * HEAD_DIM:(r + 1) * HEAD_DIM] = jnp.transpose(out_t)


def _nsa_prompt_attention(q, kv4, kwin, gates, kc, vc, batch, seq, nb):
    tq = 256
    tk = 512
    nq = seq // tq
    g4 = NSA_KV_GROUPS
    col = lambda off: pl.BlockSpec((seq, HEAD_DIM), lambda b, g, i: (b, off + g))
    cmp_spec = pl.BlockSpec((None, kc.shape[1], HEAD_DIM), lambda b, g, i: (b, 0, g))
    return pl.pallas_call(
        functools.partial(_nsa_prompt_kernel, nb=nb, tq=tq, tk=tk),
        grid=(batch, g4, nq),
        in_specs=[
            pl.BlockSpec((tq, NSA_KV), lambda b, g, i: (b * nq + i, g)),
            cmp_spec, cmp_spec,
            col(2 * g4), col(3 * g4),
            col(0), col(g4),
            pl.BlockSpec((tq, LANES), lambda b, g, i: (b * nq + i, 0)),
        ],
        out_specs=pl.BlockSpec((tq, NSA_KV), lambda b, g, i: (b * nq + i, g)),
        out_shape=jax.ShapeDtypeStruct(q.shape, F32),
        scratch_shapes=[pltpu.VMEM((1, NSA_GROUP * tq), F32), pltpu.VMEM((1, NSA_GROUP * tq), F32),
                        pltpu.VMEM((HEAD_DIM, NSA_GROUP * tq), F32), pltpu.VMEM((HEAD_DIM, NSA_GROUP * tq), F32),
                        pltpu.VMEM((LANES, tq), F32)],
        compiler_params=_params(("arbitrary", "arbitrary", "arbitrary"), 40),
        name="nsa_prompt_attention",
    )(q, kc, vc, kv4, kv4, kwin, kwin, gates)


def _nsa_sample_select_kernel(q_ref, kc_ref, vc_ref, ocmp_ref, idx_ref, *, nb, qpos):
    nbp = kc_ref.shape[0]
    q = q_ref[...]
    lane_blk = lax.broadcasted_iota(jnp.int32, (1, nbp), 1)
    row_i = lax.broadcasted_iota(jnp.int32, (nbp, nbp), 0)
    col_i = lax.broadcasted_iota(jnp.int32, (nbp, nbp), 1)
    eye = row_i == col_i
    cur = qpos // NSA_BLOCK
    avail = jnp.logical_and((lane_blk + 1) * NSA_BLOCK - 1 <= qpos, lane_blk < nb)
    forced = jnp.logical_or(lane_blk == 0, jnp.logical_or(lane_blk == cur, lane_blk == cur - 1))
    for g in range(NSA_KV_GROUPS):
        qg = jnp.concatenate(
            [q[:, (g * NSA_GROUP + r) * HEAD_DIM:(g * NSA_GROUP + r + 1) * HEAD_DIM] for r in range(NSA_GROUP)],
            axis=0).astype(BF16)
        lanes = slice(g * HEAD_DIM, (g + 1) * HEAD_DIM)
        zc = _dot_nt(qg, kc_ref[:, lanes].astype(BF16)) * ATTN_SCALE
        zm = jnp.where(avail, zc, NEG_INF)
        e = jnp.where(avail, jnp.exp(zm - jnp.max(zm, axis=1, keepdims=True)), 0.0)
        den = jnp.sum(e, axis=1, keepdims=True)
        pc = e / jnp.where(den > 0.0, den, 1.0)
        o = _dot(pc.astype(BF16), vc_ref[:, lanes].astype(BF16))
        for r in range(NSA_GROUP):
            h = g * NSA_GROUP + r
            ocmp_ref[:, h * HEAD_DIM:(h + 1) * HEAD_DIM] = o[r:r + 1, :]
        score = jnp.sum(pc, axis=0, keepdims=True)
        s_row = jnp.where(lane_blk <= cur, jnp.where(forced, FORCE_SCORE, score), -1.0)
        s_mat = jnp.broadcast_to(s_row, (nbp, nbp))
        s_col = jnp.sum(jnp.where(eye, s_mat, 0.0), axis=1, keepdims=True)
        ahead = jnp.logical_or(s_mat > s_col, jnp.logical_and(s_mat == s_col, col_i < row_i))
        rank = jnp.sum(jnp.where(ahead, 1.0, 0.0), axis=1, keepdims=True)
        sel_col = jnp.where(jnp.logical_and(rank < float(min(NSA_TOPN, nb)), s_col >= 0.0), 1.0, 0.0)
        sel_row = jnp.sum(jnp.where(eye, jnp.broadcast_to(sel_col, (nbp, nbp)), 0.0), axis=0, keepdims=True)
        slot = jnp.sum(jnp.where(col_i < row_i, jnp.broadcast_to(sel_row, (nbp, nbp)), 0.0), axis=1, keepdims=True)
        k_i = lax.broadcasted_iota(jnp.int32, (nbp, LANES), 1).astype(F32)
        n_i = lax.broadcasted_iota(jnp.int32, (nbp, LANES), 0).astype(F32)
        hit = jnp.logical_and(sel_col > 0.5, slot == k_i)
        idx_ref[g:g + 1, :] = jnp.sum(jnp.where(hit, n_i, 0.0), axis=0, keepdims=True).astype(jnp.int32)


def _nsa_sample_select(q, kc, vc, nb, qpos):
    bd = q.shape[0]
    nbp = kc.shape[1]
    cmp_spec = pl.BlockSpec((None, nbp, NSA_KV), lambda b: (b, 0, 0))
    ocmp, idx = pl.pallas_call(
        functools.partial(_nsa_sample_select_kernel, nb=nb, qpos=qpos),
        grid=(bd,),
        in_specs=[pl.BlockSpec((None, 1, D_MODEL), lambda b: (b, 0, 0)), cmp_spec, cmp_spec],
        out_specs=[pl.BlockSpec((None, 1, D_MODEL), lambda b: (b, 0, 0)),
                   pl.BlockSpec((None, NSA_KV_GROUPS, LANES), lambda b: (b, 0, 0))],
        out_shape=[jax.ShapeDtypeStruct((bd, 1, D_MODEL), F32),
                   jax.ShapeDtypeStruct((bd, NSA_KV_GROUPS, LANES), jnp.int32)],
        compiler_params=_params(("arbitrary",), 32),
        name="nsa_sample_select",
    )(q.reshape(bd, 1, D_MODEL), kc, vc)
    return ocmp, idx[:, :, :NSA_TOPN]


def _nsa_sample_attend_kernel(idx_ref, pt_ref, q_ref, ocmp_ref, gate_ref, kv_ref, kwin_ref, win_ref, cache_ref,
                              o_ref, kbuf, vbuf, sem, *, n_pages, n_past_blocks):
    b = pl.program_id(0)
    g4 = NSA_KV_GROUPS
    bpp = PAGE_SIZE // NSA_BLOCK

    def block_copies(g, k):
        blk = jnp.minimum(idx_ref[(b * g4 + g) * NSA_TOPN + k], n_past_blocks - 1)
        page = pt_ref[b * n_pages + blk // bpp]
        rows = pl.ds((blk % bpp) * NSA_BLOCK, NSA_BLOCK)
        dst = pl.ds(k * NSA_BLOCK, NSA_BLOCK)
        return (pltpu.make_async_copy(cache_ref.at[page, rows, 2], kbuf.at[g % 2, dst], sem.at[0, g % 2, k]),
                pltpu.make_async_copy(cache_ref.at[page, rows, 3], vbuf.at[g % 2, dst], sem.at[1, g % 2, k]))

    def start_group(g):
        for k in range(NSA_TOPN):
            ck, cv = block_copies(g, k)
            ck.start()
            cv.start()

    def wait_group(g):
        for k in range(NSA_TOPN):
            ck, cv = block_copies(g, k)
            ck.wait()
            cv.wait()

    start_group(0)

    q = q_ref[...]
    ocmp = ocmp_ref[...]
    gates = gate_ref[...]
    kv_new = kv_ref[...]
    kwin_new = kwin_ref[...]
    n_sel = NSA_TOPN * NSA_BLOCK
    key_slot = lax.broadcasted_iota(jnp.int32, (1, n_sel), 1) // NSA_BLOCK
    lane = lax.broadcasted_iota(jnp.int32, (NSA_GROUP, LANES), 1)
    sub = lax.broadcasted_iota(jnp.int32, (NSA_GROUP, LANES), 0)
    bf = lambda x: x.astype(BF16).astype(F32)

    def attend(qg, keys, vals, valid, k_new, v_new, new_on):
        z = _dot_nt(qg, keys.astype(BF16)) * ATTN_SCALE
        z_new = jnp.sum(bf(qg) * bf(k_new), axis=1, keepdims=True) * ATTN_SCALE
        zm = z if valid is None else jnp.where(valid, z, NEG_INF)
        zn = jnp.where(new_on, z_new, NEG_INF)
        m = jnp.maximum(jnp.max(zm, axis=1, keepdims=True), zn)
        p = jnp.exp(zm - m) if valid is None else jnp.where(valid, jnp.exp(zm - m), 0.0)
        p_new = jnp.where(new_on, jnp.exp(zn - m), 0.0)
        den = jnp.sum(p, axis=1, keepdims=True) + p_new
        return (_dot(p.astype(BF16), vals.astype(BF16)) + bf(p_new) * bf(v_new)) / den

    for g in range(g4):
        if g + 1 < g4:
            start_group(g + 1)
        wait_group(g)
        qg = jnp.concatenate(
            [q[:, (g * NSA_GROUP + r) * HEAD_DIM:(g * NSA_GROUP + r + 1) * HEAD_DIM] for r in range(NSA_GROUP)],
            axis=0).astype(BF16)
        o_cmp = jnp.concatenate(
            [ocmp[:, (g * NSA_GROUP + r) * HEAD_DIM:(g * NSA_GROUP + r + 1) * HEAD_DIM] for r in range(NSA_GROUP)],
            axis=0)
        valid = jnp.zeros((1, n_sel), jnp.int32)
        has_new = jnp.int32(0)
        for k in range(NSA_TOPN):
            is_new = (idx_ref[(b * g4 + g) * NSA_TOPN + k] >= n_past_blocks).astype(jnp.int32)
            valid = jnp.where(key_slot == k, 1 - is_new, valid)
            has_new = jnp.maximum(has_new, is_new)
        lanes = lambda slot: slice((slot * g4 + g) * HEAD_DIM, (slot * g4 + g + 1) * HEAD_DIM)
        o_sel = attend(qg, kbuf[g % 2, :, g, :], vbuf[g % 2, :, g, :], valid > 0,
                       kv_new[:, lanes(2)], kv_new[:, lanes(3)], has_new > 0)
        o_win = attend(qg, win_ref[:, 0, g, :], win_ref[:, 1, g, :], None,
                       kwin_new[:, lanes(0)], kwin_new[:, lanes(1)], True)
        out = jnp.zeros((NSA_GROUP, HEAD_DIM), F32)
        for c, o_c in enumerate((o_cmp, o_sel, o_win)):
            pick = lane == c * N_HEADS + g * NSA_GROUP + sub
            gate = jnp.sum(jnp.where(pick, jnp.broadcast_to(gates, (NSA_GROUP, LANES)), 0.0), axis=1, keepdims=True)
            out = out + gate * o_c
        for r in range(NSA_GROUP):
            h = g * NSA_GROUP + r
            o_ref[:, h * HEAD_DIM:(h + 1) * HEAD_DIM] = out[r:r + 1, :]


def _nsa_sample_attend(idx, page_table, q, ocmp, gates, kv4, kwin, win_state, cache_kv):
    bd, n_pages = page_table.shape
    g4 = NSA_KV_GROUPS
    sel_buf = pltpu.VMEM((2, NSA_TOPN * NSA_BLOCK, g4, HEAD_DIM), F32)
    row = lambda w: pl.BlockSpec((None, 1, w), lambda b, *_: (b, 0, 0))
    out = pl.pallas_call(
        functools.partial(_nsa_sample_attend_kernel, n_pages=n_pages, n_past_blocks=n_pages * PAGE_SIZE // NSA_BLOCK),
        grid_spec=pltpu.PrefetchScalarGridSpec(
            num_scalar_prefetch=2,
            grid=(bd,),
            in_specs=[row(D_MODEL), row(D_MODEL), row(LANES), row(4 * NSA_KV), row(2 * NSA_KV),
                      pl.BlockSpec((None,) + win_state.shape[1:], lambda b, *_: (b, 0, 0, 0, 0)),
                      pl.BlockSpec(memory_space=pl.ANY)],
            out_specs=row(D_MODEL),
            scratch_shapes=[sel_buf, sel_buf, pltpu.SemaphoreType.DMA((2, 2, NSA_TOPN))]),
        out_shape=jax.ShapeDtypeStruct((bd, 1, D_MODEL), F32),
        compiler_params=_params(("arbitrary",), 40),
        name="nsa_sample_attend",
    )(idx.reshape(-1), page_table.reshape(-1), q.reshape(bd, 1, D_MODEL), ocmp, gates.reshape(bd, 1, LANES),
      kv4.reshape(bd, 1, 4 * NSA_KV), kwin.reshape(bd, 1, 2 * NSA_KV), win_state, cache_kv)
    return out.reshape(bd, D_MODEL)


def _topk_rows(s, k, ids=None):
    if ids is None:
        ids = lax.broadcasted_iota(jnp.int32, s.shape, 0).astype(F32)
    vals, picked = [], []
    for _ in range(k):
        m = jnp.max(s, axis=0, keepdims=True)
        first = jnp.min(jnp.where(s == m, ids, jnp.inf), axis=0, keepdims=True)
        s = jnp.where(ids == first, -jnp.inf, s)
        vals.append(m)
        picked.append(first)
    return jnp.concatenate(vals, axis=0), jnp.concatenate(picked, axis=0)


def _pair_candidates(v1, v2):
    tokens = v1.shape[1]
    row = lambda n: lax.broadcasted_iota(jnp.int32, (n, tokens), 0).astype(F32)
    vals, ids = [], []
    for k1 in range(4):
        n = 16 if k1 == 0 else 8
        k2 = row(n)
        vals.append(jnp.where((k1 + 1) * (k2 + 1.0) <= PEER_TOPK, v1[k1:k1 + 1, :] + v2[0:n, :], -jnp.inf))
        ids.append(k1 * PEER_TOPK + k2)
    for k2 in range(3):
        n = 16 if k2 == 0 else 8
        k1 = row(n)
        keep = jnp.logical_and(k1 >= 4.0, (k1 + 1.0) * (k2 + 1) <= PEER_TOPK)
        vals.append(jnp.where(keep, v1[0:n, :] + v2[k2:k2 + 1, :], -jnp.inf))
        ids.append(k1 * PEER_TOPK + k2)
    return jnp.concatenate(vals, axis=0), jnp.concatenate(ids, axis=0)


def _pick_rows(sel, table):
    out = jnp.zeros_like(sel)
    for k in range(table.shape[0]):
        out = jnp.where(sel == float(k), table[k:k + 1, :], out)
    return out


def _peer_route_kernel(h_ref, wq_ref, k1_ref, k2_ref, i1_ref, i2_ref, g_ref, qh_ref, i1_t, i2_t, g_t):
    qh_ref[...] = _dot(h_ref[...], wq_ref[...])
    k1 = k1_ref[...].astype(BF16)
    k2 = k2_ref[...].astype(BF16)
    half = PEER_NKEYS

    def head(h, carry):
        c0 = pl.multiple_of(h * 2 * half, 2 * half)
        s1 = _dot_nt(k1, qh_ref[:, pl.ds(c0, half)].astype(BF16))
        s2 = _dot_nt(k2, qh_ref[:, pl.ds(c0 + half, half)].astype(BF16))
        v1, i1 = _topk_rows(s1, PEER_TOPK)
        v2, i2 = _topk_rows(s2, PEER_TOPK)
        cand, cand_id = _pair_candidates(v1, v2)
        sc, j = _topk_rows(cand, PEER_TOPK, cand_id)
        ja = jnp.floor(j * (1.0 / PEER_TOPK))
        jb = j - ja * PEER_TOPK
        e = jnp.exp(sc - jnp.max(sc, axis=0, keepdims=True))
        r0 = pl.multiple_of(h * PEER_TOPK, PEER_TOPK)
        i1_t[pl.ds(r0, PEER_TOPK), :] = _pick_rows(ja, i1)
        i2_t[pl.ds(r0, PEER_TOPK), :] = _pick_rows(jb, i2)
        g_t[pl.ds(r0, PEER_TOPK), :] = e / jnp.sum(e, axis=0, keepdims=True)
        return carry

    lax.fori_loop(0, PEER_HEADS, head, 0)
    i1_ref[...] = jnp.transpose(i1_t[...])
    i2_ref[...] = jnp.transpose(i2_t[...])
    g_ref[...] = jnp.transpose(g_t[...])


def _peer_route(h, wq, k1, k2, tm):
    m, d = h.shape
    tm = min(tm, m)
    hk = PEER_HEADS * PEER_TOPK
    out = pl.BlockSpec((tm, hk), lambda i: (i, 0))
    keys = pl.BlockSpec(k1.shape, lambda i: (0, 0))
    return pl.pallas_call(
        _peer_route_kernel,
        grid=(m // tm,),
        in_specs=[pl.BlockSpec((tm, d), lambda i: (i, 0)), pl.BlockSpec(wq.shape, lambda i: (0, 0)), keys, keys],
        out_specs=[out, out, out],
        out_shape=[jax.ShapeDtypeStruct((m, hk), F32)] * 3,
        scratch_shapes=[pltpu.VMEM((tm, wq.shape[1]), F32)] + [pltpu.VMEM((hk, tm), F32)] * 3,
        compiler_params=_params(("arbitrary",), 40),
        name="peer_route",
    )(h, wq, k1, k2)


PEER_TOKENS_PER_TRIP = 8


def _peer_weights_kernel(i1_ref, i2_ref, g_ref, w_ref):
    tm = i1_ref.shape[0]
    hk = i1_ref.shape[1]
    key = lax.broadcasted_iota(jnp.int32, (PEER_NKEYS, hk), 0).astype(F32)

    def body(step, carry):
        for r in range(PEER_TOKENS_PER_TRIP):
            t = step * PEER_TOKENS_PER_TRIP + r
            i1 = jnp.broadcast_to(i1_ref[pl.ds(t, 1), :], (PEER_NKEYS, hk))
            i2 = jnp.broadcast_to(i2_ref[pl.ds(t, 1), :], (PEER_NKEYS, hk))
            gt = jnp.broadcast_to(g_ref[pl.ds(t, 1), :], (PEER_NKEYS, hk))
            p = jnp.where(i1 == key, 1.0, 0.0).astype(BF16)
            q = jnp.where(i2 == key, gt, 0.0).astype(BF16)
            w_ref[t] = _dot_nt(p, q)
        return carry

    lax.fori_loop(0, tm // PEER_TOKENS_PER_TRIP, body, 0)


def _peer_weights(i1, i2, g, tm):
    m, hk = i1.shape
    tm = min(tm, m)
    spec = pl.BlockSpec((tm, hk), lambda i: (i, 0))
    return pl.pallas_call(
        _peer_weights_kernel,
        grid=(m // tm,),
        in_specs=[spec, spec, spec],
        out_specs=pl.BlockSpec((tm, PEER_NKEYS, PEER_NKEYS), lambda i: (i, 0, 0)),
        out_shape=jax.ShapeDtypeStruct((m, PEER_NKEYS, PEER_NKEYS), F32),
        compiler_params=_params(("arbitrary",), 40),
        name="peer_weights",
    )(i1, i2, g)


def _peer_dense_kernel(h_ref, u_ref, v_ref, w_ref, x_ref, gate_ref, lng_ref, lnb_ref, scale_ref, shift_ref,
                       xo_ref, ho_ref, acc_ref):
    c = pl.program_id(1)

    @pl.when(c == 0)
    def _():
        acc_ref[...] = jnp.zeros_like(acc_ref)

    act = _gelu(_dot_nt(h_ref[...], u_ref[...]))
    n_a = act.shape[1] // PEER_NKEYS
    a0 = (c % (w_ref.shape[1] // n_a)) * n_a
    weighted = [act[:, k * PEER_NKEYS:(k + 1) * PEER_NKEYS] * w_ref[:, a0 + k, :] for k in range(n_a)]
    acc_ref[...] += _dot(jnp.concatenate(weighted, axis=1).astype(BF16), v_ref[...])

    @pl.when(c == pl.num_programs(1) - 1)
    def _():
        xn, h = _ln_res(acc_ref[...], x_ref[...], gate_ref[...], lng_ref[...], lnb_ref[...],
                        scale_ref[...], shift_ref[...])
        xo_ref[...] = xn
        ho_ref[...] = h


def _peer_dense(h, u, v, w, xres, gate, lng, lnb, scale, shift, tm, te):
    m, d = h.shape
    n_exp = u.shape[0]
    tm = min(tm, m)
    bpb = (m // tm) // gate.shape[0]
    row = pl.BlockSpec((tm, d), lambda i, c: (i, 0))
    tab = pl.BlockSpec((te, d), lambda i, c: (c, 0))
    vec = pl.BlockSpec((1, d), lambda i, c: (0, 0))
    chunks_per_w = 8 * PEER_NKEYS // te
    wmap = pl.BlockSpec((tm, 8, PEER_NKEYS), lambda i, c: (i, c // chunks_per_w, 0))
    return pl.pallas_call(
        _peer_dense_kernel,
        grid=(m // tm, n_exp // te),
        in_specs=[row, tab, tab, wmap, row,
                  _mod_spec(gate, bpb), vec, vec, _mod_spec(scale, bpb), _mod_spec(shift, bpb)],
        out_specs=[row, row],
        out_shape=[jax.ShapeDtypeStruct((m, d), F32), jax.ShapeDtypeStruct((m, d), BF16)],
        scratch_shapes=[pltpu.VMEM((tm, d), F32)],
        compiler_params=_params(("arbitrary", "arbitrary"), 56),
        name="peer_dense",
    )(h, u, v, w, xres, gate, lng, lnb, scale, shift)


def _peer(h, xres, wq, k1, k2, u, v, gate, lng, lnb, scale, shift, tm_route, tm_dense, te):
    i1, i2, g = _peer_route(h, wq, k1, k2, tm_route)
    w = _peer_weights(i1, i2, g, 128)
    return _peer_dense(h, u, v, w, xres, gate, lng, lnb, scale, shift, tm_dense, te)


def _rope_tables(pos):
    half = ROPE_DIMS // 2
    inv = jnp.power(ROPE_THETA, -jnp.arange(half, dtype=F32) / half)
    ang = pos.astype(F32)[:, None] * inv
    cos, sin = jnp.cos(ang), jnp.sin(ang)
    n = pos.shape[0]
    ones = jnp.ones((n, HEAD_DIM - ROPE_DIMS), F32)
    zeros = jnp.zeros((n, HEAD_DIM - ROPE_DIMS), F32)
    zh = jnp.zeros((n, half), F32)
    return (jnp.concatenate([cos, cos, ones], axis=1),
            jnp.concatenate([-sin, zh, zeros], axis=1),
            jnp.concatenate([zh, sin, zeros], axis=1))


def _pad_rows(x, rows):
    return jnp.pad(x, ((0, rows - x.shape[0]),) + ((0, 0),) * (x.ndim - 1))


def kernel(x_prompt, x_sample, c_prompt, c_sample, cache_l0_kv, cache_l1_kv, state_l1_win, cache_l2_kv, cache_l3_kv,
           state_l3_win, page_table, ada_w, ada_b, ln1_g, ln1_b, ln2_g, ln2_b, sb_w_in, sb_w_o, nsa_w_in, nsa_w_o,
           nsa_cmp_pe, nsa_cmp_w1, nsa_cmp_b1, nsa_cmp_w2, nsa_cmp_b2, peer_wq, peer_k1, peer_k2, peer_u, peer_v):
    batch, seq, d = x_prompt.shape
    bd, dec_seq, _ = x_sample.shape
    assert dec_seq == 1 and d == D_MODEL
    depth = ada_w.shape[0]
    n_pages = page_table.shape[1]
    past = n_pages * PAGE_SIZE
    sb_caches = (cache_l0_kv, cache_l2_kv)
    nsa_caches = (cache_l1_kv, cache_l3_kv)
    nsa_wins = (state_l1_win, state_l3_win)
    win_buf = state_l1_win.shape[1]
    assert win_buf == NSA_WINDOW and past % NSA_BLOCK == 0 and seq >= win_buf
    mp_rows = batch * seq
    tm = 512

    n_c = batch + bd
    c_all = _pad_rows(jnp.concatenate([c_prompt, c_sample], axis=0), -(-n_c // 8) * 8)
    mod = _modulation(c_all, ada_w, ada_b).reshape(depth, c_all.shape[0], 6, d)
    mod_p = lambda i, k: mod[i, :batch, k].reshape(batch, 1, d)
    mod_s = lambda i, k: mod[i, batch:n_c, k].reshape(1, bd, d)

    xp = x_prompt.reshape(mp_rows, d)
    xs = x_sample.reshape(bd, d)
    hp = _modulate(xp, mod_p(0, 1), mod_p(0, 0), tm)
    hs = _modulate(xs, mod_s(0, 1), mod_s(0, 0), bd)

    rope_p = _rope_tables(jnp.arange(seq))
    rope_s = _rope_tables(jnp.full((bd,), past))
    nb_p = -(-seq // NSA_BLOCK)
    assert nb_p * NSA_BLOCK == seq
    nb_s = past // NSA_BLOCK + 1

    new = []
    for i in range(depth):
        j = i // 2
        vec = lambda a: a[i].reshape(1, d)
        if i % 2 == 0:
            w_in = sb_w_in[j].astype(BF16)
            w_o = sb_w_o[j].astype(BF16)
            sb_rows = (2, N_HEADS)
            q_p = _matmul(hp, w_in[:, :d], tm, 512, out_dtype=BF16, name="sb_q")
            kv_p, kv_rows_p = _matmul(hp, w_in[:, d:], tm, 1024, out_dtype=BF16, rows_out=sb_rows, name="sb_kv")
            o_p = _sb_prompt_attention(q_p, kv_p, batch, seq)
            q_s = _matmul(hs, w_in[:, :d], bd, 512, out_dtype=BF16, name="sb_q")
            _, kv_rows_s = _matmul(hs, w_in[:, d:], bd, 1024, out_dtype=BF16, rows_out=sb_rows, name="sb_kv")
            o_s = _sb_sample_attention(q_s, sb_caches[j], page_table)
            new.append((kv_rows_p.reshape(batch, seq, 2, N_HEADS, HEAD_DIM),
                        kv_rows_s.reshape(bd, 1, 2, N_HEADS, HEAD_DIM)))
        else:
            w_in = nsa_w_in[j].astype(BF16)
            w_o = nsa_w_o[j].astype(BF16)
            w_q = w_in[:, :d]
            w_kv4 = w_in[:, d:d + 4 * NSA_KV]
            w_kw = w_in[:, d + 4 * NSA_KV:d + 6 * NSA_KV]
            w_g = jnp.pad(w_in[:, d + 6 * NSA_KV:], ((0, 0), (0, LANES - 3 * N_HEADS)))
            cmp_w = (nsa_cmp_pe[j], nsa_cmp_w1[j].astype(BF16), nsa_cmp_b1[j], nsa_cmp_w2[j], nsa_cmp_b2[j])

            def project(h, rope, rows):
                kv4, kv4_rows = _matmul(h, w_kv4, rows, NSA_KV, epilogue="rope_even", rope=rope,
                                        rows_out=(4, NSA_KV_GROUPS), name="nsa_kv4")
                kwin, kwin_rows = _matmul(h, w_kw, rows, NSA_KV, epilogue="rope_even", rope=rope,
                                          rows_out=(2, NSA_KV_GROUPS), name="nsa_kwin")
                return (_matmul(h, w_q, rows, 512, epilogue="rope_all", rope=rope, name="nsa_q"), kv4, kwin,
                        _matmul(h, w_g, rows, LANES, epilogue="sigmoid", name="nsa_gates"), kv4_rows, kwin_rows)

            q_p, kv4_p, kwin_p, gates_p, kv4_rows_p, kwin_rows_p = project(hp, rope_p, tm)
            cmp_p = _compress(kv4_p.reshape(batch * nb_p, NSA_BLOCK, 4 * NSA_KV), *cmp_w, 256)
            nbp = -(-nb_p // LANES) * LANES
            cmp_p = jnp.pad(cmp_p.reshape(2, batch, nb_p, NSA_KV), ((0, 0), (0, 0), (0, nbp - nb_p), (0, 0)))
            o_p = _nsa_prompt_attention(q_p, kv4_p, kwin_p, gates_p, cmp_p[0], cmp_p[1], batch, seq, nb_p)
            win_p = kwin_rows_p.reshape(batch, seq, 2, NSA_KV_GROUPS, HEAD_DIM)[:, seq - win_buf:]
            q_s, kv4_s, kwin_s, gates_s, kv4_rows_s, kwin_rows_s = project(hs, rope_s, bd)
            x_past = _gather_cmp_pages(nsa_caches[j], page_table)
            cmp_past = _compress(x_past.reshape(bd * (nb_s - 1), NSA_BLOCK, 2 * NSA_KV), *cmp_w, 256)
            x_last = jnp.pad(kv4_s[:, None, :2 * NSA_KV], ((0, 0), (0, NSA_BLOCK - 1), (0, 0)))
            cmp_last = _compress(x_last, *cmp_w, bd)
            nbs = -(-nb_s // LANES) * LANES
            cmp_s = jnp.concatenate([cmp_past.reshape(2, bd, nb_s - 1, NSA_KV), cmp_last.reshape(2, bd, 1, NSA_KV)], 2)
            cmp_s = jnp.pad(cmp_s, ((0, 0), (0, 0), (0, nbs - nb_s), (0, 0)))
            ocmp_s, idx_s = _nsa_sample_select(q_s, cmp_s[0], cmp_s[1], nb_s, past)
            o_s = _nsa_sample_attend(idx_s, page_table, q_s, ocmp_s, gates_s, kv4_s, kwin_s, nsa_wins[j], nsa_caches[j])
            win_s = jnp.concatenate([nsa_wins[j][:, 1:], kwin_rows_s.reshape(bd, 1, 2, NSA_KV_GROUPS, HEAD_DIM)], axis=1)
            new.append((kv4_rows_p.reshape(batch, seq, 4, NSA_KV_GROUPS, HEAD_DIM),
                        kv4_rows_s.reshape(bd, 1, 4, NSA_KV_GROUPS, HEAD_DIM), win_p, win_s))

        xp, hp = _matmul_ln(o_p, w_o, xp, mod_p(i, 2), vec(ln1_g), vec(ln1_b), mod_p(i, 4), mod_p(i, 3), tm)
        xs, hs = _matmul_ln(o_s, w_o, xs, mod_s(i, 2), vec(ln1_g), vec(ln1_b), mod_s(i, 4), mod_s(i, 3), bd)

        nxt = min(i + 1, depth - 1)
        wq = peer_wq[i].astype(BF16)
        u = peer_u[i].astype(BF16)
        v = peer_v[i].astype(BF16)
        xp, hp = _peer(hp, xp, wq, peer_k1[i], peer_k2[i], u, v, mod_p(i, 5), vec(ln2_g), vec(ln2_b),
                       mod_p(nxt, 1), mod_p(nxt, 0), 256, 512, 512)
        pad = LANES
        hs_pad = _pad_rows(hs, pad)
        gate_s = _pad_rows(mod_s(i, 5)[0], pad)[None]
        scale_s = _pad_rows(mod_s(nxt, 1)[0], pad)[None]
        shift_s = _pad_rows(mod_s(nxt, 0)[0], pad)[None]
        xs_pad, hs_pad = _peer(hs_pad, _pad_rows(xs, pad), wq, peer_k1[i], peer_k2[i], u, v, gate_s,
                               vec(ln2_g), vec(ln2_b), scale_s, shift_s, pad, pad, 512)
        xs, hs = xs_pad[:bd], hs_pad[:bd]

    return (xp.reshape(batch, seq, d), xs.reshape(bd, 1, d), new[0][0], new[0][1], new[1][0], new[1][1], new[1][2],
            new[1][3], new[2][0], new[2][1], new[3][0], new[3][1], new[3][2], new[3][3])
```

```python
import functools

import jax
import jax.numpy as jnp
import numpy as np
from jax import lax
from jax.experimental import pallas as pl
from jax.experimental.pallas import tpu as pltpu

F32 = jnp.float32
BF16 = jnp.bfloat16

D_MODEL = 2048
HEAD_DIM = 128
N_HEADS = D_MODEL // HEAD_DIM
PAGE_SIZE = 128
NSA_KV_GROUPS = 4
NSA_GROUP = N_HEADS // NSA_KV_GROUPS
NSA_KV = NSA_KV_GROUPS * HEAD_DIM
NSA_BLOCK = 64
NSA_TOPN = 16
NSA_WINDOW = 512
ROPE_THETA = 500000.0
ROPE_DIMS = HEAD_DIM // 4
PEER_HEADS = 8
PEER_NKEYS = 128
PEER_TOPK = 16
DEPTH = 4
DEEPNORM_ALPHA = (2 * DEPTH) ** 0.25
LN_EPS = 1e-5
NEG_INF = -1e30
FORCE_SCORE = 1e9
ATTN_SCALE = HEAD_DIM ** -0.5
SQRT_HALF = float(np.sqrt(0.5))

LANES = 128
VMEM_LIMIT_CAP = 56 * 1024 * 1024
SB_LOG_UNDERFLOW = -104.0


def _params(sem, vmem_mb):
    return pltpu.CompilerParams(
        dimension_semantics=sem, vmem_limit_bytes=min(vmem_mb * 1024 * 1024, VMEM_LIMIT_CAP))


def _dot(a, b):
    return jnp.dot(a, b, preferred_element_type=F32)


def _dot_nt(a, b):
    return lax.dot_general(a, b, (((1,), (1,)), ((), ())), preferred_element_type=F32)


def _gelu(x):
    return 0.5 * x * (1.0 + lax.erf(x * SQRT_HALF))


def _log_sigmoid_pair(z):
    l1p = jnp.log(1.0 + jnp.exp(-jnp.abs(z)))
    return jnp.minimum(z, 0.0) - l1p, jnp.minimum(-z, 0.0) - l1p


def _ln_res(y, xres, gate, lng, lnb, scale, shift):
    v = DEEPNORM_ALPHA * xres + gate * y
    mu = jnp.mean(v, axis=-1, keepdims=True)
    d = v - mu
    var = jnp.mean(d * d, axis=-1, keepdims=True)
    xn = d * lax.rsqrt(var + LN_EPS) * lng + lnb
    return xn, (xn * (1.0 + scale) + shift).astype(BF16)


def _mod_spec(arr, blocks_per_batch):
    return pl.BlockSpec((None,) + arr.shape[1:], lambda i, *_: (i // blocks_per_batch, 0, 0))


def _modulation_kernel(c_ref, w_ref, b_ref, o_ref):
    c = c_ref[...]
    s = (c * jax.nn.sigmoid(c)).astype(BF16)
    o_ref[...] = _dot(s, w_ref[...].astype(BF16)) + b_ref[...]


def _modulation(c_all, ada_w, ada_b):
    depth, d, n = ada_w.shape
    r = c_all.shape[0]
    tn = 1024
    return pl.pallas_call(
        _modulation_kernel,
        grid=(depth, n // tn),
        in_specs=[
            pl.BlockSpec((r, d), lambda l, j: (0, 0)),
            pl.BlockSpec((None, d, tn), lambda l, j: (l, 0, j)),
            pl.BlockSpec((None, 1, tn), lambda l, j: (l, 0, j)),
        ],
        out_specs=pl.BlockSpec((None, r, tn), lambda l, j: (l, 0, j)),
        out_shape=jax.ShapeDtypeStruct((depth, r, n), F32),
        compiler_params=_params(("arbitrary", "arbitrary"), 32),
        name="modulation",
    )(c_all, ada_w, ada_b.reshape(depth, 1, n))


def _modulate_kernel(x_ref, scale_ref, shift_ref, o_ref):
    o_ref[...] = (x_ref[...] * (1.0 + scale_ref[...]) + shift_ref[...]).astype(BF16)


def _modulate(x, scale, shift, tm):
    m, d = x.shape
    bpb = (m // tm) // scale.shape[0]
    return pl.pallas_call(
        _modulate_kernel,
        grid=(m // tm,),
        in_specs=[pl.BlockSpec((tm, d), lambda i: (i, 0)), _mod_spec(scale, bpb), _mod_spec(shift, bpb)],
        out_specs=pl.BlockSpec((tm, d), lambda i: (i, 0)),
        out_shape=jax.ShapeDtypeStruct((m, d), BF16),
        compiler_params=_params(("arbitrary",), 32),
        name="modulate",
    )(x, scale, shift)


def _store_tile(y, o_refs):
    o_refs[0][...] = y.astype(o_refs[0].dtype)
    if len(o_refs) > 1:
        for h in range(o_refs[1].shape[1]):
            o_refs[1][:, h, :] = y[:, h * HEAD_DIM:(h + 1) * HEAD_DIM]


def _mm_kernel(a_ref, w_ref, *o_refs):
    _store_tile(_dot(a_ref[...], w_ref[...]), o_refs)


def _mm_sigmoid_kernel(a_ref, w_ref, o_ref):
    o_ref[...] = jax.nn.sigmoid(_dot(a_ref[...], w_ref[...]))


def _mm_rope_kernel(a_ref, w_ref, cos_ref, sa_ref, sb_ref, *o_refs, even_only):
    acc = _dot(a_ref[...], w_ref[...])
    tn = acc.shape[1]
    reps = tn // HEAD_DIM
    cos = jnp.concatenate([cos_ref[...]] * reps, axis=1)
    sa = jnp.concatenate([sa_ref[...]] * reps, axis=1)
    sb = jnp.concatenate([sb_ref[...]] * reps, axis=1)
    half = ROPE_DIMS // 2
    rot = acc * cos + pltpu.roll(acc, tn - half, 1) * sa + pltpu.roll(acc, half, 1) * sb
    if even_only:
        rot = jnp.where(pl.program_id(1) % 2 == 0, rot, acc)
    _store_tile(rot, o_refs)


def _matmul(a, w, tm, tn, *, epilogue=None, rope=None, out_dtype=F32, rows_out=None, name="matmul"):
    m, k = a.shape
    n = w.shape[1]
    tm = min(tm, m)
    in_specs = [pl.BlockSpec((tm, k), lambda i, j: (i, 0)), pl.BlockSpec((k, tn), lambda i, j: (0, j))]
    args = [a, w]
    if epilogue in ("rope_all", "rope_even"):
        assert epilogue == "rope_all" or tn == NSA_KV
        t_tab = rope[0].shape[0]
        nt = t_tab // tm
        in_specs += [pl.BlockSpec((tm, HEAD_DIM), lambda i, j: (i % nt, 0))] * 3
        args += list(rope)
        kern = functools.partial(_mm_rope_kernel, even_only=(epilogue == "rope_even"))
    elif epilogue == "sigmoid":
        kern = _mm_sigmoid_kernel
    else:
        kern = _mm_kernel
    out_specs = pl.BlockSpec((tm, tn), lambda i, j: (i, j))
    out_shape = jax.ShapeDtypeStruct((m, n), out_dtype)
    if rows_out is not None:
        slots, heads = rows_out
        tiles_per_slot = heads * HEAD_DIM // tn
        out_specs = [out_specs, pl.BlockSpec((tm, None, heads // tiles_per_slot, HEAD_DIM),
                                             lambda i, j: (i, j // tiles_per_slot, j % tiles_per_slot, 0))]
        out_shape = [out_shape, jax.ShapeDtypeStruct((m, slots, heads, HEAD_DIM), F32)]
    return pl.pallas_call(
        kern,
        grid=(m // tm, n // tn),
        in_specs=in_specs,
        out_specs=out_specs,
        out_shape=out_shape,
        compiler_params=_params(("arbitrary", "arbitrary"), 40),
        name=name,
    )(*args)


def _mm_ln_kernel(a_ref, w_ref, x_ref, gate_ref, lng_ref, lnb_ref, scale_ref, shift_ref, xo_ref, ho_ref):
    y = _dot(a_ref[...].astype(BF16), w_ref[...])
    xn, h = _ln_res(y, x_ref[...], gate_ref[...], lng_ref[...], lnb_ref[...], scale_ref[...], shift_ref[...])
    xo_ref[...] = xn
    ho_ref[...] = h


def _matmul_ln(a, w, xres, gate, lng, lnb, scale, shift, tm):
    m, k = a.shape
    d = w.shape[1]
    tm = min(tm, m)
    bpb = (m // tm) // gate.shape[0]
    row = pl.BlockSpec((tm, d), lambda i: (i, 0))
    vec = pl.BlockSpec((1, d), lambda i: (0, 0))
    return pl.pallas_call(
        _mm_ln_kernel,
        grid=(m // tm,),
        in_specs=[pl.BlockSpec((tm, k), lambda i: (i, 0)), pl.BlockSpec((k, d), lambda i: (0, 0)), row,
                  _mod_spec(gate, bpb), vec, vec, _mod_spec(scale, bpb), _mod_spec(shift, bpb)],
        out_specs=[row, row],
        out_shape=[jax.ShapeDtypeStruct((m, d), F32), jax.ShapeDtypeStruct((m, d), BF16)],
        compiler_params=_params(("arbitrary",), 48),
        name="matmul_ln",
    )(a, w, xres, gate, lng, lnb, scale, shift)


def _suffix_matrix(n):
    return (lax.broadcasted_iota(jnp.int32, (n, n), 0) > lax.broadcasted_iota(jnp.int32, (n, n), 1)).astype(BF16)


def _suffix_sum(x, lmat):
    hi = x.astype(BF16)
    lo = (x - hi.astype(F32)).astype(BF16)
    return _dot(hi, lmat) + _dot(lo, lmat)


SB_HEADS_PER_STEP = 8


def _sb_prompt_kernel(q_ref, k_ref, v_ref, o_ref, carry_ref, acc_ref, *, tq, tk):
    i = pl.program_id(2)
    carry_ref[...] = jnp.zeros_like(carry_ref)
    acc_ref[...] = jnp.zeros_like(acc_ref)
    qpos = i * tq + lax.broadcasted_iota(jnp.int32, (tq, 1), 0)
    lmat = jnp.concatenate([_suffix_matrix(tk), jnp.ones((tk, tk), BF16)], axis=1)

    def body(state):
        j, _ = state
        k0 = pl.multiple_of(j * tk, tk)
        allowed = (k0 + lax.broadcasted_iota(jnp.int32, (1, tk), 1)) < qpos
        heads = range(SB_HEADS_PER_STEP)
        lanes = [slice(hh * HEAD_DIM, (hh + 1) * HEAD_DIM) for hh in heads]
        zs = [_dot_nt(q_ref[:, lanes[hh]].astype(BF16), k_ref[pl.ds(k0, tk), lanes[hh]].astype(BF16)) * ATTN_SCALE
              for hh in heads]
        pairs = [_log_sigmoid_pair(z) for z in zs]
        sums = [_suffix_sum(jnp.where(allowed, lsn, 0.0), lmat) for _, lsn in pairs]
        carries = [carry_ref[hh] for hh in heads]
        probs = [jnp.where(allowed, jnp.exp(pairs[hh][0] + sums[hh][:, :tk] + carries[hh]), 0.0).astype(BF16)
                 for hh in heads]
        pvs = [_dot(probs[hh], v_ref[pl.ds(k0, tk), lanes[hh]].astype(BF16)) for hh in heads]
        top = jnp.full((tq, tk), -jnp.inf, F32)
        for hh in heads:
            acc_ref[:, lanes[hh]] += pvs[hh]
            carry = carries[hh] + sums[hh][:, tk:]
            carry_ref[hh] = carry
            top = jnp.maximum(top, carry)
        return j - 1, (jnp.max(top) > SB_LOG_UNDERFLOW).astype(jnp.int32)

    lax.while_loop(lambda s: jnp.logical_and(s[0] >= 0, s[1] > 0), body, ((i * tq + tq - 1) // tk, jnp.int32(1)))
    o_ref[...] = acc_ref[...]


def _sb_prompt_attention(q, kv, batch, seq, tq=256, tk=128):
    nq = seq // tq
    hb = SB_HEADS_PER_STEP
    width = hb * HEAD_DIM
    n_hb = N_HEADS // hb
    return pl.pallas_call(
        functools.partial(_sb_prompt_kernel, tq=tq, tk=tk),
        grid=(batch, n_hb, nq),
        in_specs=[
            pl.BlockSpec((tq, width), lambda b, h, i: (b * nq + i, h)),
            pl.BlockSpec((seq, width), lambda b, h, i: (b, h)),
            pl.BlockSpec((seq, width), lambda b, h, i: (b, n_hb + h)),
        ],
        out_specs=pl.BlockSpec((tq, width), lambda b, h, i: (b * nq + i, h)),
        out_shape=jax.ShapeDtypeStruct(q.shape, F32),
        scratch_shapes=[pltpu.VMEM((hb, tq, tk), F32), pltpu.VMEM((tq, width), F32)],
        compiler_params=_params(("arbitrary", "arbitrary", "arbitrary"), 48),
        name="sb_prompt_attention",
    )(q, kv, kv)


SB_ROW_REP = 8


def _sb_sample_kernel(pt_ref, q_ref, cache_ref, o_ref, kbuf, vbuf, sem, carry_ref, acc_ref, *, n_pages):
    b = pl.program_id(0)
    rep = SB_ROW_REP
    q = q_ref[...]
    qh = [jnp.broadcast_to(q[:, h * HEAD_DIM:(h + 1) * HEAD_DIM], (rep, HEAD_DIM)).astype(BF16)
          for h in range(N_HEADS)]
    lmat = _suffix_matrix(PAGE_SIZE)
    carry_ref[...] = jnp.zeros_like(carry_ref)
    acc_ref[...] = jnp.zeros_like(acc_ref)

    def page_copies(page):
        return (pltpu.make_async_copy(cache_ref.at[page, :, 0], kbuf, sem.at[0]),
                pltpu.make_async_copy(cache_ref.at[page, :, 1], vbuf, sem.at[1]))

    def body(state):
        p, _ = state
        ck, cv = page_copies(pt_ref[b * n_pages + p])
        ck.start()
        cv.start()
        ck.wait()
        cv.wait()
        z = jnp.concatenate([_dot_nt(qh[h], kbuf[:, h, :].astype(BF16)) for h in range(N_HEADS)], axis=0)
        ls, lsn = _log_sigmoid_pair(z * ATTN_SCALE)
        later = _suffix_sum(lsn, lmat) + carry_ref[...]
        a = jnp.exp(ls + later)
        pv = [_dot(a[h * rep:(h + 1) * rep, :].astype(BF16), vbuf[:, h, :].astype(BF16)) for h in range(N_HEADS)]
        acc_ref[...] += jnp.concatenate(pv, axis=0)
        carry = carry_ref[...] + jnp.sum(lsn, axis=1, keepdims=True)
        carry_ref[...] = carry
        return p - 1, (jnp.max(carry) > SB_LOG_UNDERFLOW).astype(jnp.int32)

    lax.while_loop(lambda s: jnp.logical_and(s[0] >= 0, s[1] > 0), body, (jnp.int32(n_pages - 1), jnp.int32(1)))
    for h in range(N_HEADS):
        o_ref[:, h * HEAD_DIM:(h + 1) * HEAD_DIM] = acc_ref[h * rep:h * rep + 1, :]


def _sb_sample_attention(q, cache_kv, page_table):
    bd, n_pages = page_table.shape
    page_buf = pltpu.VMEM((PAGE_SIZE, N_HEADS, HEAD_DIM), F32)
    out = pl.pallas_call(
        functools.partial(_sb_sample_kernel, n_pages=n_pages),
        grid_spec=pltpu.PrefetchScalarGridSpec(
            num_scalar_prefetch=1,
            grid=(bd,),
            in_specs=[pl.BlockSpec((None, 1, D_MODEL), lambda b, pt: (b, 0, 0)),
                      pl.BlockSpec(memory_space=pl.ANY)],
            out_specs=pl.BlockSpec((None, 1, D_MODEL), lambda b, pt: (b, 0, 0)),
            scratch_shapes=[page_buf, page_buf, pltpu.SemaphoreType.DMA((2,)),
                            pltpu.VMEM((N_HEADS * SB_ROW_REP, 1), F32),
                            pltpu.VMEM((N_HEADS * SB_ROW_REP, HEAD_DIM), F32)]),
        out_shape=jax.ShapeDtypeStruct((bd, 1, D_MODEL), F32),
        compiler_params=_params(("arbitrary",), 16),
        name="sb_sample_attention",
    )(page_table.reshape(-1), q.reshape(bd, 1, D_MODEL), cache_kv)
    return out.reshape(bd, D_MODEL)


def _compress_kernel(x_hbm, pe_ref, w1_ref, b1_ref, w2_ref, b2_ref, o_ref, xbuf, sem, *, tr):
    n_i = pl.num_programs(2)
    step = (pl.program_id(0) * pl.num_programs(1) + pl.program_id(1)) * n_i + pl.program_id(2)
    total = pl.num_programs(0) * pl.num_programs(1) * n_i

    def block_copies(s):
        rows = pl.ds((s % n_i) * tr, tr)
        lanes = pl.ds((s // n_i) * HEAD_DIM, HEAD_DIM)
        return [pltpu.make_async_copy(x_hbm.at[rows, t, lanes], xbuf.at[s % 2, t], sem.at[s % 2, t])
                for t in range(NSA_BLOCK)]

    @pl.when(step == 0)
    def _():
        for cp in block_copies(step):
            cp.start()

    @pl.when(step + 1 < total)
    def _():
        for cp in block_copies(step + 1):
            cp.start()

    for cp in block_copies(step):
        cp.wait()
    acc = jnp.zeros((tr, w1_ref.shape[1]), F32)
    for t in range(NSA_BLOCK):
        xt = (xbuf[step % 2, t] + pe_ref[t:t + 1, :]).astype(BF16)
        acc = acc + _dot(xt, w1_ref[t * HEAD_DIM:(t + 1) * HEAD_DIM, :])
    hid = _gelu(acc + b1_ref[...])
    o_ref[...] = _dot(hid.astype(BF16), w2_ref[...].astype(BF16)) + b2_ref[...]


def _compress(x, pe, w1, b1, w2, b2, tr):
    r = x.shape[0]
    tr = min(tr, r)
    hidden = w1.shape[2]
    return pl.pallas_call(
        functools.partial(_compress_kernel, tr=tr),
        grid=(2, NSA_KV_GROUPS, r // tr),
        in_specs=[
            pl.BlockSpec(memory_space=pl.ANY),
            pl.BlockSpec((None, NSA_BLOCK, HEAD_DIM), lambda s, g, i: (s, 0, 0)),
            pl.BlockSpec((None, NSA_BLOCK * HEAD_DIM, hidden), lambda s, g, i: (s, 0, 0)),
            pl.BlockSpec((None, 1, hidden), lambda s, g, i: (s, 0, 0)),
            pl.BlockSpec((None, hidden, HEAD_DIM), lambda s, g, i: (s, 0, 0)),
            pl.BlockSpec((None, 1, HEAD_DIM), lambda s, g, i: (s, 0, 0)),
        ],
        out_specs=pl.BlockSpec((None, tr, HEAD_DIM), lambda s, g, i: (s, i, g)),
        out_shape=jax.ShapeDtypeStruct((2, r, NSA_KV), F32),
        scratch_shapes=[pltpu.VMEM((2, NSA_BLOCK, tr, HEAD_DIM), F32), pltpu.SemaphoreType.DMA((2, NSA_BLOCK))],
        compiler_params=_params(("arbitrary", "arbitrary", "arbitrary"), 48),
        name="nsa_compress",
    )(x, pe, w1, b1.reshape(2, 1, hidden), w2, b2.reshape(2, 1, HEAD_DIM))


def _page_gather_kernel(pt_ref, *refs):
    o_ref = refs[-1]
    for k, r in enumerate(refs[:-1]):
        for s in range(2):
            for g in range(NSA_KV_GROUPS):
                lanes = slice((s * NSA_KV_GROUPS + g) * HEAD_DIM, (s * NSA_KV_GROUPS + g + 1) * HEAD_DIM)
                o_ref[k * PAGE_SIZE:(k + 1) * PAGE_SIZE, lanes] = r[:, s, g, :]


def _gather_cmp_pages(cache_kv, page_table, pages_per_step=8):
    bd, n_pages = page_table.shape
    width = 2 * NSA_KV
    pps = pages_per_step

    def in_map(k):
        return lambda b, c, pt: (pt[b * n_pages + c * pps + k], 0, 0, 0, 0)

    return pl.pallas_call(
        _page_gather_kernel,
        grid_spec=pltpu.PrefetchScalarGridSpec(
            num_scalar_prefetch=1,
            grid=(bd, n_pages // pps),
            in_specs=[pl.BlockSpec((None, PAGE_SIZE, 2, NSA_KV_GROUPS, HEAD_DIM), in_map(k)) for k in range(pps)],
            out_specs=pl.BlockSpec((None, pps * PAGE_SIZE, width), lambda b, c, pt: (b, c, 0))),
        out_shape=jax.ShapeDtypeStruct((bd, n_pages * PAGE_SIZE, width), F32),
        compiler_params=_params(("arbitrary", "arbitrary"), 40),
        name="nsa_page_gather",
    )(page_table.reshape(-1), *([cache_kv] * pps))


def _online_softmax_step(z, v_t, m_ref, l_ref, acc_ref):
    m_new = jnp.maximum(m_ref[...], jnp.max(z, axis=0, keepdims=True))
    alpha = jnp.exp(m_ref[...] - m_new)
    p = jnp.exp(z - m_new)
    l_ref[...] = alpha * l_ref[...] + jnp.sum(p, axis=0, keepdims=True)
    acc_ref[...] = alpha * acc_ref[...] + _dot(v_t, p.astype(BF16))
    m_ref[...] = m_new


def _nsa_prompt_kernel(q_ref, kc_ref, vc_ref, ks_ref, vs_ref, kw_ref, vw_ref, gate_ref, o_ref,
                       m_ref, l_ref, acc_ref, osel_ref, gate_t_ref, *, nb, tq, tk):
    g = pl.program_id(1)
    i = pl.program_id(2)
    nq = NSA_GROUP * tq
    nbp = kc_ref.shape[0]
    qblk = q_ref[...]
    qg = jnp.concatenate([qblk[:, r * HEAD_DIM:(r + 1) * HEAD_DIM] for r in range(NSA_GROUP)], axis=0).astype(BF16)
    qpos = i * tq + lax.broadcasted_iota(jnp.int32, (1, tq), 1)
    tile4 = lambda x: jnp.concatenate([x] * NSA_GROUP, axis=1)

    blk = lax.broadcasted_iota(jnp.int32, (nbp, 1), 0)
    zc = _dot_nt(kc_ref[...].astype(BF16), qg) * ATTN_SCALE
    avail = tile4(jnp.logical_and((blk + 1) * NSA_BLOCK - 1 <= qpos, blk < nb))
    zm = jnp.where(avail, zc, NEG_INF)
    e = jnp.where(avail, jnp.exp(zm - jnp.max(zm, axis=0, keepdims=True)), 0.0)
    den = jnp.sum(e, axis=0, keepdims=True)
    pc = e / jnp.where(den > 0.0, den, 1.0)
    o_cmp = _dot(jnp.transpose(vc_ref[...]).astype(BF16), pc.astype(BF16))

    score = pc[:, 0:tq]
    for r in range(1, NSA_GROUP):
        score = score + pc[:, r * tq:(r + 1) * tq]
    nbr = -(-nb // 8) * 8
    blk = blk[0:nbr]
    cur = qpos // NSA_BLOCK
    forced = jnp.logical_or(blk == 0, jnp.logical_or(blk == cur, blk == cur - 1))
    s = jnp.where(blk <= cur, jnp.where(forced, FORCE_SCORE, score[0:nbr]), -1.0)
    rank = jnp.zeros((nbr, tq), F32)
    for mth in range(nb):
        sm = s[mth:mth + 1, :]
        ahead = jnp.logical_or(sm > s, jnp.logical_and(sm == s, blk > mth))
        rank = rank + jnp.where(ahead, 1.0, 0.0)
    sel = jnp.where(jnp.logical_and(rank < float(min(NSA_TOPN, nb)), s >= 0.0), 1.0, 0.0).astype(BF16)

    def reset():
        m_ref[...] = jnp.full_like(m_ref, NEG_INF)
        l_ref[...] = jnp.zeros_like(l_ref)
        acc_ref[...] = jnp.zeros_like(acc_ref)

    reset()
    bpt = tk // NSA_BLOCK

    def sel_body(kt, carry):
        k0 = pl.multiple_of(kt * tk, tk)
        z = _dot_nt(ks_ref[pl.ds(k0, tk), :].astype(BF16), qg) * ATTN_SCALE
        key = lax.broadcasted_iota(jnp.int32, (tk, nbr), 0)
        col = lax.broadcasted_iota(jnp.int32, (tk, nbr), 1)
        expand = (col == kt * bpt + key // NSA_BLOCK).astype(BF16)
        chosen = _dot(expand, sel) > 0.5
        kpos = k0 + lax.broadcasted_iota(jnp.int32, (tk, 1), 0)
        bias = jnp.where(jnp.logical_and(chosen, kpos <= qpos), 0.0, NEG_INF)
        v_t = jnp.transpose(vs_ref[pl.ds(k0, tk), :]).astype(BF16)
        _online_softmax_step(z + tile4(bias), v_t, m_ref, l_ref, acc_ref)
        return carry

    lax.fori_loop(0, (i * tq + tq - 1) // tk + 1, sel_body, 0)
    osel_ref[...] = acc_ref[...] / l_ref[...]

    reset()

    def win_body(kt, carry):
        k0 = pl.multiple_of(kt * tq, tq)
        z = _dot_nt(kw_ref[pl.ds(k0, tq), :].astype(BF16), qg) * ATTN_SCALE
        dist = qpos - (k0 + lax.broadcasted_iota(jnp.int32, (tq, 1), 0))
        bias = jnp.where(jnp.logical_and(dist >= 0, dist <= NSA_WINDOW), 0.0, NEG_INF)
        v_t = jnp.transpose(vw_ref[pl.ds(k0, tq), :]).astype(BF16)
        _online_softmax_step(z + tile4(bias), v_t, m_ref, l_ref, acc_ref)
        return carry

    lax.fori_loop(jnp.maximum(i - NSA_WINDOW // tq, 0), i + 1, win_body, 0)
    o_win = acc_ref[...] / l_ref[...]
    o_sel = osel_ref[...]

    gate_t_ref[...] = jnp.transpose(gate_ref[...])
    for r in range(NSA_GROUP):
        lanes = slice(r * tq, (r + 1) * tq)
        out_t = jnp.zeros((HEAD_DIM, tq), F32)
        for c, o_c in enumerate((o_cmp, o_sel, o_win)):
            gate = gate_t_ref[pl.ds(c * N_HEADS + g * NSA_GROUP + r, 1), :]
            out_t = out_t + gate * o_c[:, lanes]
        o_ref[:, r * HEAD_DIM:(r + 1) * HEAD_DIM] = jnp.transpose(out_t)


def _nsa_prompt_attention(q, kv4, kwin, gates, kc, vc, batch, seq, nb):
    tq = 256
    tk = 512
    nq = seq // tq
    g4 = NSA_KV_GROUPS
    col = lambda off: pl.BlockSpec((seq, HEAD_DIM), lambda b, g, i: (b, off + g))
    cmp_spec = pl.BlockSpec((None, kc.shape[1], HEAD_DIM), lambda b, g, i: (b, 0, g))
    return pl.pallas_call(
        functools.partial(_nsa_prompt_kernel, nb=nb, tq=tq, tk=tk),
        grid=(batch, g4, nq),
        in_specs=[
            pl.BlockSpec((tq, NSA_KV), lambda b, g, i: (b * nq + i, g)),
            cmp_spec, cmp_spec,
            col(2 * g4), col(3 * g4),
            col(0), col(g4),
            pl.BlockSpec((tq, LANES), lambda b, g, i: (b * nq + i, 0)),
        ],
        out_specs=pl.BlockSpec((tq, NSA_KV), lambda b, g, i: (b * nq + i, g)),
        out_shape=jax.ShapeDtypeStruct(q.shape, F32),
        scratch_shapes=[pltpu.VMEM((1, NSA_GROUP * tq), F32), pltpu.VMEM((1, NSA_GROUP * tq), F32),
                        pltpu.VMEM((HEAD_DIM, NSA_GROUP * tq), F32), pltpu.VMEM((HEAD_DIM, NSA_GROUP * tq), F32),
                        pltpu.VMEM((LANES, tq), F32)],
        compiler_params=_params(("arbitrary", "arbitrary", "arbitrary"), 40),
        name="nsa_prompt_attention",
    )(q, kc, vc, kv4, kv4, kwin, kwin, gates)


def _nsa_sample_select_kernel(q_ref, kc_ref, vc_ref, ocmp_ref, idx_ref, *, nb, qpos):
    nbp = kc_ref.shape[0]
    q = q_ref[...]
    lane_blk = lax.broadcasted_iota(jnp.int32, (1, nbp), 1)
    row_i = lax.broadcasted_iota(jnp.int32, (nbp, nbp), 0)
    col_i = lax.broadcasted_iota(jnp.int32, (nbp, nbp), 1)
    eye = row_i == col_i
    cur = qpos // NSA_BLOCK
    avail = jnp.logical_and((lane_blk + 1) * NSA_BLOCK - 1 <= qpos, lane_blk < nb)
    forced = jnp.logical_or(lane_blk == 0, jnp.logical_or(lane_blk == cur, lane_blk == cur - 1))
    for g in range(NSA_KV_GROUPS):
        qg = jnp.concatenate(
            [q[:, (g * NSA_GROUP + r) * HEAD_DIM:(g * NSA_GROUP + r + 1) * HEAD_DIM] for r in range(NSA_GROUP)],
            axis=0).astype(BF16)
        lanes = slice(g * HEAD_DIM, (g + 1) * HEAD_DIM)
        zc = _dot_nt(qg, kc_ref[:, lanes].astype(BF16)) * ATTN_SCALE
        zm = jnp.where(avail, zc, NEG_INF)
        e = jnp.where(avail, jnp.exp(zm - jnp.max(zm, axis=1, keepdims=True)), 0.0)
        den = jnp.sum(e, axis=1, keepdims=True)
        pc = e / jnp.where(den > 0.0, den, 1.0)
        o = _dot(pc.astype(BF16), vc_ref[:, lanes].astype(BF16))
        for r in range(NSA_GROUP):
            h = g * NSA_GROUP + r
            ocmp_ref[:, h * HEAD_DIM:(h + 1) * HEAD_DIM] = o[r:r + 1, :]
        score = jnp.sum(pc, axis=0, keepdims=True)
        s_row = jnp.where(lane_blk <= cur, jnp.where(forced, FORCE_SCORE, score), -1.0)
        s_mat = jnp.broadcast_to(s_row, (nbp, nbp))
        s_col = jnp.sum(jnp.where(eye, s_mat, 0.0), axis=1, keepdims=True)
        ahead = jnp.logical_or(s_mat > s_col, jnp.logical_and(s_mat == s_col, col_i < row_i))
        rank = jnp.sum(jnp.where(ahead, 1.0, 0.0), axis=1, keepdims=True)
        sel_col = jnp.where(jnp.logical_and(rank < float(min(NSA_TOPN, nb)), s_col >= 0.0), 1.0, 0.0)
        sel_row = jnp.sum(jnp.where(eye, jnp.broadcast_to(sel_col, (nbp, nbp)), 0.0), axis=0, keepdims=True)
        slot = jnp.sum(jnp.where(col_i < row_i, jnp.broadcast_to(sel_row, (nbp, nbp)), 0.0), axis=1, keepdims=True)
        k_i = lax.broadcasted_iota(jnp.int32, (nbp, LANES), 1).astype(F32)
        n_i = lax.broadcasted_iota(jnp.int32, (nbp, LANES), 0).astype(F32)
        hit = jnp.logical_and(sel_col > 0.5, slot == k_i)
        idx_ref[g:g + 1, :] = jnp.sum(jnp.where(hit, n_i, 0.0), axis=0, keepdims=True).astype(jnp.int32)


def _nsa_sample_select(q, kc, vc, nb, qpos):
    bd = q.shape[0]
    nbp = kc.shape[1]
    cmp_spec = pl.BlockSpec((None, nbp, NSA_KV), lambda b: (b, 0, 0))
    ocmp, idx = pl.pallas_call(
        functools.partial(_nsa_sample_select_kernel, nb=nb, qpos=qpos),
        grid=(bd,),
        in_specs=[pl.BlockSpec((None, 1, D_MODEL), lambda b: (b, 0, 0)), cmp_spec, cmp_spec],
        out_specs=[pl.BlockSpec((None, 1, D_MODEL), lambda b: (b, 0, 0)),
                   pl.BlockSpec((None, NSA_KV_GROUPS, LANES), lambda b: (b, 0, 0))],
        out_shape=[jax.ShapeDtypeStruct((bd, 1, D_MODEL), F32),
                   jax.ShapeDtypeStruct((bd, NSA_KV_GROUPS, LANES), jnp.int32)],
        compiler_params=_params(("arbitrary",), 32),
        name="nsa_sample_select",
    )(q.reshape(bd, 1, D_MODEL), kc, vc)
    return ocmp, idx[:, :, :NSA_TOPN]


def _nsa_sample_attend_kernel(idx_ref, pt_ref, q_ref, ocmp_ref, gate_ref, kv_ref, kwin_ref, win_ref, cache_ref,
                              o_ref, kbuf, vbuf, sem, *, n_pages, n_past_blocks):
    b = pl.program_id(0)
    g4 = NSA_KV_GROUPS
    bpp = PAGE_SIZE // NSA_BLOCK

    def block_copies(g, k):
        blk = jnp.minimum(idx_ref[(b * g4 + g) * NSA_TOPN + k], n_past_blocks - 1)
        page = pt_ref[b * n_pages + blk // bpp]
        rows = pl.ds((blk % bpp) * NSA_BLOCK, NSA_BLOCK)
        dst = pl.ds(k * NSA_BLOCK, NSA_BLOCK)
        return (pltpu.make_async_copy(cache_ref.at[page, rows, 2], kbuf.at[g % 2, dst], sem.at[0, g % 2, k]),
                pltpu.make_async_copy(cache_ref.at[page, rows, 3], vbuf.at[g % 2, dst], sem.at[1, g % 2, k]))

    def start_group(g):
        for k in range(NSA_TOPN):
            ck, cv = block_copies(g, k)
            ck.start()
            cv.start()

    def wait_group(g):
        for k in range(NSA_TOPN):
            ck, cv = block_copies(g, k)
            ck.wait()
            cv.wait()

    start_group(0)

    q = q_ref[...]
    ocmp = ocmp_ref[...]
    gates = gate_ref[...]
    kv_new = kv_ref[...]
    kwin_new = kwin_ref[...]
    n_sel = NSA_TOPN * NSA_BLOCK
    key_slot = lax.broadcasted_iota(jnp.int32, (1, n_sel), 1) // NSA_BLOCK
    lane = lax.broadcasted_iota(jnp.int32, (NSA_GROUP, LANES), 1)
    sub = lax.broadcasted_iota(jnp.int32, (NSA_GROUP, LANES), 0)
    bf = lambda x: x.astype(BF16).astype(F32)

    def attend(qg, keys, vals, valid, k_new, v_new, new_on):
        z = _dot_nt(qg, keys.astype(BF16)) * ATTN_SCALE
        z_new = jnp.sum(bf(qg) * bf(k_new), axis=1, keepdims=True) * ATTN_SCALE
        zm = z if valid is None else jnp.where(valid, z, NEG_INF)
        zn = jnp.where(new_on, z_new, NEG_INF)
        m = jnp.maximum(jnp.max(zm, axis=1, keepdims=True), zn)
        p = jnp.exp(zm - m) if valid is None else jnp.where(valid, jnp.exp(zm - m), 0.0)
        p_new = jnp.where(new_on, jnp.exp(zn - m), 0.0)
        den = jnp.sum(p, axis=1, keepdims=True) + p_new
        return (_dot(p.astype(BF16), vals.astype(BF16)) + bf(p_new) * bf(v_new)) / den

    for g in range(g4):
        if g + 1 < g4:
            start_group(g + 1)
        wait_group(g)
        qg = jnp.concatenate(
            [q[:, (g * NSA_GROUP + r) * HEAD_DIM:(g * NSA_GROUP + r + 1) * HEAD_DIM] for r in range(NSA_GROUP)],
            axis=0).astype(BF16)
        o_cmp = jnp.concatenate(
            [ocmp[:, (g * NSA_GROUP + r) * HEAD_DIM:(g * NSA_GROUP + r + 1) * HEAD_DIM] for r in range(NSA_GROUP)],
            axis=0)
        valid = jnp.zeros((1, n_sel), jnp.int32)
        has_new = jnp.int32(0)
        for k in range(NSA_TOPN):
            is_new = (idx_ref[(b * g4 + g) * NSA_TOPN + k] >= n_past_blocks).astype(jnp.int32)
            valid = jnp.where(key_slot == k, 1 - is_new, valid)
            has_new = jnp.maximum(has_new, is_new)
        lanes = lambda slot: slice((slot * g4 + g) * HEAD_DIM, (slot * g4 + g + 1) * HEAD_DIM)
        o_sel = attend(qg, kbuf[g % 2, :, g, :], vbuf[g % 2, :, g, :], valid > 0,
                       kv_new[:, lanes(2)], kv_new[:, lanes(3)], has_new > 0)
        o_win = attend(qg, win_ref[:, 0, g, :], win_ref[:, 1, g, :], None,
                       kwin_new[:, lanes(0)], kwin_new[:, lanes(1)], True)
        out = jnp.zeros((NSA_GROUP, HEAD_DIM), F32)
        for c, o_c in enumerate((o_cmp, o_sel, o_win)):
            pick = lane == c * N_HEADS + g * NSA_GROUP + sub
            gate = jnp.sum(jnp.where(pick, jnp.broadcast_to(gates, (NSA_GROUP, LANES)), 0.0), axis=1, keepdims=True)
            out = out + gate * o_c
        for r in range(NSA_GROUP):
            h = g * NSA_GROUP + r
            o_ref[:, h * HEAD_DIM:(h + 1) * HEAD_DIM] = out[r:r + 1, :]


def _nsa_sample_attend(idx, page_table, q, ocmp, gates, kv4, kwin, win_state, cache_kv):
    bd, n_pages = page_table.shape
    g4 = NSA_KV_GROUPS
    sel_buf = pltpu.VMEM((2, NSA_TOPN * NSA_BLOCK, g4, HEAD_DIM), F32)
    row = lambda w: pl.BlockSpec((None, 1, w), lambda b, *_: (b, 0, 0))
    out = pl.pallas_call(
        functools.partial(_nsa_sample_attend_kernel, n_pages=n_pages, n_past_blocks=n_pages * PAGE_SIZE // NSA_BLOCK),
        grid_spec=pltpu.PrefetchScalarGridSpec(
            num_scalar_prefetch=2,
            grid=(bd,),
            in_specs=[row(D_MODEL), row(D_MODEL), row(LANES), row(4 * NSA_KV), row(2 * NSA_KV),
                      pl.BlockSpec((None,) + win_state.shape[1:], lambda b, *_: (b, 0, 0, 0, 0)),
                      pl.BlockSpec(memory_space=pl.ANY)],
            out_specs=row(D_MODEL),
            scratch_shapes=[sel_buf, sel_buf, pltpu.SemaphoreType.DMA((2, 2, NSA_TOPN))]),
        out_shape=jax.ShapeDtypeStruct((bd, 1, D_MODEL), F32),
        compiler_params=_params(("arbitrary",), 40),
        name="nsa_sample_attend",
    )(idx.reshape(-1), page_table.reshape(-1), q.reshape(bd, 1, D_MODEL), ocmp, gates.reshape(bd, 1, LANES),
      kv4.reshape(bd, 1, 4 * NSA_KV), kwin.reshape(bd, 1, 2 * NSA_KV), win_state, cache_kv)
    return out.reshape(bd, D_MODEL)


def _topk_rows(s, k, ids=None):
    if ids is None:
        ids = lax.broadcasted_iota(jnp.int32, s.shape, 0).astype(F32)
    vals, picked = [], []
    for _ in range(k):
        m = jnp.max(s, axis=0, keepdims=True)
        first = jnp.min(jnp.where(s == m, ids, jnp.inf), axis=0, keepdims=True)
        s = jnp.where(ids == first, -jnp.inf, s)
        vals.append(m)
        picked.append(first)
    return jnp.concatenate(vals, axis=0), jnp.concatenate(picked, axis=0)


def _pair_candidates(v1, v2):
    tokens = v1.shape[1]
    row = lambda n: lax.broadcasted_iota(jnp.int32, (n, tokens), 0).astype(F32)
    vals, ids = [], []
    for k1 in range(4):
        n = 16 if k1 == 0 else 8
        k2 = row(n)
        vals.append(jnp.where((k1 + 1) * (k2 + 1.0) <= PEER_TOPK, v1[k1:k1 + 1, :] + v2[0:n, :], -jnp.inf))
        ids.append(k1 * PEER_TOPK + k2)
    for k2 in range(3):
        n = 16 if k2 == 0 else 8
        k1 = row(n)
        keep = jnp.logical_and(k1 >= 4.0, (k1 + 1.0) * (k2 + 1) <= PEER_TOPK)
        vals.append(jnp.where(keep, v1[0:n, :] + v2[k2:k2 + 1, :], -jnp.inf))
        ids.append(k1 * PEER_TOPK + k2)
    return jnp.concatenate(vals, axis=0), jnp.concatenate(ids, axis=0)


def _pick_rows(sel, table):
    out = jnp.zeros_like(sel)
    for k in range(table.shape[0]):
        out = jnp.where(sel == float(k), table[k:k + 1, :], out)
    return out


def _peer_route_kernel(h_ref, wq_ref, k1_ref, k2_ref, i1_ref, i2_ref, g_ref, qh_ref, i1_t, i2_t, g_t):
    qh_ref[...] = _dot(h_ref[...], wq_ref[...])
    k1 = k1_ref[...].astype(BF16)
    k2 = k2_ref[...].astype(BF16)
    half = PEER_NKEYS

    def head(h, carry):
        c0 = pl.multiple_of(h * 2 * half, 2 * half)
        s1 = _dot_nt(k1, qh_ref[:, pl.ds(c0, half)].astype(BF16))
        s2 = _dot_nt(k2, qh_ref[:, pl.ds(c0 + half, half)].astype(BF16))
        v1, i1 = _topk_rows(s1, PEER_TOPK)
        v2, i2 = _topk_rows(s2, PEER_TOPK)
        cand, cand_id = _pair_candidates(v1, v2)
        sc, j = _topk_rows(cand, PEER_TOPK, cand_id)
        ja = jnp.floor(j * (1.0 / PEER_TOPK))
        jb = j - ja * PEER_TOPK
        e = jnp.exp(sc - jnp.max(sc, axis=0, keepdims=True))
        r0 = pl.multiple_of(h * PEER_TOPK, PEER_TOPK)
        i1_t[pl.ds(r0, PEER_TOPK), :] = _pick_rows(ja, i1)
        i2_t[pl.ds(r0, PEER_TOPK), :] = _pick_rows(jb, i2)
        g_t[pl.ds(r0, PEER_TOPK), :] = e / jnp.sum(e, axis=0, keepdims=True)
        return carry

    lax.fori_loop(0, PEER_HEADS, head, 0)
    i1_ref[...] = jnp.transpose(i1_t[...])
    i2_ref[...] = jnp.transpose(i2_t[...])
    g_ref[...] = jnp.transpose(g_t[...])


def _peer_route(h, wq, k1, k2, tm):
    m, d = h.shape
    tm = min(tm, m)
    hk = PEER_HEADS * PEER_TOPK
    out = pl.BlockSpec((tm, hk), lambda i: (i, 0))
    keys = pl.BlockSpec(k1.shape, lambda i: (0, 0))
    return pl.pallas_call(
        _peer_route_kernel,
        grid=(m // tm,),
        in_specs=[pl.BlockSpec((tm, d), lambda i: (i, 0)), pl.BlockSpec(wq.shape, lambda i: (0, 0)), keys, keys],
        out_specs=[out, out, out],
        out_shape=[jax.ShapeDtypeStruct((m, hk), F32)] * 3,
        scratch_shapes=[pltpu.VMEM((tm, wq.shape[1]), F32)] + [pltpu.VMEM((hk, tm), F32)] * 3,
        compiler_params=_params(("arbitrary",), 40),
        name="peer_route",
    )(h, wq, k1, k2)


PEER_TOKENS_PER_TRIP = 8


def _peer_weights_kernel(i1_ref, i2_ref, g_ref, w_ref):
    tm = i1_ref.shape[0]
    hk = i1_ref.shape[1]
    key = lax.broadcasted_iota(jnp.int32, (PEER_NKEYS, hk), 0).astype(F32)

    def body(step, carry):
        for r in range(PEER_TOKENS_PER_TRIP):
            t = step * PEER_TOKENS_PER_TRIP + r
            i1 = jnp.broadcast_to(i1_ref[pl.ds(t, 1), :], (PEER_NKEYS, hk))
            i2 = jnp.broadcast_to(i2_ref[pl.ds(t, 1), :], (PEER_NKEYS, hk))
            gt = jnp.broadcast_to(g_ref[pl.ds(t, 1), :], (PEER_NKEYS, hk))
            p = jnp.where(i1 == key, 1.0, 0.0).astype(BF16)
            q = jnp.where(i2 == key, gt, 0.0).astype(BF16)
            w_ref[t] = _dot_nt(p, q)
        return carry

    lax.fori_loop(0, tm // PEER_TOKENS_PER_TRIP, body, 0)


def _peer_weights(i1, i2, g, tm):
    m, hk = i1.shape
    tm = min(tm, m)
    spec = pl.BlockSpec((tm, hk), lambda i: (i, 0))
    return pl.pallas_call(
        _peer_weights_kernel,
        grid=(m // tm,),
        in_specs=[spec, spec, spec],
        out_specs=pl.BlockSpec((tm, PEER_NKEYS, PEER_NKEYS), lambda i: (i, 0, 0)),
        out_shape=jax.ShapeDtypeStruct((m, PEER_NKEYS, PEER_NKEYS), F32),
        compiler_params=_params(("arbitrary",), 40),
        name="peer_weights",
    )(i1, i2, g)


def _peer_dense_kernel(h_ref, u_ref, v_ref, x_ref, gate_ref, lng_ref, lnb_ref, scale_ref, shift_ref, w_hbm,
                       xo_ref, ho_ref, acc_ref, wbuf, wsem, *, n_a):
    i = pl.program_id(0)
    c = pl.program_id(1)
    nc = pl.num_programs(1)
    step = i * nc + c
    tm = h_ref.shape[0]

    def map_copies(s):
        rows = pl.ds((s // nc) * tm, tm)
        return [pltpu.make_async_copy(w_hbm.at[rows, (s % nc) * n_a + k, :], wbuf.at[s % 2, k], wsem.at[s % 2, k])
                for k in range(n_a)]

    @pl.when(step == 0)
    def _():
        for cp in map_copies(step):
            cp.start()

    @pl.when(step + 1 < pl.num_programs(0) * nc)
    def _():
        for cp in map_copies(step + 1):
            cp.start()

    @pl.when(c == 0)
    def _():
        acc_ref[...] = jnp.zeros_like(acc_ref)

    act = _gelu(_dot_nt(h_ref[...], u_ref[...]))
    for cp in map_copies(step):
        cp.wait()
    weighted = [act[:, k * PEER_NKEYS:(k + 1) * PEER_NKEYS] * wbuf[step % 2, k] for k in range(n_a)]
    acc_ref[...] += _dot(jnp.concatenate(weighted, axis=1).astype(BF16), v_ref[...])

    @pl.when(c == pl.num_programs(1) - 1)
    def _():
        xn, h = _ln_res(acc_ref[...], x_ref[...], gate_ref[...], lng_ref[...], lnb_ref[...],
                        scale_ref[...], shift_ref[...])
        xo_ref[...] = xn
        ho_ref[...] = h


def _peer_dense(h, u, v, layer, w, xres, gate, lng, lnb, scale, shift, tm, te):
    m, d = h.shape
    n_exp = u.shape[1]
    tm = min(tm, m)
    bpb = (m // tm) // gate.shape[0]
    n_a = te // PEER_NKEYS
    row = pl.BlockSpec((tm, d), lambda i, c: (i, 0))
    tab = pl.BlockSpec((None, te, d), lambda i, c: (layer, c, 0))
    vec = pl.BlockSpec((1, d), lambda i, c: (0, 0))
    return pl.pallas_call(
        functools.partial(_peer_dense_kernel, n_a=n_a),
        grid=(m // tm, n_exp // te),
        in_specs=[row, tab, tab, row, _mod_spec(gate, bpb), vec, vec, _mod_spec(scale, bpb), _mod_spec(shift, bpb),
                  pl.BlockSpec(memory_space=pl.ANY)],
        out_specs=[row, row],
        out_shape=[jax.ShapeDtypeStruct((m, d), F32), jax.ShapeDtypeStruct((m, d), BF16)],
        scratch_shapes=[pltpu.VMEM((tm, d), F32), pltpu.VMEM((2, n_a, tm, PEER_NKEYS), F32),
                        pltpu.SemaphoreType.DMA((2, n_a))],
        compiler_params=_params(("arbitrary", "arbitrary"), 56),
        name="peer_dense",
    )(h, u, v, xres, gate, lng, lnb, scale, shift, w)


def _peer(h, xres, wq, k1, k2, u, v, layer, gate, lng, lnb, scale, shift, tm_route, tm_dense, te):
    i1, i2, g = _peer_route(h, wq, k1, k2, tm_route)
    w = _peer_weights(i1, i2, g, 128)
    return _peer_dense(h, u, v, layer, w, xres, gate, lng, lnb, scale, shift, tm_dense, te)


def _rope_tables(pos):
    half = ROPE_DIMS // 2
    inv = jnp.power(ROPE_THETA, -jnp.arange(half, dtype=F32) / half)
    ang = pos.astype(F32)[:, None] * inv
    cos, sin = jnp.cos(ang), jnp.sin(ang)
    n = pos.shape[0]
    ones = jnp.ones((n, HEAD_DIM - ROPE_DIMS), F32)
    zeros = jnp.zeros((n, HEAD_DIM - ROPE_DIMS), F32)
    zh = jnp.zeros((n, half), F32)
    return (jnp.concatenate([cos, cos, ones], axis=1),
            jnp.concatenate([-sin, zh, zeros], axis=1),
            jnp.concatenate([zh, sin, zeros], axis=1))


def _pad_rows(x, rows):
    return jnp.pad(x, ((0, rows - x.shape[0]),) + ((0, 0),) * (x.ndim - 1))


def kernel(x_prompt, x_sample, c_prompt, c_sample, cache_l0_kv, cache_l1_kv, state_l1_win, cache_l2_kv, cache_l3_kv,
           state_l3_win, page_table, ada_w, ada_b, ln1_g, ln1_b, ln2_g, ln2_b, sb_w_in, sb_w_o, nsa_w_in, nsa_w_o,
           nsa_cmp_pe, nsa_cmp_w1, nsa_cmp_b1, nsa_cmp_w2, nsa_cmp_b2, peer_wq, peer_k1, peer_k2, peer_u, peer_v):
    batch, seq, d = x_prompt.shape
    bd, dec_seq, _ = x_sample.shape
    assert dec_seq == 1 and d == D_MODEL
    depth = ada_w.shape[0]
    n_pages = page_table.shape[1]
    past = n_pages * PAGE_SIZE
    sb_caches = (cache_l0_kv, cache_l2_kv)
    nsa_caches = (cache_l1_kv, cache_l3_kv)
    nsa_wins = (state_l1_win, state_l3_win)
    win_buf = state_l1_win.shape[1]
    assert win_buf == NSA_WINDOW and past % NSA_BLOCK == 0 and seq >= win_buf
    mp_rows = batch * seq
    tm = 512

    n_c = batch + bd
    c_all = _pad_rows(jnp.concatenate([c_prompt, c_sample], axis=0), -(-n_c // 8) * 8)
    mod = _modulation(c_all, ada_w, ada_b).reshape(depth, c_all.shape[0], 6, d)
    mod_p = lambda i, k: mod[i, :batch, k].reshape(batch, 1, d)
    mod_s = lambda i, k: mod[i, batch:n_c, k].reshape(1, bd, d)

    xp = x_prompt.reshape(mp_rows, d)
    xs = x_sample.reshape(bd, d)
    hp = _modulate(xp, mod_p(0, 1), mod_p(0, 0), tm)
    hs = _modulate(xs, mod_s(0, 1), mod_s(0, 0), bd)

    rope_p = _rope_tables(jnp.arange(seq))
    rope_s = _rope_tables(jnp.full((bd,), past))
    nb_p = -(-seq // NSA_BLOCK)
    assert nb_p * NSA_BLOCK == seq
    nb_s = past // NSA_BLOCK + 1

    u_all = peer_u.astype(BF16)
    v_all = peer_v.astype(BF16)
    new = []
    for i in range(depth):
        j = i // 2
        vec = lambda a: a[i].reshape(1, d)
        if i % 2 == 0:
            w_in = sb_w_in[j].astype(BF16)
            w_o = sb_w_o[j].astype(BF16)
            sb_rows = (2, N_HEADS)
            q_p = _matmul(hp, w_in[:, :d], tm, 512, out_dtype=BF16, name="sb_q")
            kv_p, kv_rows_p = _matmul(hp, w_in[:, d:], tm, 1024, out_dtype=BF16, rows_out=sb_rows, name="sb_kv")
            o_p = _sb_prompt_attention(q_p, kv_p, batch, seq)
            q_s = _matmul(hs, w_in[:, :d], bd, 512, out_dtype=BF16, name="sb_q")
            _, kv_rows_s = _matmul(hs, w_in[:, d:], bd, 1024, out_dtype=BF16, rows_out=sb_rows, name="sb_kv")
            o_s = _sb_sample_attention(q_s, sb_caches[j], page_table)
            new.append((kv_rows_p.reshape(batch, seq, 2, N_HEADS, HEAD_DIM),
                        kv_rows_s.reshape(bd, 1, 2, N_HEADS, HEAD_DIM)))
        else:
            w_in = nsa_w_in[j].astype(BF16)
            w_o = nsa_w_o[j].astype(BF16)
            w_q = w_in[:, :d]
            w_kv4 = w_in[:, d:d + 4 * NSA_KV]
            w_kw = w_in[:, d + 4 * NSA_KV:d + 6 * NSA_KV]
            w_g = jnp.pad(w_in[:, d + 6 * NSA_KV:], ((0, 0), (0, LANES - 3 * N_HEADS)))
            cmp_w = (nsa_cmp_pe[j], nsa_cmp_w1[j].astype(BF16), nsa_cmp_b1[j], nsa_cmp_w2[j], nsa_cmp_b2[j])

            def project(h, rope, rows):
                kv4, kv4_rows = _matmul(h, w_kv4, rows, NSA_KV, epilogue="rope_even", rope=rope,
                                        rows_out=(4, NSA_KV_GROUPS), name="nsa_kv4")
                kwin, kwin_rows = _matmul(h, w_kw, rows, NSA_KV, epilogue="rope_even", rope=rope,
                                          rows_out=(2, NSA_KV_GROUPS), name="nsa_kwin")
                return (_matmul(h, w_q, rows, 512, epilogue="rope_all", rope=rope, name="nsa_q"), kv4, kwin,
                        _matmul(h, w_g, rows, LANES, epilogue="sigmoid", name="nsa_gates"), kv4_rows, kwin_rows)

            q_p, kv4_p, kwin_p, gates_p, kv4_rows_p, kwin_rows_p = project(hp, rope_p, tm)
            cmp_p = _compress(kv4_p.reshape(batch * nb_p, NSA_BLOCK, 4 * NSA_KV), *cmp_w, 256)
            nbp = -(-nb_p // LANES) * LANES
            cmp_p = jnp.pad(cmp_p.reshape(2, batch, nb_p, NSA_KV), ((0, 0), (0, 0), (0, nbp - nb_p), (0, 0)))
            o_p = _nsa_prompt_attention(q_p, kv4_p, kwin_p, gates_p, cmp_p[0], cmp_p[1], batch, seq, nb_p)
            win_p = kwin_rows_p.reshape(batch, seq, 2, NSA_KV_GROUPS, HEAD_DIM)[:, seq - win_buf:]
            q_s, kv4_s, kwin_s, gates_s, kv4_rows_s, kwin_rows_s = project(hs, rope_s, bd)
            x_past = _gather_cmp_pages(nsa_caches[j], page_table)
            cmp_past = _compress(x_past.reshape(bd * (nb_s - 1), NSA_BLOCK, 2 * NSA_KV), *cmp_w, 256)
            x_last = jnp.pad(kv4_s[:, None, :2 * NSA_KV], ((0, 0), (0, NSA_BLOCK - 1), (0, 0)))
            cmp_last = _compress(x_last, *cmp_w, bd)
            nbs = -(-nb_s // LANES) * LANES
            cmp_s = jnp.concatenate([cmp_past.reshape(2, bd, nb_s - 1, NSA_KV), cmp_last.reshape(2, bd, 1, NSA_KV)], 2)
            cmp_s = jnp.pad(cmp_s, ((0, 0), (0, 0), (0, nbs - nb_s), (0, 0)))
            ocmp_s, idx_s = _nsa_sample_select(q_s, cmp_s[0], cmp_s[1], nb_s, past)
            o_s = _nsa_sample_attend(idx_s, page_table, q_s, ocmp_s, gates_s, kv4_s, kwin_s, nsa_wins[j], nsa_caches[j])
            win_s = jnp.concatenate([nsa_wins[j][:, 1:], kwin_rows_s.reshape(bd, 1, 2, NSA_KV_GROUPS, HEAD_DIM)], axis=1)
            new.append((kv4_rows_p.reshape(batch, seq, 4, NSA_KV_GROUPS, HEAD_DIM),
                        kv4_rows_s.reshape(bd, 1, 4, NSA_KV_GROUPS, HEAD_DIM), win_p, win_s))

        xp, hp = _matmul_ln(o_p, w_o, xp, mod_p(i, 2), vec(ln1_g), vec(ln1_b), mod_p(i, 4), mod_p(i, 3), tm)
        xs, hs = _matmul_ln(o_s, w_o, xs, mod_s(i, 2), vec(ln1_g), vec(ln1_b), mod_s(i, 4), mod_s(i, 3), bd)

        nxt = min(i + 1, depth - 1)
        wq = peer_wq[i].astype(BF16)
        xp, hp = _peer(hp, xp, wq, peer_k1[i], peer_k2[i], u_all, v_all, i, mod_p(i, 5), vec(ln2_g), vec(ln2_b),
                       mod_p(nxt, 1), mod_p(nxt, 0), 256, 512, 512)
        pad = LANES
        hs_pad = _pad_rows(hs, pad)
        gate_s = _pad_rows(mod_s(i, 5)[0], pad)[None]
        scale_s = _pad_rows(mod_s(nxt, 1)[0], pad)[None]
        shift_s = _pad_rows(mod_s(nxt, 0)[0], pad)[None]
        xs_pad, hs_pad = _peer(hs_pad, _pad_rows(xs, pad), wq, peer_k1[i], peer_k2[i], u_all, v_all, i, gate_s,
                               vec(ln2_g), vec(ln2_b), scale_s, shift_s, pad, pad, 512)
        xs, hs = xs_pad[:bd], hs_pad[:bd]

    return (xp.reshape(batch, seq, d), xs.reshape(bd, 1, d), new[0][0], new[0][1], new[1][0], new[1][1], new[1][2],
            new[1][3], new[2][0], new[2][1], new[3][0], new[3][1], new[3][2], new[3][3])
```

```python
import functools

import jax
import jax.numpy as jnp
import numpy as np
from jax import lax
from jax.experimental import pallas as pl
from jax.experimental.pallas import tpu as pltpu

F32 = jnp.float32
BF16 = jnp.bfloat16

D_MODEL = 2048
HEAD_DIM = 128
N_HEADS = D_MODEL // HEAD_DIM
PAGE_SIZE = 128
NSA_KV_GROUPS = 4
NSA_GROUP = N_HEADS // NSA_KV_GROUPS
NSA_KV = NSA_KV_GROUPS * HEAD_DIM
NSA_BLOCK = 64
NSA_TOPN = 16
NSA_WINDOW = 512
ROPE_THETA = 500000.0
ROPE_DIMS = HEAD_DIM // 4
PEER_HEADS = 8
PEER_NKEYS = 128
PEER_TOPK = 16
DEPTH = 4
DEEPNORM_ALPHA = (2 * DEPTH) ** 0.25
LN_EPS = 1e-5
NEG_INF = -1e30
FORCE_SCORE = 1e9
ATTN_SCALE = HEAD_DIM ** -0.5
SQRT_HALF = float(np.sqrt(0.5))

LANES = 128
VMEM_LIMIT_CAP = 56 * 1024 * 1024
SB_LOG_UNDERFLOW = -104.0


def _params(sem, vmem_mb):
    return pltpu.CompilerParams(
        dimension_semantics=sem, vmem_limit_bytes=min(vmem_mb * 1024 * 1024, VMEM_LIMIT_CAP))


def _dot(a, b):
    return jnp.dot(a, b, preferred_element_type=F32)


def _dot_nt(a, b):
    return lax.dot_general(a, b, (((1,), (1,)), ((), ())), preferred_element_type=F32)


def _gelu(x):
    return 0.5 * x * (1.0 + lax.erf(x * SQRT_HALF))


def _log_sigmoid_pair(z):
    l1p = jnp.log(1.0 + jnp.exp(-jnp.abs(z)))
    return jnp.minimum(z, 0.0) - l1p, jnp.minimum(-z, 0.0) - l1p


def _ln_res(y, xres, gate, lng, lnb, scale, shift):
    v = DEEPNORM_ALPHA * xres + gate * y
    mu = jnp.mean(v, axis=-1, keepdims=True)
    d = v - mu
    var = jnp.mean(d * d, axis=-1, keepdims=True)
    xn = d * lax.rsqrt(var + LN_EPS) * lng + lnb
    return xn, (xn * (1.0 + scale) + shift).astype(BF16)


def _mod_spec(arr, blocks_per_batch):
    return pl.BlockSpec((None,) + arr.shape[1:], lambda i, *_: (i // blocks_per_batch, 0, 0))


def _modulation_kernel(c_ref, w_ref, b_ref, o_ref):
    c = c_ref[...]
    s = (c * jax.nn.sigmoid(c)).astype(BF16)
    o_ref[...] = _dot(s, w_ref[...].astype(BF16)) + b_ref[...]


def _modulation(c_all, ada_w, ada_b):
    depth, d, n = ada_w.shape
    r = c_all.shape[0]
    tn = 1024
    return pl.pallas_call(
        _modulation_kernel,
        grid=(depth, n // tn),
        in_specs=[
            pl.BlockSpec((r, d), lambda l, j: (0, 0)),
            pl.BlockSpec((None, d, tn), lambda l, j: (l, 0, j)),
            pl.BlockSpec((None, 1, tn), lambda l, j: (l, 0, j)),
        ],
        out_specs=pl.BlockSpec((None, r, tn), lambda l, j: (l, 0, j)),
        out_shape=jax.ShapeDtypeStruct((depth, r, n), F32),
        compiler_params=_params(("arbitrary", "arbitrary"), 32),
        name="modulation",
    )(c_all, ada_w, ada_b.reshape(depth, 1, n))


def _modulate_kernel(x_ref, scale_ref, shift_ref, o_ref):
    o_ref[...] = (x_ref[...] * (1.0 + scale_ref[...]) + shift_ref[...]).astype(BF16)


def _modulate(x, scale, shift, tm):
    m, d = x.shape
    bpb = (m // tm) // scale.shape[0]
    return pl.pallas_call(
        _modulate_kernel,
        grid=(m // tm,),
        in_specs=[pl.BlockSpec((tm, d), lambda i: (i, 0)), _mod_spec(scale, bpb), _mod_spec(shift, bpb)],
        out_specs=pl.BlockSpec((tm, d), lambda i: (i, 0)),
        out_shape=jax.ShapeDtypeStruct((m, d), BF16),
        compiler_params=_params(("arbitrary",), 32),
        name="modulate",
    )(x, scale, shift)


def _store_tile(y, o_refs):
    o_refs[0][...] = y.astype(o_refs[0].dtype)
    if len(o_refs) > 1:
        for h in range(o_refs[1].shape[1]):
            o_refs[1][:, h, :] = y[:, h * HEAD_DIM:(h + 1) * HEAD_DIM]


def _mm_kernel(a_ref, w_ref, *o_refs):
    _store_tile(_dot(a_ref[...], w_ref[...]), o_refs)


def _mm_sigmoid_kernel(a_ref, w_ref, o_ref):
    o_ref[...] = jax.nn.sigmoid(_dot(a_ref[...], w_ref[...]))


def _mm_rope_kernel(a_ref, w_ref, cos_ref, sa_ref, sb_ref, *o_refs, even_only):
    acc = _dot(a_ref[...], w_ref[...])
    tn = acc.shape[1]
    reps = tn // HEAD_DIM
    cos = jnp.concatenate([cos_ref[...]] * reps, axis=1)
    sa = jnp.concatenate([sa_ref[...]] * reps, axis=1)
    sb = jnp.concatenate([sb_ref[...]] * reps, axis=1)
    half = ROPE_DIMS // 2
    rot = acc * cos + pltpu.roll(acc, tn - half, 1) * sa + pltpu.roll(acc, half, 1) * sb
    if even_only:
        rot = jnp.where(pl.program_id(1) % 2 == 0, rot, acc)
    _store_tile(rot, o_refs)


def _matmul(a, w, tm, tn, *, epilogue=None, rope=None, out_dtype=F32, rows_out=None, name="matmul"):
    m, k = a.shape
    n = w.shape[1]
    tm = min(tm, m)
    in_specs = [pl.BlockSpec((tm, k), lambda i, j: (i, 0)), pl.BlockSpec((k, tn), lambda i, j: (0, j))]
    args = [a, w]
    if epilogue in ("rope_all", "rope_even"):
        assert epilogue == "rope_all" or tn == NSA_KV
        t_tab = rope[0].shape[0]
        nt = t_tab // tm
        in_specs += [pl.BlockSpec((tm, HEAD_DIM), lambda i, j: (i % nt, 0))] * 3
        args += list(rope)
        kern = functools.partial(_mm_rope_kernel, even_only=(epilogue == "rope_even"))
    elif epilogue == "sigmoid":
        kern = _mm_sigmoid_kernel
    else:
        kern = _mm_kernel
    out_specs = pl.BlockSpec((tm, tn), lambda i, j: (i, j))
    out_shape = jax.ShapeDtypeStruct((m, n), out_dtype)
    if rows_out is not None:
        slots, heads = rows_out
        tiles_per_slot = heads * HEAD_DIM // tn
        out_specs = [out_specs, pl.BlockSpec((tm, None, heads // tiles_per_slot, HEAD_DIM),
                                             lambda i, j: (i, j // tiles_per_slot, j % tiles_per_slot, 0))]
        out_shape = [out_shape, jax.ShapeDtypeStruct((m, slots, heads, HEAD_DIM), F32)]
    return pl.pallas_call(
        kern,
        grid=(m // tm, n // tn),
        in_specs=in_specs,
        out_specs=out_specs,
        out_shape=out_shape,
        compiler_params=_params(("arbitrary", "arbitrary"), 40),
        name=name,
    )(*args)


def _mm_ln_kernel(a_ref, w_ref, x_ref, gate_ref, lng_ref, lnb_ref, scale_ref, shift_ref, xo_ref, ho_ref):
    y = _dot(a_ref[...].astype(BF16), w_ref[...])
    xn, h = _ln_res(y, x_ref[...], gate_ref[...], lng_ref[...], lnb_ref[...], scale_ref[...], shift_ref[...])
    xo_ref[...] = xn
    ho_ref[...] = h


def _matmul_ln(a, w, xres, gate, lng, lnb, scale, shift, tm):
    m, k = a.shape
    d = w.shape[1]
    tm = min(tm, m)
    bpb = (m // tm) // gate.shape[0]
    row = pl.BlockSpec((tm, d), lambda i: (i, 0))
    vec = pl.BlockSpec((1, d), lambda i: (0, 0))
    return pl.pallas_call(
        _mm_ln_kernel,
        grid=(m // tm,),
        in_specs=[pl.BlockSpec((tm, k), lambda i: (i, 0)), pl.BlockSpec((k, d), lambda i: (0, 0)), row,
                  _mod_spec(gate, bpb), vec, vec, _mod_spec(scale, bpb), _mod_spec(shift, bpb)],
        out_specs=[row, row],
        out_shape=[jax.ShapeDtypeStruct((m, d), F32), jax.ShapeDtypeStruct((m, d), BF16)],
        compiler_params=_params(("arbitrary",), 48),
        name="matmul_ln",
    )(a, w, xres, gate, lng, lnb, scale, shift)


def _suffix_matrix(n):
    return (lax.broadcasted_iota(jnp.int32, (n, n), 0) > lax.broadcasted_iota(jnp.int32, (n, n), 1)).astype(BF16)


def _suffix_sum(x, lmat):
    hi = x.astype(BF16)
    lo = (x - hi.astype(F32)).astype(BF16)
    return _dot(hi, lmat) + _dot(lo, lmat)


SB_HEADS_PER_STEP = 8


def _sb_prompt_kernel(q_ref, k_ref, v_ref, o_ref, carry_ref, acc_ref, *, tq, tk):
    i = pl.program_id(2)
    carry_ref[...] = jnp.zeros_like(carry_ref)
    acc_ref[...] = jnp.zeros_like(acc_ref)
    qpos = i * tq + lax.broadcasted_iota(jnp.int32, (tq, 1), 0)
    lmat = jnp.concatenate([_suffix_matrix(tk), jnp.ones((tk, tk), BF16)], axis=1)

    def body(state):
        j, _ = state
        k0 = pl.multiple_of(j * tk, tk)
        allowed = (k0 + lax.broadcasted_iota(jnp.int32, (1, tk), 1)) < qpos
        heads = range(SB_HEADS_PER_STEP)
        lanes = [slice(hh * HEAD_DIM, (hh + 1) * HEAD_DIM) for hh in heads]
        zs = [_dot_nt(q_ref[:, lanes[hh]].astype(BF16), k_ref[pl.ds(k0, tk), lanes[hh]].astype(BF16)) * ATTN_SCALE
              for hh in heads]
        pairs = [_log_sigmoid_pair(z) for z in zs]
        sums = [_suffix_sum(jnp.where(allowed, lsn, 0.0), lmat) for _, lsn in pairs]
        carries = [carry_ref[hh] for hh in heads]
        probs = [jnp.where(allowed, jnp.exp(pairs[hh][0] + sums[hh][:, :tk] + carries[hh]), 0.0).astype(BF16)
                 for hh in heads]
        pvs = [_dot(probs[hh], v_ref[pl.ds(k0, tk), lanes[hh]].astype(BF16)) for hh in heads]
        top = jnp.full((tq, tk), -jnp.inf, F32)
        for hh in heads:
            acc_ref[:, lanes[hh]] += pvs[hh]
            carry = carries[hh] + sums[hh][:, tk:]
            carry_ref[hh] = carry
            top = jnp.maximum(top, carry)
        return j - 1, (jnp.max(top) > SB_LOG_UNDERFLOW).astype(jnp.int32)

    lax.while_loop(lambda s: jnp.logical_and(s[0] >= 0, s[1] > 0), body, ((i * tq + tq - 1) // tk, jnp.int32(1)))
    o_ref[...] = acc_ref[...]


def _sb_prompt_attention(q, kv, batch, seq, tq=256, tk=128):
    nq = seq // tq
    hb = SB_HEADS_PER_STEP
    width = hb * HEAD_DIM
    n_hb = N_HEADS // hb
    return pl.pallas_call(
        functools.partial(_sb_prompt_kernel, tq=tq, tk=tk),
        grid=(batch, n_hb, nq),
        in_specs=[
            pl.BlockSpec((tq, width), lambda b, h, i: (b * nq + i, h)),
            pl.BlockSpec((seq, width), lambda b, h, i: (b, h)),
            pl.BlockSpec((seq, width), lambda b, h, i: (b, n_hb + h)),
        ],
        out_specs=pl.BlockSpec((tq, width), lambda b, h, i: (b * nq + i, h)),
        out_shape=jax.ShapeDtypeStruct(q.shape, F32),
        scratch_shapes=[pltpu.VMEM((hb, tq, tk), F32), pltpu.VMEM((tq, width), F32)],
        compiler_params=_params(("arbitrary", "arbitrary", "arbitrary"), 48),
        name="sb_prompt_attention",
    )(q, kv, kv)


SB_ROW_REP = 8


def _sb_sample_kernel(pt_ref, q_ref, cache_ref, o_ref, kbuf, vbuf, sem, carry_ref, acc_ref, *, n_pages):
    b = pl.program_id(0)
    rep = SB_ROW_REP
    q = q_ref[...]
    qh = [jnp.broadcast_to(q[:, h * HEAD_DIM:(h + 1) * HEAD_DIM], (rep, HEAD_DIM)).astype(BF16)
          for h in range(N_HEADS)]
    lmat = _suffix_matrix(PAGE_SIZE)
    carry_ref[...] = jnp.zeros_like(carry_ref)
    acc_ref[...] = jnp.zeros_like(acc_ref)

    def page_copies(page):
        return (pltpu.make_async_copy(cache_ref.at[page, :, 0], kbuf, sem.at[0]),
                pltpu.make_async_copy(cache_ref.at[page, :, 1], vbuf, sem.at[1]))

    def body(state):
        p, _ = state
        ck, cv = page_copies(pt_ref[b * n_pages + p])
        ck.start()
        cv.start()
        ck.wait()
        cv.wait()
        z = jnp.concatenate([_dot_nt(qh[h], kbuf[:, h, :].astype(BF16)) for h in range(N_HEADS)], axis=0)
        ls, lsn = _log_sigmoid_pair(z * ATTN_SCALE)
        later = _suffix_sum(lsn, lmat) + carry_ref[...]
        a = jnp.exp(ls + later)
        pv = [_dot(a[h * rep:(h + 1) * rep, :].astype(BF16), vbuf[:, h, :].astype(BF16)) for h in range(N_HEADS)]
        acc_ref[...] += jnp.concatenate(pv, axis=0)
        carry = carry_ref[...] + jnp.sum(lsn, axis=1, keepdims=True)
        carry_ref[...] = carry
        return p - 1, (jnp.max(carry) > SB_LOG_UNDERFLOW).astype(jnp.int32)

    lax.while_loop(lambda s: jnp.logical_and(s[0] >= 0, s[1] > 0), body, (jnp.int32(n_pages - 1), jnp.int32(1)))
    for h in range(N_HEADS):
        o_ref[:, h * HEAD_DIM:(h + 1) * HEAD_DIM] = acc_ref[h * rep:h * rep + 1, :]


def _sb_sample_attention(q, cache_kv, page_table):
    bd, n_pages = page_table.shape
    page_buf = pltpu.VMEM((PAGE_SIZE, N_HEADS, HEAD_DIM), F32)
    out = pl.pallas_call(
        functools.partial(_sb_sample_kernel, n_pages=n_pages),
        grid_spec=pltpu.PrefetchScalarGridSpec(
            num_scalar_prefetch=1,
            grid=(bd,),
            in_specs=[pl.BlockSpec((None, 1, D_MODEL), lambda b, pt: (b, 0, 0)),
                      pl.BlockSpec(memory_space=pl.ANY)],
            out_specs=pl.BlockSpec((None, 1, D_MODEL), lambda b, pt: (b, 0, 0)),
            scratch_shapes=[page_buf, page_buf, pltpu.SemaphoreType.DMA((2,)),
                            pltpu.VMEM((N_HEADS * SB_ROW_REP, 1), F32),
                            pltpu.VMEM((N_HEADS * SB_ROW_REP, HEAD_DIM), F32)]),
        out_shape=jax.ShapeDtypeStruct((bd, 1, D_MODEL), F32),
        compiler_params=_params(("arbitrary",), 16),
        name="sb_sample_attention",
    )(page_table.reshape(-1), q.reshape(bd, 1, D_MODEL), cache_kv)
    return out.reshape(bd, D_MODEL)


def _compress_kernel(x_hbm, pe_ref, w1_ref, b1_ref, w2_ref, b2_ref, o_ref, xbuf, sem, *, tr):
    n_i = pl.num_programs(2)
    step = (pl.program_id(0) * pl.num_programs(1) + pl.program_id(1)) * n_i + pl.program_id(2)
    total = pl.num_programs(0) * pl.num_programs(1) * n_i

    def block_copies(s):
        rows = pl.ds((s % n_i) * tr, tr)
        lanes = pl.ds((s // n_i) * HEAD_DIM, HEAD_DIM)
        return [pltpu.make_async_copy(x_hbm.at[rows, t, lanes], xbuf.at[s % 2, t], sem.at[s % 2, t])
                for t in range(NSA_BLOCK)]

    @pl.when(step == 0)
    def _():
        for cp in block_copies(step):
            cp.start()

    @pl.when(step + 1 < total)
    def _():
        for cp in block_copies(step + 1):
            cp.start()

    for cp in block_copies(step):
        cp.wait()
    acc = jnp.zeros((tr, w1_ref.shape[1]), F32)
    for t in range(NSA_BLOCK):
        xt = (xbuf[step % 2, t] + pe_ref[t:t + 1, :]).astype(BF16)
        acc = acc + _dot(xt, w1_ref[t * HEAD_DIM:(t + 1) * HEAD_DIM, :])
    hid = _gelu(acc + b1_ref[...])
    o_ref[...] = _dot(hid.astype(BF16), w2_ref[...].astype(BF16)) + b2_ref[...]


def _compress(x, pe, w1, b1, w2, b2, tr):
    r = x.shape[0]
    tr = min(tr, r)
    hidden = w1.shape[2]
    return pl.pallas_call(
        functools.partial(_compress_kernel, tr=tr),
        grid=(2, NSA_KV_GROUPS, r // tr),
        in_specs=[
            pl.BlockSpec(memory_space=pl.ANY),
            pl.BlockSpec((None, NSA_BLOCK, HEAD_DIM), lambda s, g, i: (s, 0, 0)),
            pl.BlockSpec((None, NSA_BLOCK * HEAD_DIM, hidden), lambda s, g, i: (s, 0, 0)),
            pl.BlockSpec((None, 1, hidden), lambda s, g, i: (s, 0, 0)),
            pl.BlockSpec((None, hidden, HEAD_DIM), lambda s, g, i: (s, 0, 0)),
            pl.BlockSpec((None, 1, HEAD_DIM), lambda s, g, i: (s, 0, 0)),
        ],
        out_specs=pl.BlockSpec((None, tr, HEAD_DIM), lambda s, g, i: (s, i, g)),
        out_shape=jax.ShapeDtypeStruct((2, r, NSA_KV), F32),
        scratch_shapes=[pltpu.VMEM((2, NSA_BLOCK, tr, HEAD_DIM), F32), pltpu.SemaphoreType.DMA((2, NSA_BLOCK))],
        compiler_params=_params(("arbitrary", "arbitrary", "arbitrary"), 48),
        name="nsa_compress",
    )(x, pe, w1, b1.reshape(2, 1, hidden), w2, b2.reshape(2, 1, HEAD_DIM))


GATHER_PAGES_IN_FLIGHT = 8


def _page_gather_kernel(pt_ref, cache_ref, o_ref, sem, *, n_pages):
    b = pl.program_id(0)
    g4 = NSA_KV_GROUPS
    window = min(GATHER_PAGES_IN_FLIGHT, n_pages)

    def page_copies(p):
        page = pt_ref[b * n_pages + p]
        rows = pl.ds(p * PAGE_SIZE, PAGE_SIZE)
        return [pltpu.make_async_copy(cache_ref.at[page, :, s, g, :],
                                      o_ref.at[b, rows, pl.ds((s * g4 + g) * HEAD_DIM, HEAD_DIM)],
                                      sem.at[p % window, s * g4 + g])
                for s in range(2) for g in range(g4)]

    for p in range(window):
        for cp in page_copies(p):
            cp.start()

    def body(p, carry):
        for cp in page_copies(p):
            cp.wait()

        @pl.when(p + window < n_pages)
        def _():
            for cp in page_copies(p + window):
                cp.start()

        return carry

    lax.fori_loop(0, n_pages, body, 0)


def _gather_cmp_pages(cache_kv, page_table):
    bd, n_pages = page_table.shape
    width = 2 * NSA_KV
    return pl.pallas_call(
        functools.partial(_page_gather_kernel, n_pages=n_pages),
        grid_spec=pltpu.PrefetchScalarGridSpec(
            num_scalar_prefetch=1,
            grid=(bd,),
            in_specs=[pl.BlockSpec(memory_space=pl.ANY)],
            out_specs=pl.BlockSpec(memory_space=pl.ANY),
            scratch_shapes=[pltpu.SemaphoreType.DMA((GATHER_PAGES_IN_FLIGHT, 2 * NSA_KV_GROUPS))]),
        out_shape=jax.ShapeDtypeStruct((bd, n_pages * PAGE_SIZE, width), F32),
        compiler_params=_params(("arbitrary",), 16),
        name="nsa_page_gather",
    )(page_table.reshape(-1), cache_kv)


def _online_softmax_step(z, v_t, m_ref, l_ref, acc_ref):
    m_new = jnp.maximum(m_ref[...], jnp.max(z, axis=0, keepdims=True))
    alpha = jnp.exp(m_ref[...] - m_new)
    p = jnp.exp(z - m_new)
    l_ref[...] = alpha * l_ref[...] + jnp.sum(p, axis=0, keepdims=True)
    acc_ref[...] = alpha * acc_ref[...] + _dot(v_t, p.astype(BF16))
    m_ref[...] = m_new


def _nsa_prompt_kernel(q_ref, kc_ref, vc_ref, ks_ref, vs_ref, kw_ref, vw_ref, gate_ref, o_ref,
                       m_ref, l_ref, acc_ref, osel_ref, gate_t_ref, *, nb, tq, tk):
    g = pl.program_id(1)
    i = pl.program_id(2)
    nq = NSA_GROUP * tq
    nbp = kc_ref.shape[0]
    qblk = q_ref[...]
    qg = jnp.concatenate([qblk[:, r * HEAD_DIM:(r + 1) * HEAD_DIM] for r in range(NSA_GROUP)], axis=0).astype(BF16)
    qpos = i * tq + lax.broadcasted_iota(jnp.int32, (1, tq), 1)
    tile4 = lambda x: jnp.concatenate([x] * NSA_GROUP, axis=1)

    blk = lax.broadcasted_iota(jnp.int32, (nbp, 1), 0)
    zc = _dot_nt(kc_ref[...].astype(BF16), qg) * ATTN_SCALE
    avail = tile4(jnp.logical_and((blk + 1) * NSA_BLOCK - 1 <= qpos, blk < nb))
    zm = jnp.where(avail, zc, NEG_INF)
    e = jnp.where(avail, jnp.exp(zm - jnp.max(zm, axis=0, keepdims=True)), 0.0)
    den = jnp.sum(e, axis=0, keepdims=True)
    pc = e / jnp.where(den > 0.0, den, 1.0)
    o_cmp = _dot(jnp.transpose(vc_ref[...]).astype(BF16), pc.astype(BF16))

    score = pc[:, 0:tq]
    for r in range(1, NSA_GROUP):
        score = score + pc[:, r * tq:(r + 1) * tq]
    nbr = -(-nb // 8) * 8
    blk = blk[0:nbr]
    cur = qpos // NSA_BLOCK
    forced = jnp.logical_or(blk == 0, jnp.logical_or(blk == cur, blk == cur - 1))
    s = jnp.where(blk <= cur, jnp.where(forced, FORCE_SCORE, score[0:nbr]), -1.0)
    rank = jnp.zeros((nbr, tq), F32)
    for mth in range(nb):
        sm = s[mth:mth + 1, :]
        ahead = jnp.logical_or(sm > s, jnp.logical_and(sm == s, blk > mth))
        rank = rank + jnp.where(ahead, 1.0, 0.0)
    sel = jnp.where(jnp.logical_and(rank < float(min(NSA_TOPN, nb)), s >= 0.0), 1.0, 0.0).astype(BF16)

    def reset():
        m_ref[...] = jnp.full_like(m_ref, NEG_INF)
        l_ref[...] = jnp.zeros_like(l_ref)
        acc_ref[...] = jnp.zeros_like(acc_ref)

    reset()
    bpt = tk // NSA_BLOCK

    def sel_body(kt, carry):
        k0 = pl.multiple_of(kt * tk, tk)
        z = _dot_nt(ks_ref[pl.ds(k0, tk), :].astype(BF16), qg) * ATTN_SCALE
        key = lax.broadcasted_iota(jnp.int32, (tk, nbr), 0)
        col = lax.broadcasted_iota(jnp.int32, (tk, nbr), 1)
        expand = (col == kt * bpt + key // NSA_BLOCK).astype(BF16)
        chosen = _dot(expand, sel) > 0.5
        kpos = k0 + lax.broadcasted_iota(jnp.int32, (tk, 1), 0)
        bias = jnp.where(jnp.logical_and(chosen, kpos <= qpos), 0.0, NEG_INF)
        v_t = jnp.transpose(vs_ref[pl.ds(k0, tk), :]).astype(BF16)
        _online_softmax_step(z + tile4(bias), v_t, m_ref, l_ref, acc_ref)
        return carry

    lax.fori_loop(0, (i * tq + tq - 1) // tk + 1, sel_body, 0)
    osel_ref[...] = acc_ref[...] / l_ref[...]

    reset()

    def win_body(kt, carry):
        k0 = pl.multiple_of(kt * tq, tq)
        z = _dot_nt(kw_ref[pl.ds(k0, tq), :].astype(BF16), qg) * ATTN_SCALE
        dist = qpos - (k0 + lax.broadcasted_iota(jnp.int32, (tq, 1), 0))
        bias = jnp.where(jnp.logical_and(dist >= 0, dist <= NSA_WINDOW), 0.0, NEG_INF)
        v_t = jnp.transpose(vw_ref[pl.ds(k0, tq), :]).astype(BF16)
        _online_softmax_step(z + tile4(bias), v_t, m_ref, l_ref, acc_ref)
        return carry

    lax.fori_loop(jnp.maximum(i - NSA_WINDOW // tq, 0), i + 1, win_body, 0)
    o_win = acc_ref[...] / l_ref[...]
    o_sel = osel_ref[...]

    gate_t_ref[...] = jnp.transpose(gate_ref[...])
    for r in range(NSA_GROUP):
        lanes = slice(r * tq, (r + 1) * tq)
        out_t = jnp.zeros((HEAD_DIM, tq), F32)
        for c, o_c in enumerate((o_cmp, o_sel, o_win)):
            gate = gate_t_ref[pl.ds(c * N_HEADS + g * NSA_GROUP + r, 1), :]
            out_t = out_t + gate * o_c[:, lanes]
        o_ref[:, r * HEAD_DIM:(r + 1) * HEAD_DIM] = jnp.transpose(out_t)


def _nsa_prompt_attention(q, kv4, kwin, gates, kc, vc, batch, seq, nb):
    tq = 256
    tk = 512
    nq = seq // tq
    g4 = NSA_KV_GROUPS
    col = lambda off: pl.BlockSpec((seq, HEAD_DIM), lambda b, g, i: (b, off + g))
    cmp_spec = pl.BlockSpec((None, kc.shape[1], HEAD_DIM), lambda b, g, i: (b, 0, g))
    return pl.pallas_call(
        functools.partial(_nsa_prompt_kernel, nb=nb, tq=tq, tk=tk),
        grid=(batch, g4, nq),
        in_specs=[
            pl.BlockSpec((tq, NSA_KV), lambda b, g, i: (b * nq + i, g)),
            cmp_spec, cmp_spec,
            col(2 * g4), col(3 * g4),
            col(0), col(g4),
            pl.BlockSpec((tq, LANES), lambda b, g, i: (b * nq + i, 0)),
        ],
        out_specs=pl.BlockSpec((tq, NSA_KV), lambda b, g, i: (b * nq + i, g)),
        out_shape=jax.ShapeDtypeStruct(q.shape, F32),
        scratch_shapes=[pltpu.VMEM((1, NSA_GROUP * tq), F32), pltpu.VMEM((1, NSA_GROUP * tq), F32),
                        pltpu.VMEM((HEAD_DIM, NSA_GROUP * tq), F32), pltpu.VMEM((HEAD_DIM, NSA_GROUP * tq), F32),
                        pltpu.VMEM((LANES, tq), F32)],
        compiler_params=_params(("arbitrary", "arbitrary", "arbitrary"), 40),
        name="nsa_prompt_attention",
    )(q, kc, vc, kv4, kv4, kwin, kwin, gates)


def _nsa_sample_select_kernel(q_ref, kc_ref, vc_ref, ocmp_ref, idx_ref, *, nb, qpos):
    nbp = kc_ref.shape[0]
    q = q_ref[...]
    lane_blk = lax.broadcasted_iota(jnp.int32, (1, nbp), 1)
    row_i = lax.broadcasted_iota(jnp.int32, (nbp, nbp), 0)
    col_i = lax.broadcasted_iota(jnp.int32, (nbp, nbp), 1)
    eye = row_i == col_i
    cur = qpos // NSA_BLOCK
    avail = jnp.logical_and((lane_blk + 1) * NSA_BLOCK - 1 <= qpos, lane_blk < nb)
    forced = jnp.logical_or(lane_blk == 0, jnp.logical_or(lane_blk == cur, lane_blk == cur - 1))
    for g in range(NSA_KV_GROUPS):
        qg = jnp.concatenate(
            [q[:, (g * NSA_GROUP + r) * HEAD_DIM:(g * NSA_GROUP + r + 1) * HEAD_DIM] for r in range(NSA_GROUP)],
            axis=0).astype(BF16)
        lanes = slice(g * HEAD_DIM, (g + 1) * HEAD_DIM)
        zc = _dot_nt(qg, kc_ref[:, lanes].astype(BF16)) * ATTN_SCALE
        zm = jnp.where(avail, zc, NEG_INF)
        e = jnp.where(avail, jnp.exp(zm - jnp.max(zm, axis=1, keepdims=True)), 0.0)
        den = jnp.sum(e, axis=1, keepdims=True)
        pc = e / jnp.where(den > 0.0, den, 1.0)
        o = _dot(pc.astype(BF16), vc_ref[:, lanes].astype(BF16))
        for r in range(NSA_GROUP):
            h = g * NSA_GROUP + r
            ocmp_ref[:, h * HEAD_DIM:(h + 1) * HEAD_DIM] = o[r:r + 1, :]
        score = jnp.sum(pc, axis=0, keepdims=True)
        s_row = jnp.where(lane_blk <= cur, jnp.where(forced, FORCE_SCORE, score), -1.0)
        s_mat = jnp.broadcast_to(s_row, (nbp, nbp))
        s_col = jnp.sum(jnp.where(eye, s_mat, 0.0), axis=1, keepdims=True)
        ahead = jnp.logical_or(s_mat > s_col, jnp.logical_and(s_mat == s_col, col_i < row_i))
        rank = jnp.sum(jnp.where(ahead, 1.0, 0.0), axis=1, keepdims=True)
        sel_col = jnp.where(jnp.logical_and(rank < float(min(NSA_TOPN, nb)), s_col >= 0.0), 1.0, 0.0)
        sel_row = jnp.sum(jnp.where(eye, jnp.broadcast_to(sel_col, (nbp, nbp)), 0.0), axis=0, keepdims=True)
        slot = jnp.sum(jnp.where(col_i < row_i, jnp.broadcast_to(sel_row, (nbp, nbp)), 0.0), axis=1, keepdims=True)
        k_i = lax.broadcasted_iota(jnp.int32, (nbp, LANES), 1).astype(F32)
        n_i = lax.broadcasted_iota(jnp.int32, (nbp, LANES), 0).astype(F32)
        hit = jnp.logical_and(sel_col > 0.5, slot == k_i)
        idx_ref[g:g + 1, :] = jnp.sum(jnp.where(hit, n_i, 0.0), axis=0, keepdims=True).astype(jnp.int32)


def _nsa_sample_select(q, kc, vc, nb, qpos):
    bd = q.shape[0]
    nbp = kc.shape[1]
    cmp_spec = pl.BlockSpec((None, nbp, NSA_KV), lambda b: (b, 0, 0))
    ocmp, idx = pl.pallas_call(
        functools.partial(_nsa_sample_select_kernel, nb=nb, qpos=qpos),
        grid=(bd,),
        in_specs=[pl.BlockSpec((None, 1, D_MODEL), lambda b: (b, 0, 0)), cmp_spec, cmp_spec],
        out_specs=[pl.BlockSpec((None, 1, D_MODEL), lambda b: (b, 0, 0)),
                   pl.BlockSpec((None, NSA_KV_GROUPS, LANES), lambda b: (b, 0, 0))],
        out_shape=[jax.ShapeDtypeStruct((bd, 1, D_MODEL), F32),
                   jax.ShapeDtypeStruct((bd, NSA_KV_GROUPS, LANES), jnp.int32)],
        compiler_params=_params(("arbitrary",), 32),
        name="nsa_sample_select",
    )(q.reshape(bd, 1, D_MODEL), kc, vc)
    return ocmp, idx[:, :, :NSA_TOPN]


def _nsa_sample_attend_kernel(idx_ref, pt_ref, q_ref, ocmp_ref, gate_ref, kv_ref, kwin_ref, win_ref, cache_ref,
                              o_ref, kbuf, vbuf, sem, *, n_pages, n_past_blocks):
    b = pl.program_id(0)
    g4 = NSA_KV_GROUPS
    bpp = PAGE_SIZE // NSA_BLOCK

    def block_copies(g, k):
        blk = jnp.minimum(idx_ref[(b * g4 + g) * NSA_TOPN + k], n_past_blocks - 1)
        page = pt_ref[b * n_pages + blk // bpp]
        rows = pl.ds((blk % bpp) * NSA_BLOCK, NSA_BLOCK)
        dst = pl.ds(k * NSA_BLOCK, NSA_BLOCK)
        return (pltpu.make_async_copy(cache_ref.at[page, rows, 2], kbuf.at[g % 2, dst], sem.at[0, g % 2, k]),
                pltpu.make_async_copy(cache_ref.at[page, rows, 3], vbuf.at[g % 2, dst], sem.at[1, g % 2, k]))

    def start_group(g):
        for k in range(NSA_TOPN):
            ck, cv = block_copies(g, k)
            ck.start()
            cv.start()

    def wait_group(g):
        for k in range(NSA_TOPN):
            ck, cv = block_copies(g, k)
            ck.wait()
            cv.wait()

    start_group(0)

    q = q_ref[...]
    ocmp = ocmp_ref[...]
    gates = gate_ref[...]
    kv_new = kv_ref[...]
    kwin_new = kwin_ref[...]
    n_sel = NSA_TOPN * NSA_BLOCK
    key_slot = lax.broadcasted_iota(jnp.int32, (1, n_sel), 1) // NSA_BLOCK
    lane = lax.broadcasted_iota(jnp.int32, (NSA_GROUP, LANES), 1)
    sub = lax.broadcasted_iota(jnp.int32, (NSA_GROUP, LANES), 0)
    bf = lambda x: x.astype(BF16).astype(F32)

    def attend(qg, keys, vals, valid, k_new, v_new, new_on):
        z = _dot_nt(qg, keys.astype(BF16)) * ATTN_SCALE
        z_new = jnp.sum(bf(qg) * bf(k_new), axis=1, keepdims=True) * ATTN_SCALE
        zm = z if valid is None else jnp.where(valid, z, NEG_INF)
        zn = jnp.where(new_on, z_new, NEG_INF)
        m = jnp.maximum(jnp.max(zm, axis=1, keepdims=True), zn)
        p = jnp.exp(zm - m) if valid is None else jnp.where(valid, jnp.exp(zm - m), 0.0)
        p_new = jnp.where(new_on, jnp.exp(zn - m), 0.0)
        den = jnp.sum(p, axis=1, keepdims=True) + p_new
        return (_dot(p.astype(BF16), vals.astype(BF16)) + bf(p_new) * bf(v_new)) / den

    for g in range(g4):
        if g + 1 < g4:
            start_group(g + 1)
        wait_group(g)
        qg = jnp.concatenate(
            [q[:, (g * NSA_GROUP + r) * HEAD_DIM:(g * NSA_GROUP + r + 1) * HEAD_DIM] for r in range(NSA_GROUP)],
            axis=0).astype(BF16)
        o_cmp = jnp.concatenate(
            [ocmp[:, (g * NSA_GROUP + r) * HEAD_DIM:(g * NSA_GROUP + r + 1) * HEAD_DIM] for r in range(NSA_GROUP)],
            axis=0)
        valid = jnp.zeros((1, n_sel), jnp.int32)
        has_new = jnp.int32(0)
        for k in range(NSA_TOPN):
            is_new = (idx_ref[(b * g4 + g) * NSA_TOPN + k] >= n_past_blocks).astype(jnp.int32)
            valid = jnp.where(key_slot == k, 1 - is_new, valid)
            has_new = jnp.maximum(has_new, is_new)
        lanes = lambda slot: slice((slot * g4 + g) * HEAD_DIM, (slot * g4 + g + 1) * HEAD_DIM)
        o_sel = attend(qg, kbuf[g % 2, :, g, :], vbuf[g % 2, :, g, :], valid > 0,
                       kv_new[:, lanes(2)], kv_new[:, lanes(3)], has_new > 0)
        o_win = attend(qg, win_ref[:, 0, g, :], win_ref[:, 1, g, :], None,
                       kwin_new[:, lanes(0)], kwin_new[:, lanes(1)], True)
        out = jnp.zeros((NSA_GROUP, HEAD_DIM), F32)
        for c, o_c in enumerate((o_cmp, o_sel, o_win)):
            pick = lane == c * N_HEADS + g * NSA_GROUP + sub
            gate = jnp.sum(jnp.where(pick, jnp.broadcast_to(gates, (NSA_GROUP, LANES)), 0.0), axis=1, keepdims=True)
            out = out + gate * o_c
        for r in range(NSA_GROUP):
            h = g * NSA_GROUP + r
            o_ref[:, h * HEAD_DIM:(h + 1) * HEAD_DIM] = out[r:r + 1, :]


def _nsa_sample_attend(idx, page_table, q, ocmp, gates, kv4, kwin, win_state, cache_kv):
    bd, n_pages = page_table.shape
    g4 = NSA_KV_GROUPS
    sel_buf = pltpu.VMEM((2, NSA_TOPN * NSA_BLOCK, g4, HEAD_DIM), F32)
    row = lambda w: pl.BlockSpec((None, 1, w), lambda b, *_: (b, 0, 0))
    out = pl.pallas_call(
        functools.partial(_nsa_sample_attend_kernel, n_pages=n_pages, n_past_blocks=n_pages * PAGE_SIZE // NSA_BLOCK),
        grid_spec=pltpu.PrefetchScalarGridSpec(
            num_scalar_prefetch=2,
            grid=(bd,),
            in_specs=[row(D_MODEL), row(D_MODEL), row(LANES), row(4 * NSA_KV), row(2 * NSA_KV),
                      pl.BlockSpec((None,) + win_state.shape[1:], lambda b, *_: (b, 0, 0, 0, 0)),
                      pl.BlockSpec(memory_space=pl.ANY)],
            out_specs=row(D_MODEL),
            scratch_shapes=[sel_buf, sel_buf, pltpu.SemaphoreType.DMA((2, 2, NSA_TOPN))]),
        out_shape=jax.ShapeDtypeStruct((bd, 1, D_MODEL), F32),
        compiler_params=_params(("arbitrary",), 40),
        name="nsa_sample_attend",
    )(idx.reshape(-1), page_table.reshape(-1), q.reshape(bd, 1, D_MODEL), ocmp, gates.reshape(bd, 1, LANES),
      kv4.reshape(bd, 1, 4 * NSA_KV), kwin.reshape(bd, 1, 2 * NSA_KV), win_state, cache_kv)
    return out.reshape(bd, D_MODEL)


def _topk_rows(s, k, ids=None):
    if ids is None:
        ids = lax.broadcasted_iota(jnp.int32, s.shape, 0).astype(F32)
    vals, picked = [], []
    for _ in range(k):
        m = jnp.max(s, axis=0, keepdims=True)
        first = jnp.min(jnp.where(s == m, ids, jnp.inf), axis=0, keepdims=True)
        s = jnp.where(ids == first, -jnp.inf, s)
        vals.append(m)
        picked.append(first)
    return jnp.concatenate(vals, axis=0), jnp.concatenate(picked, axis=0)


def _pair_candidates(v1, v2):
    tokens = v1.shape[1]
    row = lambda n: lax.broadcasted_iota(jnp.int32, (n, tokens), 0).astype(F32)
    vals, ids = [], []
    for k1 in range(4):
        n = 16 if k1 == 0 else 8
        k2 = row(n)
        vals.append(jnp.where((k1 + 1) * (k2 + 1.0) <= PEER_TOPK, v1[k1:k1 + 1, :] + v2[0:n, :], -jnp.inf))
        ids.append(k1 * PEER_TOPK + k2)
    for k2 in range(3):
        n = 16 if k2 == 0 else 8
        k1 = row(n)
        keep = jnp.logical_and(k1 >= 4.0, (k1 + 1.0) * (k2 + 1) <= PEER_TOPK)
        vals.append(jnp.where(keep, v1[0:n, :] + v2[k2:k2 + 1, :], -jnp.inf))
        ids.append(k1 * PEER_TOPK + k2)
    return jnp.concatenate(vals, axis=0), jnp.concatenate(ids, axis=0)


def _pick_rows(sel, table):
    out = jnp.zeros_like(sel)
    for k in range(table.shape[0]):
        out = jnp.where(sel == float(k), table[k:k + 1, :], out)
    return out


def _peer_route_kernel(h_ref, wq_ref, k1_ref, k2_ref, i1_ref, i2_ref, g_ref, qh_ref, i1_t, i2_t, g_t):
    qh_ref[...] = _dot(h_ref[...], wq_ref[...])
    k1 = k1_ref[...].astype(BF16)
    k2 = k2_ref[...].astype(BF16)
    half = PEER_NKEYS

    def head(h, carry):
        c0 = pl.multiple_of(h * 2 * half, 2 * half)
        s1 = _dot_nt(k1, qh_ref[:, pl.ds(c0, half)].astype(BF16))
        s2 = _dot_nt(k2, qh_ref[:, pl.ds(c0 + half, half)].astype(BF16))
        v1, i1 = _topk_rows(s1, PEER_TOPK)
        v2, i2 = _topk_rows(s2, PEER_TOPK)
        cand, cand_id = _pair_candidates(v1, v2)
        sc, j = _topk_rows(cand, PEER_TOPK, cand_id)
        ja = jnp.floor(j * (1.0 / PEER_TOPK))
        jb = j - ja * PEER_TOPK
        e = jnp.exp(sc - jnp.max(sc, axis=0, keepdims=True))
        r0 = pl.multiple_of(h * PEER_TOPK, PEER_TOPK)
        i1_t[pl.ds(r0, PEER_TOPK), :] = _pick_rows(ja, i1)
        i2_t[pl.ds(r0, PEER_TOPK), :] = _pick_rows(jb, i2)
        g_t[pl.ds(r0, PEER_TOPK), :] = e / jnp.sum(e, axis=0, keepdims=True)
        return carry

    lax.fori_loop(0, PEER_HEADS, head, 0)
    i1_ref[...] = jnp.transpose(i1_t[...])
    i2_ref[...] = jnp.transpose(i2_t[...])
    g_ref[...] = jnp.transpose(g_t[...])


def _peer_route(h, wq, k1, k2, tm):
    m, d = h.shape
    tm = min(tm, m)
    hk = PEER_HEADS * PEER_TOPK
    out = pl.BlockSpec((tm, hk), lambda i: (i, 0))
    keys = pl.BlockSpec(k1.shape, lambda i: (0, 0))
    return pl.pallas_call(
        _peer_route_kernel,
        grid=(m // tm,),
        in_specs=[pl.BlockSpec((tm, d), lambda i: (i, 0)), pl.BlockSpec(wq.shape, lambda i: (0, 0)), keys, keys],
        out_specs=[out, out, out],
        out_shape=[jax.ShapeDtypeStruct((m, hk), F32)] * 3,
        scratch_shapes=[pltpu.VMEM((tm, wq.shape[1]), F32)] + [pltpu.VMEM((hk, tm), F32)] * 3,
        compiler_params=_params(("arbitrary",), 40),
        name="peer_route",
    )(h, wq, k1, k2)


PEER_TOKENS_PER_TRIP = 16


def _peer_weights_kernel(i1_ref, i2_ref, g_ref, w_ref):
    tm = i1_ref.shape[0]
    hk = i1_ref.shape[1]
    key = lax.broadcasted_iota(jnp.int32, (PEER_NKEYS, hk), 0).astype(F32)

    def body(step, carry):
        for r in range(PEER_TOKENS_PER_TRIP):
            t = step * PEER_TOKENS_PER_TRIP + r
            i1 = jnp.broadcast_to(i1_ref[pl.ds(t, 1), :], (PEER_NKEYS, hk))
            i2 = jnp.broadcast_to(i2_ref[pl.ds(t, 1), :], (PEER_NKEYS, hk))
            gt = jnp.broadcast_to(g_ref[pl.ds(t, 1), :], (PEER_NKEYS, hk))
            p = jnp.where(i1 == key, 1.0, 0.0).astype(BF16)
            q = jnp.where(i2 == key, gt, 0.0).astype(BF16)
            w_ref[t] = _dot_nt(p, q)
        return carry

    lax.fori_loop(0, tm // PEER_TOKENS_PER_TRIP, body, 0)


def _peer_weights(i1, i2, g, tm):
    m, hk = i1.shape
    tm = min(tm, m)
    spec = pl.BlockSpec((tm, hk), lambda i: (i, 0))
    return pl.pallas_call(
        _peer_weights_kernel,
        grid=(m // tm,),
        in_specs=[spec, spec, spec],
        out_specs=pl.BlockSpec((tm, PEER_NKEYS, PEER_NKEYS), lambda i: (i, 0, 0)),
        out_shape=jax.ShapeDtypeStruct((m, PEER_NKEYS, PEER_NKEYS), F32),
        compiler_params=_params(("arbitrary",), 40),
        name="peer_weights",
    )(i1, i2, g)


def _peer_dense_kernel(h_ref, u_ref, v_ref, x_ref, gate_ref, lng_ref, lnb_ref, scale_ref, shift_ref, w_hbm,
                       xo_ref, ho_ref, acc_ref, wbuf, wsem, *, n_a):
    i = pl.program_id(0)
    c = pl.program_id(1)
    nc = pl.num_programs(1)
    step = i * nc + c
    tm = h_ref.shape[0]

    def map_copies(s):
        rows = pl.ds((s // nc) * tm, tm)
        return [pltpu.make_async_copy(w_hbm.at[rows, (s % nc) * n_a + k, :], wbuf.at[s % 2, k], wsem.at[s % 2, k])
                for k in range(n_a)]

    @pl.when(step == 0)
    def _():
        for cp in map_copies(step):
            cp.start()

    @pl.when(step + 1 < pl.num_programs(0) * nc)
    def _():
        for cp in map_copies(step + 1):
            cp.start()

    @pl.when(c == 0)
    def _():
        acc_ref[...] = jnp.zeros_like(acc_ref)

    act = _gelu(_dot_nt(h_ref[...], u_ref[...]))
    for cp in map_copies(step):
        cp.wait()
    weighted = [act[:, k * PEER_NKEYS:(k + 1) * PEER_NKEYS] * wbuf[step % 2, k] for k in range(n_a)]
    acc_ref[...] += _dot(jnp.concatenate(weighted, axis=1).astype(BF16), v_ref[...])

    @pl.when(c == pl.num_programs(1) - 1)
    def _():
        xn, h = _ln_res(acc_ref[...], x_ref[...], gate_ref[...], lng_ref[...], lnb_ref[...],
                        scale_ref[...], shift_ref[...])
        xo_ref[...] = xn
        ho_ref[...] = h


def _peer_dense(h, u, v, layer, w, xres, gate, lng, lnb, scale, shift, tm, te):
    m, d = h.shape
    n_exp = u.shape[1]
    tm = min(tm, m)
    bpb = (m // tm) // gate.shape[0]
    n_a = te // PEER_NKEYS
    row = pl.BlockSpec((tm, d), lambda i, c: (i, 0))
    tab = pl.BlockSpec((None, te, d), lambda i, c: (layer, c, 0))
    vec = pl.BlockSpec((1, d), lambda i, c: (0, 0))
    return pl.pallas_call(
        functools.partial(_peer_dense_kernel, n_a=n_a),
        grid=(m // tm, n_exp // te),
        in_specs=[row, tab, tab, row, _mod_spec(gate, bpb), vec, vec, _mod_spec(scale, bpb), _mod_spec(shift, bpb),
                  pl.BlockSpec(memory_space=pl.ANY)],
        out_specs=[row, row],
        out_shape=[jax.ShapeDtypeStruct((m, d), F32), jax.ShapeDtypeStruct((m, d), BF16)],
        scratch_shapes=[pltpu.VMEM((tm, d), F32), pltpu.VMEM((2, n_a, tm, PEER_NKEYS), F32),
                        pltpu.SemaphoreType.DMA((2, n_a))],
        compiler_params=_params(("arbitrary", "arbitrary"), 56),
        name="peer_dense",
    )(h, u, v, xres, gate, lng, lnb, scale, shift, w)


def _peer(h, xres, wq, k1, k2, u, v, layer, gate, lng, lnb, scale, shift, tm_route, tm_dense, te):
    i1, i2, g = _peer_route(h, wq, k1, k2, tm_route)
    w = _peer_weights(i1, i2, g, 128)
    return _peer_dense(h, u, v, layer, w, xres, gate, lng, lnb, scale, shift, tm_dense, te)


def _rope_tables(pos):
    half = ROPE_DIMS // 2
    inv = jnp.power(ROPE_THETA, -jnp.arange(half, dtype=F32) / half)
    ang = pos.astype(F32)[:, None] * inv
    cos, sin = jnp.cos(ang), jnp.sin(ang)
    n = pos.shape[0]
    ones = jnp.ones((n, HEAD_DIM - ROPE_DIMS), F32)
    zeros = jnp.zeros((n, HEAD_DIM - ROPE_DIMS), F32)
    zh = jnp.zeros((n, half), F32)
    return (jnp.concatenate([cos, cos, ones], axis=1),
            jnp.concatenate([-sin, zh, zeros], axis=1),
            jnp.concatenate([zh, sin, zeros], axis=1))


def _pad_rows(x, rows):
    return jnp.pad(x, ((0, rows - x.shape[0]),) + ((0, 0),) * (x.ndim - 1))


def kernel(x_prompt, x_sample, c_prompt, c_sample, cache_l0_kv, cache_l1_kv, state_l1_win, cache_l2_kv, cache_l3_kv,
           state_l3_win, page_table, ada_w, ada_b, ln1_g, ln1_b, ln2_g, ln2_b, sb_w_in, sb_w_o, nsa_w_in, nsa_w_o,
           nsa_cmp_pe, nsa_cmp_w1, nsa_cmp_b1, nsa_cmp_w2, nsa_cmp_b2, peer_wq, peer_k1, peer_k2, peer_u, peer_v):
    batch, seq, d = x_prompt.shape
    bd, dec_seq, _ = x_sample.shape
    assert dec_seq == 1 and d == D_MODEL
    depth = ada_w.shape[0]
    n_pages = page_table.shape[1]
    past = n_pages * PAGE_SIZE
    sb_caches = (cache_l0_kv, cache_l2_kv)
    nsa_caches = (cache_l1_kv, cache_l3_kv)
    nsa_wins = (state_l1_win, state_l3_win)
    win_buf = state_l1_win.shape[1]
    assert win_buf == NSA_WINDOW and past % NSA_BLOCK == 0 and seq >= win_buf
    mp_rows = batch * seq
    tm = 512

    n_c = batch + bd
    c_all = _pad_rows(jnp.concatenate([c_prompt, c_sample], axis=0), -(-n_c // 8) * 8)
    mod = _modulation(c_all, ada_w, ada_b).reshape(depth, c_all.shape[0], 6, d)
    mod_p = lambda i, k: mod[i, :batch, k].reshape(batch, 1, d)
    mod_s = lambda i, k: mod[i, batch:n_c, k].reshape(1, bd, d)

    xp = x_prompt.reshape(mp_rows, d)
    xs = x_sample.reshape(bd, d)
    hp = _modulate(xp, mod_p(0, 1), mod_p(0, 0), tm)
    hs = _modulate(xs, mod_s(0, 1), mod_s(0, 0), bd)

    rope_p = _rope_tables(jnp.arange(seq))
    rope_s = _rope_tables(jnp.full((bd,), past))
    nb_p = -(-seq // NSA_BLOCK)
    assert nb_p * NSA_BLOCK == seq
    nb_s = past // NSA_BLOCK + 1

    u_all = peer_u.astype(BF16)
    v_all = peer_v.astype(BF16)
    new = []
    for i in range(depth):
        j = i // 2
        vec = lambda a: a[i].reshape(1, d)
        if i % 2 == 0:
            w_in = sb_w_in[j].astype(BF16)
            w_o = sb_w_o[j].astype(BF16)
            sb_rows = (2, N_HEADS)
            q_p = _matmul(hp, w_in[:, :d], tm, 512, out_dtype=BF16, name="sb_q")
            kv_p, kv_rows_p = _matmul(hp, w_in[:, d:], tm, 1024, out_dtype=BF16, rows_out=sb_rows, name="sb_kv")
            o_p = _sb_prompt_attention(q_p, kv_p, batch, seq)
            q_s = _matmul(hs, w_in[:, :d], bd, 512, out_dtype=BF16, name="sb_q")
            _, kv_rows_s = _matmul(hs, w_in[:, d:], bd, 1024, out_dtype=BF16, rows_out=sb_rows, name="sb_kv")
            o_s = _sb_sample_attention(q_s, sb_caches[j], page_table)
            new.append((kv_rows_p.reshape(batch, seq, 2, N_HEADS, HEAD_DIM),
                        kv_rows_s.reshape(bd, 1, 2, N_HEADS, HEAD_DIM)))
        else:
            w_in = nsa_w_in[j].astype(BF16)
            w_o = nsa_w_o[j].astype(BF16)
            w_q = w_in[:, :d]
            w_kv4 = w_in[:, d:d + 4 * NSA_KV]
            w_kw = w_in[:, d + 4 * NSA_KV:d + 6 * NSA_KV]
            w_g = jnp.pad(w_in[:, d + 6 * NSA_KV:], ((0, 0), (0, LANES - 3 * N_HEADS)))
            cmp_w = (nsa_cmp_pe[j], nsa_cmp_w1[j].astype(BF16), nsa_cmp_b1[j], nsa_cmp_w2[j], nsa_cmp_b2[j])

            def project(h, rope, rows):
                kv4, kv4_rows = _matmul(h, w_kv4, rows, NSA_KV, epilogue="rope_even", rope=rope,
                                        rows_out=(4, NSA_KV_GROUPS), name="nsa_kv4")
                kwin, kwin_rows = _matmul(h, w_kw, rows, NSA_KV, epilogue="rope_even", rope=rope,
                                          rows_out=(2, NSA_KV_GROUPS), name="nsa_kwin")
                return (_matmul(h, w_q, rows, 512, epilogue="rope_all", rope=rope, name="nsa_q"), kv4, kwin,
                        _matmul(h, w_g, rows, LANES, epilogue="sigmoid", name="nsa_gates"), kv4_rows, kwin_rows)

            q_p, kv4_p, kwin_p, gates_p, kv4_rows_p, kwin_rows_p = project(hp, rope_p, tm)
            cmp_p = _compress(kv4_p.reshape(batch * nb_p, NSA_BLOCK, 4 * NSA_KV), *cmp_w, 256)
            nbp = -(-nb_p // LANES) * LANES
            cmp_p = jnp.pad(cmp_p.reshape(2, batch, nb_p, NSA_KV), ((0, 0), (0, 0), (0, nbp - nb_p), (0, 0)))
            o_p = _nsa_prompt_attention(q_p, kv4_p, kwin_p, gates_p, cmp_p[0], cmp_p[1], batch, seq, nb_p)
            win_p = kwin_rows_p.reshape(batch, seq, 2, NSA_KV_GROUPS, HEAD_DIM)[:, seq - win_buf:]
            q_s, kv4_s, kwin_s, gates_s, kv4_rows_s, kwin_rows_s = project(hs, rope_s, bd)
            x_past = _gather_cmp_pages(nsa_caches[j], page_table)
            cmp_past = _compress(x_past.reshape(bd * (nb_s - 1), NSA_BLOCK, 2 * NSA_KV), *cmp_w, 256)
            x_last = jnp.pad(kv4_s[:, None, :2 * NSA_KV], ((0, 0), (0, NSA_BLOCK - 1), (0, 0)))
            cmp_last = _compress(x_last, *cmp_w, bd)
            nbs = -(-nb_s // LANES) * LANES
            cmp_s = jnp.concatenate([cmp_past.reshape(2, bd, nb_s - 1, NSA_KV), cmp_last.reshape(2, bd, 1, NSA_KV)], 2)
            cmp_s = jnp.pad(cmp_s, ((0, 0), (0, 0), (0, nbs - nb_s), (0, 0)))
            ocmp_s, idx_s = _nsa_sample_select(q_s, cmp_s[0], cmp_s[1], nb_s, past)
            o_s = _nsa_sample_attend(idx_s, page_table, q_s, ocmp_s, gates_s, kv4_s, kwin_s, nsa_wins[j], nsa_caches[j])
            win_s = jnp.concatenate([nsa_wins[j][:, 1:], kwin_rows_s.reshape(bd, 1, 2, NSA_KV_GROUPS, HEAD_DIM)], axis=1)
            new.append((kv4_rows_p.reshape(batch, seq, 4, NSA_KV_GROUPS, HEAD_DIM),
                        kv4_rows_s.reshape(bd, 1, 4, NSA_KV_GROUPS, HEAD_DIM), win_p, win_s))

        xp, hp = _matmul_ln(o_p, w_o, xp, mod_p(i, 2), vec(ln1_g), vec(ln1_b), mod_p(i, 4), mod_p(i, 3), tm)
        xs, hs = _matmul_ln(o_s, w_o, xs, mod_s(i, 2), vec(ln1_g), vec(ln1_b), mod_s(i, 4), mod_s(i, 3), bd)

        nxt = min(i + 1, depth - 1)
        wq = peer_wq[i].astype(BF16)
        xp, hp = _peer(hp, xp, wq, peer_k1[i], peer_k2[i], u_all, v_all, i, mod_p(i, 5), vec(ln2_g), vec(ln2_b),
                       mod_p(nxt, 1), mod_p(nxt, 0), 256, 512, 1024)
        pad = LANES
        hs_pad = _pad_rows(hs, pad)
        gate_s = _pad_rows(mod_s(i, 5)[0], pad)[None]
        scale_s = _pad_rows(mod_s(nxt, 1)[0], pad)[None]
        shift_s = _pad_rows(mod_s(nxt, 0)[0], pad)[None]
        xs_pad, hs_pad = _peer(hs_pad, _pad_rows(xs, pad), wq, peer_k1[i], peer_k2[i], u_all, v_all, i, gate_s,
                               vec(ln2_g), vec(ln2_b), scale_s, shift_s, pad, pad, 512)
        xs, hs = xs_pad[:bd], hs_pad[:bd]

    return (xp.reshape(batch, seq, d), xs.reshape(bd, 1, d), new[0][0], new[0][1], new[1][0], new[1][1], new[1][2],
            new[1][3], new[2][0], new[2][1], new[3][0], new[3][1], new[3][2], new[3][3])
```

```python
import functools

import jax
import jax.numpy as jnp
import numpy as np
from jax import lax
from jax.experimental import pallas as pl
from jax.experimental.pallas import tpu as pltpu

F32 = jnp.float32
BF16 = jnp.bfloat16

D_MODEL = 2048
HEAD_DIM = 128
N_HEADS = D_MODEL // HEAD_DIM
PAGE_SIZE = 128
NSA_KV_GROUPS = 4
NSA_GROUP = N_HEADS // NSA_KV_GROUPS
NSA_KV = NSA_KV_GROUPS * HEAD_DIM
NSA_BLOCK = 64
NSA_TOPN = 16
NSA_WINDOW = 512
ROPE_THETA = 500000.0
ROPE_DIMS = HEAD_DIM // 4
PEER_HEADS = 8
PEER_NKEYS = 128
PEER_TOPK = 16
DEPTH = 4
DEEPNORM_ALPHA = (2 * DEPTH) ** 0.25
LN_EPS = 1e-5
NEG_INF = -1e30
FORCE_SCORE = 1e9
ATTN_SCALE = HEAD_DIM ** -0.5
SQRT_HALF = float(np.sqrt(0.5))

LANES = 128
VMEM_LIMIT_CAP = 56 * 1024 * 1024
SB_LOG_UNDERFLOW = -104.0


def _params(sem, vmem_mb):
    return pltpu.CompilerParams(
        dimension_semantics=sem, vmem_limit_bytes=min(vmem_mb * 1024 * 1024, VMEM_LIMIT_CAP))


def _dot(a, b):
    return jnp.dot(a, b, preferred_element_type=F32)


def _dot_nt(a, b):
    return lax.dot_general(a, b, (((1,), (1,)), ((), ())), preferred_element_type=F32)


def _gelu(x):
    return 0.5 * x * (1.0 + lax.erf(x * SQRT_HALF))


def _log_sigmoid_pair(z):
    l1p = jnp.log(1.0 + jnp.exp(-jnp.abs(z)))
    return jnp.minimum(z, 0.0) - l1p, jnp.minimum(-z, 0.0) - l1p


def _ln_res(y, xres, gate, lng, lnb, scale, shift):
    v = DEEPNORM_ALPHA * xres + gate * y
    mu = jnp.mean(v, axis=-1, keepdims=True)
    d = v - mu
    var = jnp.mean(d * d, axis=-1, keepdims=True)
    xn = d * lax.rsqrt(var + LN_EPS) * lng + lnb
    return xn, (xn * (1.0 + scale) + shift).astype(BF16)


def _mod_spec(arr, blocks_per_batch):
    return pl.BlockSpec((None,) + arr.shape[1:], lambda i, *_: (i // blocks_per_batch, 0, 0))


def _modulation_kernel(c_ref, w_ref, b_ref, o_ref):
    c = c_ref[...]
    s = (c * jax.nn.sigmoid(c)).astype(BF16)
    o_ref[...] = _dot(s, w_ref[...].astype(BF16)) + b_ref[...]


def _modulation(c_all, ada_w, ada_b):
    depth, d, n = ada_w.shape
    r = c_all.shape[0]
    tn = 1024
    return pl.pallas_call(
        _modulation_kernel,
        grid=(depth, n // tn),
        in_specs=[
            pl.BlockSpec((r, d), lambda l, j: (0, 0)),
            pl.BlockSpec((None, d, tn), lambda l, j: (l, 0, j)),
            pl.BlockSpec((None, 1, tn), lambda l, j: (l, 0, j)),
        ],
        out_specs=pl.BlockSpec((None, r, tn), lambda l, j: (l, 0, j)),
        out_shape=jax.ShapeDtypeStruct((depth, r, n), F32),
        compiler_params=_params(("arbitrary", "arbitrary"), 32),
        name="modulation",
    )(c_all, ada_w, ada_b.reshape(depth, 1, n))


def _modulate_kernel(x_ref, scale_ref, shift_ref, o_ref):
    o_ref[...] = (x_ref[...] * (1.0 + scale_ref[...]) + shift_ref[...]).astype(BF16)


def _modulate(x, scale, shift, tm):
    m, d = x.shape
    bpb = (m // tm) // scale.shape[0]
    return pl.pallas_call(
        _modulate_kernel,
        grid=(m // tm,),
        in_specs=[pl.BlockSpec((tm, d), lambda i: (i, 0)), _mod_spec(scale, bpb), _mod_spec(shift, bpb)],
        out_specs=pl.BlockSpec((tm, d), lambda i: (i, 0)),
        out_shape=jax.ShapeDtypeStruct((m, d), BF16),
        compiler_params=_params(("arbitrary",), 32),
        name="modulate",
    )(x, scale, shift)


def _store_tile(y, o_refs):
    o_refs[0][...] = y.astype(o_refs[0].dtype)
    if len(o_refs) > 1:
        for h in range(o_refs[1].shape[1]):
            o_refs[1][:, h, :] = y[:, h * HEAD_DIM:(h + 1) * HEAD_DIM]


def _mm_kernel(a_ref, w_ref, *o_refs):
    _store_tile(_dot(a_ref[...], w_ref[...]), o_refs)


def _mm_sigmoid_kernel(a_ref, w_ref, o_ref):
    o_ref[...] = jax.nn.sigmoid(_dot(a_ref[...], w_ref[...]))


def _mm_rope_kernel(a_ref, w_ref, cos_ref, sa_ref, sb_ref, *o_refs, even_only):
    acc = _dot(a_ref[...], w_ref[...])
    tn = acc.shape[1]
    reps = tn // HEAD_DIM
    cos = jnp.concatenate([cos_ref[...]] * reps, axis=1)
    sa = jnp.concatenate([sa_ref[...]] * reps, axis=1)
    sb = jnp.concatenate([sb_ref[...]] * reps, axis=1)
    half = ROPE_DIMS // 2
    rot = acc * cos + pltpu.roll(acc, tn - half, 1) * sa + pltpu.roll(acc, half, 1) * sb
    if even_only:
        rot = jnp.where(pl.program_id(1) % 2 == 0, rot, acc)
    _store_tile(rot, o_refs)


def _matmul(a, w, tm, tn, *, epilogue=None, rope=None, out_dtype=F32, rows_out=None, name="matmul"):
    m, k = a.shape
    n = w.shape[1]
    tm = min(tm, m)
    in_specs = [pl.BlockSpec((tm, k), lambda i, j: (i, 0)), pl.BlockSpec((k, tn), lambda i, j: (0, j))]
    args = [a, w]
    if epilogue in ("rope_all", "rope_even"):
        assert epilogue == "rope_all" or tn == NSA_KV
        t_tab = rope[0].shape[0]
        nt = t_tab // tm
        in_specs += [pl.BlockSpec((tm, HEAD_DIM), lambda i, j: (i % nt, 0))] * 3
        args += list(rope)
        kern = functools.partial(_mm_rope_kernel, even_only=(epilogue == "rope_even"))
    elif epilogue == "sigmoid":
        kern = _mm_sigmoid_kernel
    else:
        kern = _mm_kernel
    out_specs = pl.BlockSpec((tm, tn), lambda i, j: (i, j))
    out_shape = jax.ShapeDtypeStruct((m, n), out_dtype)
    if rows_out is not None:
        slots, heads = rows_out
        tiles_per_slot = heads * HEAD_DIM // tn
        out_specs = [out_specs, pl.BlockSpec((tm, None, heads // tiles_per_slot, HEAD_DIM),
                                             lambda i, j: (i, j // tiles_per_slot, j % tiles_per_slot, 0))]
        out_shape = [out_shape, jax.ShapeDtypeStruct((m, slots, heads, HEAD_DIM), F32)]
    return pl.pallas_call(
        kern,
        grid=(m // tm, n // tn),
        in_specs=in_specs,
        out_specs=out_specs,
        out_shape=out_shape,
        compiler_params=_params(("arbitrary", "arbitrary"), 40),
        name=name,
    )(*args)


def _mm_ln_kernel(a_ref, w_ref, x_ref, gate_ref, lng_ref, lnb_ref, scale_ref, shift_ref, xo_ref, ho_ref):
    y = _dot(a_ref[...].astype(BF16), w_ref[...])
    xn, h = _ln_res(y, x_ref[...], gate_ref[...], lng_ref[...], lnb_ref[...], scale_ref[...], shift_ref[...])
    xo_ref[...] = xn
    ho_ref[...] = h


def _matmul_ln(a, w, xres, gate, lng, lnb, scale, shift, tm):
    m, k = a.shape
    d = w.shape[1]
    tm = min(tm, m)
    bpb = (m // tm) // gate.shape[0]
    row = pl.BlockSpec((tm, d), lambda i: (i, 0))
    vec = pl.BlockSpec((1, d), lambda i: (0, 0))
    return pl.pallas_call(
        _mm_ln_kernel,
        grid=(m // tm,),
        in_specs=[pl.BlockSpec((tm, k), lambda i: (i, 0)), pl.BlockSpec((k, d), lambda i: (0, 0)), row,
                  _mod_spec(gate, bpb), vec, vec, _mod_spec(scale, bpb), _mod_spec(shift, bpb)],
        out_specs=[row, row],
        out_shape=[jax.ShapeDtypeStruct((m, d), F32), jax.ShapeDtypeStruct((m, d), BF16)],
        compiler_params=_params(("arbitrary",), 48),
        name="matmul_ln",
    )(a, w, xres, gate, lng, lnb, scale, shift)


def _suffix_matrix(n):
    return (lax.broadcasted_iota(jnp.int32, (n, n), 0) > lax.broadcasted_iota(jnp.int32, (n, n), 1)).astype(BF16)


def _suffix_sum(x, lmat):
    hi = x.astype(BF16)
    lo = (x - hi.astype(F32)).astype(BF16)
    return _dot(hi, lmat) + _dot(lo, lmat)


SB_HEADS_PER_STEP = 8


def _sb_prompt_kernel(q_ref, k_ref, v_ref, o_ref, carry_ref, acc_ref, *, tq, tk):
    i = pl.program_id(2)
    carry_ref[...] = jnp.zeros_like(carry_ref)
    acc_ref[...] = jnp.zeros_like(acc_ref)
    qpos = i * tq + lax.broadcasted_iota(jnp.int32, (tq, 1), 0)
    lmat = jnp.concatenate([_suffix_matrix(tk), jnp.ones((tk, tk), BF16)], axis=1)

    def body(state):
        j, _ = state
        k0 = pl.multiple_of(j * tk, tk)
        allowed = (k0 + lax.broadcasted_iota(jnp.int32, (1, tk), 1)) < qpos
        heads = range(SB_HEADS_PER_STEP)
        lanes = [slice(hh * HEAD_DIM, (hh + 1) * HEAD_DIM) for hh in heads]
        zs = [_dot_nt(q_ref[:, lanes[hh]].astype(BF16), k_ref[pl.ds(k0, tk), lanes[hh]].astype(BF16)) * ATTN_SCALE
              for hh in heads]
        pairs = [_log_sigmoid_pair(z) for z in zs]
        sums = [_suffix_sum(jnp.where(allowed, lsn, 0.0), lmat) for _, lsn in pairs]
        carries = [carry_ref[hh] for hh in heads]
        probs = [jnp.where(allowed, jnp.exp(pairs[hh][0] + sums[hh][:, :tk] + carries[hh]), 0.0).astype(BF16)
                 for hh in heads]
        pvs = [_dot(probs[hh], v_ref[pl.ds(k0, tk), lanes[hh]].astype(BF16)) for hh in heads]
        top = jnp.full((tq, tk), -jnp.inf, F32)
        for hh in heads:
            acc_ref[:, lanes[hh]] += pvs[hh]
            carry = carries[hh] + sums[hh][:, tk:]
            carry_ref[hh] = carry
            top = jnp.maximum(top, carry)
        return j - 1, (jnp.max(top) > SB_LOG_UNDERFLOW).astype(jnp.int32)

    lax.while_loop(lambda s: jnp.logical_and(s[0] >= 0, s[1] > 0), body, ((i * tq + tq - 1) // tk, jnp.int32(1)))
    o_ref[...] = acc_ref[...]


def _sb_prompt_attention(q, kv, batch, seq, tq=256, tk=128):
    nq = seq // tq
    hb = SB_HEADS_PER_STEP
    width = hb * HEAD_DIM
    n_hb = N_HEADS // hb
    return pl.pallas_call(
        functools.partial(_sb_prompt_kernel, tq=tq, tk=tk),
        grid=(batch, n_hb, nq),
        in_specs=[
            pl.BlockSpec((tq, width), lambda b, h, i: (b * nq + i, h)),
            pl.BlockSpec((seq, width), lambda b, h, i: (b, h)),
            pl.BlockSpec((seq, width), lambda b, h, i: (b, n_hb + h)),
        ],
        out_specs=pl.BlockSpec((tq, width), lambda b, h, i: (b * nq + i, h)),
        out_shape=jax.ShapeDtypeStruct(q.shape, F32),
        scratch_shapes=[pltpu.VMEM((hb, tq, tk), F32), pltpu.VMEM((tq, width), F32)],
        compiler_params=_params(("arbitrary", "arbitrary", "arbitrary"), 48),
        name="sb_prompt_attention",
    )(q, kv, kv)


SB_ROW_REP = 8


def _sb_sample_kernel(pt_ref, q_ref, cache_ref, o_ref, kbuf, vbuf, sem, carry_ref, acc_ref, *, n_pages):
    b = pl.program_id(0)
    rep = SB_ROW_REP
    q = q_ref[...]
    qh = [jnp.broadcast_to(q[:, h * HEAD_DIM:(h + 1) * HEAD_DIM], (rep, HEAD_DIM)).astype(BF16)
          for h in range(N_HEADS)]
    lmat = _suffix_matrix(PAGE_SIZE)
    carry_ref[...] = jnp.zeros_like(carry_ref)
    acc_ref[...] = jnp.zeros_like(acc_ref)

    def page_copies(page):
        return (pltpu.make_async_copy(cache_ref.at[page, :, 0], kbuf, sem.at[0]),
                pltpu.make_async_copy(cache_ref.at[page, :, 1], vbuf, sem.at[1]))

    def body(state):
        p, _ = state
        ck, cv = page_copies(pt_ref[b * n_pages + p])
        ck.start()
        cv.start()
        ck.wait()
        cv.wait()
        z = jnp.concatenate([_dot_nt(qh[h], kbuf[:, h, :].astype(BF16)) for h in range(N_HEADS)], axis=0)
        ls, lsn = _log_sigmoid_pair(z * ATTN_SCALE)
        later = _suffix_sum(lsn, lmat) + carry_ref[...]
        a = jnp.exp(ls + later)
        pv = [_dot(a[h * rep:(h + 1) * rep, :].astype(BF16), vbuf[:, h, :].astype(BF16)) for h in range(N_HEADS)]
        acc_ref[...] += jnp.concatenate(pv, axis=0)
        carry = carry_ref[...] + jnp.sum(lsn, axis=1, keepdims=True)
        carry_ref[...] = carry
        return p - 1, (jnp.max(carry) > SB_LOG_UNDERFLOW).astype(jnp.int32)

    lax.while_loop(lambda s: jnp.logical_and(s[0] >= 0, s[1] > 0), body, (jnp.int32(n_pages - 1), jnp.int32(1)))
    for h in range(N_HEADS):
        o_ref[:, h * HEAD_DIM:(h + 1) * HEAD_DIM] = acc_ref[h * rep:h * rep + 1, :]


def _sb_sample_attention(q, cache_kv, page_table):
    bd, n_pages = page_table.shape
    page_buf = pltpu.VMEM((PAGE_SIZE, N_HEADS, HEAD_DIM), F32)
    out = pl.pallas_call(
        functools.partial(_sb_sample_kernel, n_pages=n_pages),
        grid_spec=pltpu.PrefetchScalarGridSpec(
            num_scalar_prefetch=1,
            grid=(bd,),
            in_specs=[pl.BlockSpec((None, 1, D_MODEL), lambda b, pt: (b, 0, 0)),
                      pl.BlockSpec(memory_space=pl.ANY)],
            out_specs=pl.BlockSpec((None, 1, D_MODEL), lambda b, pt: (b, 0, 0)),
            scratch_shapes=[page_buf, page_buf, pltpu.SemaphoreType.DMA((2,)),
                            pltpu.VMEM((N_HEADS * SB_ROW_REP, 1), F32),
                            pltpu.VMEM((N_HEADS * SB_ROW_REP, HEAD_DIM), F32)]),
        out_shape=jax.ShapeDtypeStruct((bd, 1, D_MODEL), F32),
        compiler_params=_params(("arbitrary",), 16),
        name="sb_sample_attention",
    )(page_table.reshape(-1), q.reshape(bd, 1, D_MODEL), cache_kv)
    return out.reshape(bd, D_MODEL)


def _compress_kernel(x_hbm, pe_ref, w1_ref, b1_ref, w2_ref, b2_ref, o_ref, xbuf, sem, *, tr):
    n_i = pl.num_programs(2)
    step = (pl.program_id(0) * pl.num_programs(1) + pl.program_id(1)) * n_i + pl.program_id(2)
    total = pl.num_programs(0) * pl.num_programs(1) * n_i

    def block_copies(s):
        rows = pl.ds((s % n_i) * tr, tr)
        lanes = pl.ds((s // n_i) * HEAD_DIM, HEAD_DIM)
        return [pltpu.make_async_copy(x_hbm.at[rows, t, lanes], xbuf.at[s % 2, t], sem.at[s % 2, t])
                for t in range(NSA_BLOCK)]

    @pl.when(step == 0)
    def _():
        for cp in block_copies(step):
            cp.start()

    @pl.when(step + 1 < total)
    def _():
        for cp in block_copies(step + 1):
            cp.start()

    for cp in block_copies(step):
        cp.wait()
    acc = jnp.zeros((tr, w1_ref.shape[1]), F32)
    for t in range(NSA_BLOCK):
        xt = (xbuf[step % 2, t] + pe_ref[t:t + 1, :]).astype(BF16)
        acc = acc + _dot(xt, w1_ref[t * HEAD_DIM:(t + 1) * HEAD_DIM, :])
    hid = _gelu(acc + b1_ref[...])
    o_ref[...] = _dot(hid.astype(BF16), w2_ref[...].astype(BF16)) + b2_ref[...]


def _compress(x, pe, w1, b1, w2, b2, tr):
    r = x.shape[0]
    tr = min(tr, r)
    hidden = w1.shape[2]
    return pl.pallas_call(
        functools.partial(_compress_kernel, tr=tr),
        grid=(2, NSA_KV_GROUPS, r // tr),
        in_specs=[
            pl.BlockSpec(memory_space=pl.ANY),
            pl.BlockSpec((None, NSA_BLOCK, HEAD_DIM), lambda s, g, i: (s, 0, 0)),
            pl.BlockSpec((None, NSA_BLOCK * HEAD_DIM, hidden), lambda s, g, i: (s, 0, 0)),
            pl.BlockSpec((None, 1, hidden), lambda s, g, i: (s, 0, 0)),
            pl.BlockSpec((None, hidden, HEAD_DIM), lambda s, g, i: (s, 0, 0)),
            pl.BlockSpec((None, 1, HEAD_DIM), lambda s, g, i: (s, 0, 0)),
        ],
        out_specs=pl.BlockSpec((None, tr, HEAD_DIM), lambda s, g, i: (s, i, g)),
        out_shape=jax.ShapeDtypeStruct((2, r, NSA_KV), F32),
        scratch_shapes=[pltpu.VMEM((2, NSA_BLOCK, tr, HEAD_DIM), F32), pltpu.SemaphoreType.DMA((2, NSA_BLOCK))],
        compiler_params=_params(("arbitrary", "arbitrary", "arbitrary"), 48),
        name="nsa_compress",
    )(x, pe, w1, b1.reshape(2, 1, hidden), w2, b2.reshape(2, 1, HEAD_DIM))


def _page_gather_kernel(pt_ref, *refs):
    o_ref = refs[-1]
    for k, r in enumerate(refs[:-1]):
        for s in range(2):
            for g in range(NSA_KV_GROUPS):
                lanes = slice((s * NSA_KV_GROUPS + g) * HEAD_DIM, (s * NSA_KV_GROUPS + g + 1) * HEAD_DIM)
                o_ref[k * PAGE_SIZE:(k + 1) * PAGE_SIZE, lanes] = r[:, s, g, :]


def _gather_cmp_pages(cache_kv, page_table, pages_per_step=8):
    bd, n_pages = page_table.shape
    width = 2 * NSA_KV
    pps = pages_per_step

    def in_map(k):
        return lambda b, c, pt: (pt[b * n_pages + c * pps + k], 0, 0, 0, 0)

    return pl.pallas_call(
        _page_gather_kernel,
        grid_spec=pltpu.PrefetchScalarGridSpec(
            num_scalar_prefetch=1,
            grid=(bd, n_pages // pps),
            in_specs=[pl.BlockSpec((None, PAGE_SIZE, 2, NSA_KV_GROUPS, HEAD_DIM), in_map(k)) for k in range(pps)],
            out_specs=pl.BlockSpec((None, pps * PAGE_SIZE, width), lambda b, c, pt: (b, c, 0))),
        out_shape=jax.ShapeDtypeStruct((bd, n_pages * PAGE_SIZE, width), F32),
        compiler_params=_params(("arbitrary", "arbitrary"), 40),
        name="nsa_page_gather",
    )(page_table.reshape(-1), *([cache_kv] * pps))


def _online_softmax_step(z, v_t, m_ref, l_ref, acc_ref):
    m_new = jnp.maximum(m_ref[...], jnp.max(z, axis=0, keepdims=True))
    alpha = jnp.exp(m_ref[...] - m_new)
    p = jnp.exp(z - m_new)
    l_ref[...] = alpha * l_ref[...] + jnp.sum(p, axis=0, keepdims=True)
    acc_ref[...] = alpha * acc_ref[...] + _dot(v_t, p.astype(BF16))
    m_ref[...] = m_new


def _nsa_prompt_kernel(q_ref, kc_ref, vc_ref, ks_ref, vs_ref, kw_ref, vw_ref, gate_ref, o_ref,
                       m_ref, l_ref, acc_ref, osel_ref, gate_t_ref, *, nb, tq, tk):
    g = pl.program_id(1)
    i = pl.program_id(2)
    nq = NSA_GROUP * tq
    nbp = kc_ref.shape[0]
    qblk = q_ref[...]
    qg = jnp.concatenate([qblk[:, r * HEAD_DIM:(r + 1) * HEAD_DIM] for r in range(NSA_GROUP)], axis=0).astype(BF16)
    qpos = i * tq + lax.broadcasted_iota(jnp.int32, (1, tq), 1)
    tile4 = lambda x: jnp.concatenate([x] * NSA_GROUP, axis=1)

    blk = lax.broadcasted_iota(jnp.int32, (nbp, 1), 0)
    zc = _dot_nt(kc_ref[...].astype(BF16), qg) * ATTN_SCALE
    avail = tile4(jnp.logical_and((blk + 1) * NSA_BLOCK - 1 <= qpos, blk < nb))
    zm = jnp.where(avail, zc, NEG_INF)
    e = jnp.where(avail, jnp.exp(zm - jnp.max(zm, axis=0, keepdims=True)), 0.0)
    den = jnp.sum(e, axis=0, keepdims=True)
    pc = e / jnp.where(den > 0.0, den, 1.0)
    o_cmp = _dot(jnp.transpose(vc_ref[...]).astype(BF16), pc.astype(BF16))

    score = pc[:, 0:tq]
    for r in range(1, NSA_GROUP):
        score = score + pc[:, r * tq:(r + 1) * tq]
    nbr = -(-nb // 8) * 8
    blk = blk[0:nbr]
    cur = qpos // NSA_BLOCK
    forced = jnp.logical_or(blk == 0, jnp.logical_or(blk == cur, blk == cur - 1))
    s = jnp.where(blk <= cur, jnp.where(forced, FORCE_SCORE, score[0:nbr]), -1.0)
    rank = jnp.zeros((nbr, tq), F32)
    for mth in range(nb):
        sm = s[mth:mth + 1, :]
        ahead = jnp.logical_or(sm > s, jnp.logical_and(sm == s, blk > mth))
        rank = rank + jnp.where(ahead, 1.0, 0.0)
    sel = jnp.where(jnp.logical_and(rank < float(min(NSA_TOPN, nb)), s >= 0.0), 1.0, 0.0).astype(BF16)

    def reset():
        m_ref[...] = jnp.full_like(m_ref, NEG_INF)
        l_ref[...] = jnp.zeros_like(l_ref)
        acc_ref[...] = jnp.zeros_like(acc_ref)

    reset()
    bpt = tk // NSA_BLOCK

    def sel_body(kt, carry):
        k0 = pl.multiple_of(kt * tk, tk)
        z = _dot_nt(ks_ref[pl.ds(k0, tk), :].astype(BF16), qg) * ATTN_SCALE
        key = lax.broadcasted_iota(jnp.int32, (tk, nbr), 0)
        col = lax.broadcasted_iota(jnp.int32, (tk, nbr), 1)
        expand = (col == kt * bpt + key // NSA_BLOCK).astype(BF16)
        chosen = _dot(expand, sel) > 0.5
        kpos = k0 + lax.broadcasted_iota(jnp.int32, (tk, 1), 0)
        bias = jnp.where(jnp.logical_and(chosen, kpos <= qpos), 0.0, NEG_INF)
        v_t = jnp.transpose(vs_ref[pl.ds(k0, tk), :]).astype(BF16)
        _online_softmax_step(z + tile4(bias), v_t, m_ref, l_ref, acc_ref)
        return carry

    lax.fori_loop(0, (i * tq + tq - 1) // tk + 1, sel_body, 0)
    osel_ref[...] = acc_ref[...] / l_ref[...]

    reset()

    def win_body(kt, carry):
        k0 = pl.multiple_of(kt * tq, tq)
        z = _dot_nt(kw_ref[pl.ds(k0, tq), :].astype(BF16), qg) * ATTN_SCALE
        dist = qpos - (k0 + lax.broadcasted_iota(jnp.int32, (tq, 1), 0))
        bias = jnp.where(jnp.logical_and(dist >= 0, dist <= NSA_WINDOW), 0.0, NEG_INF)
        v_t = jnp.transpose(vw_ref[pl.ds(k0, tq), :]).astype(BF16)
        _online_softmax_step(z + tile4(bias), v_t, m_ref, l_ref, acc_ref)
        return carry

    lax.fori_loop(jnp.maximum(i - NSA_WINDOW // tq, 0), i + 1, win_body, 0)
    o_win = acc_ref[...] / l_ref[...]
    o_sel = osel_ref[...]

    gate_t_ref[...] = jnp.transpose(gate_ref[...])
    for r in range(NSA_GROUP):
        lanes = slice(r * tq, (r + 1) * tq)
        out_t = jnp.zeros((HEAD_DIM, tq), F32)
        for c, o_c in enumerate((o_cmp, o_sel, o_win)):
            gate = gate_t_ref[pl.ds(c * N_HEADS + g * NSA_GROUP + r, 1), :]
            out_t = out_t + gate * o_c[:, lanes]
        o_ref[:, r * HEAD_DIM:(r + 1) * HEAD_DIM] = jnp.transpose(out_t)


def _nsa_prompt_attention(q, kv4, kwin, gates, kc, vc, batch, seq, nb):
    tq = 256
    tk = 512
    nq = seq // tq
    g4 = NSA_KV_GROUPS
    col = lambda off: pl.BlockSpec((seq, HEAD_DIM), lambda b, g, i: (b, off + g))
    cmp_spec = pl.BlockSpec((None, kc.shape[1], HEAD_DIM), lambda b, g, i: (b, 0, g))
    return pl.pallas_call(
        functools.partial(_nsa_prompt_kernel, nb=nb, tq=tq, tk=tk),
        grid=(batch, g4, nq),
        in_specs=[
            pl.BlockSpec((tq, NSA_KV), lambda b, g, i: (b * nq + i, g)),
            cmp_spec, cmp_spec,
            col(2 * g4), col(3 * g4),
            col(0), col(g4),
            pl.BlockSpec((tq, LANES), lambda b, g, i: (b * nq + i, 0)),
        ],
        out_specs=pl.BlockSpec((tq, NSA_KV), lambda b, g, i: (b * nq + i, g)),
        out_shape=jax.ShapeDtypeStruct(q.shape, F32),
        scratch_shapes=[pltpu.VMEM((1, NSA_GROUP * tq), F32), pltpu.VMEM((1, NSA_GROUP * tq), F32),
                        pltpu.VMEM((HEAD_DIM, NSA_GROUP * tq), F32), pltpu.VMEM((HEAD_DIM, NSA_GROUP * tq), F32),
                        pltpu.VMEM((LANES, tq), F32)],
        compiler_params=_params(("arbitrary", "arbitrary", "arbitrary"), 40),
        name="nsa_prompt_attention",
    )(q, kc, vc, kv4, kv4, kwin, kwin, gates)


def _nsa_sample_select_kernel(q_ref, kc_ref, vc_ref, ocmp_ref, idx_ref, *, nb, qpos):
    nbp = kc_ref.shape[0]
    q = q_ref[...]
    lane_blk = lax.broadcasted_iota(jnp.int32, (1, nbp), 1)
    row_i = lax.broadcasted_iota(jnp.int32, (nbp, nbp), 0)
    col_i = lax.broadcasted_iota(jnp.int32, (nbp, nbp), 1)
    eye = row_i == col_i
    cur = qpos // NSA_BLOCK
    avail = jnp.logical_and((lane_blk + 1) * NSA_BLOCK - 1 <= qpos, lane_blk < nb)
    forced = jnp.logical_or(lane_blk == 0, jnp.logical_or(lane_blk == cur, lane_blk == cur - 1))
    for g in range(NSA_KV_GROUPS):
        qg = jnp.concatenate(
            [q[:, (g * NSA_GROUP + r) * HEAD_DIM:(g * NSA_GROUP + r + 1) * HEAD_DIM] for r in range(NSA_GROUP)],
            axis=0).astype(BF16)
        lanes = slice(g * HEAD_DIM, (g + 1) * HEAD_DIM)
        zc = _dot_nt(qg, kc_ref[:, lanes].astype(BF16)) * ATTN_SCALE
        zm = jnp.where(avail, zc, NEG_INF)
        e = jnp.where(avail, jnp.exp(zm - jnp.max(zm, axis=1, keepdims=True)), 0.0)
        den = jnp.sum(e, axis=1, keepdims=True)
        pc = e / jnp.where(den > 0.0, den, 1.0)
        o = _dot(pc.astype(BF16), vc_ref[:, lanes].astype(BF16))
        for r in range(NSA_GROUP):
            h = g * NSA_GROUP + r
            ocmp_ref[:, h * HEAD_DIM:(h + 1) * HEAD_DIM] = o[r:r + 1, :]
        score = jnp.sum(pc, axis=0, keepdims=True)
        s_row = jnp.where(lane_blk <= cur, jnp.where(forced, FORCE_SCORE, score), -1.0)
        s_mat = jnp.broadcast_to(s_row, (nbp, nbp))
        s_col = jnp.sum(jnp.where(eye, s_mat, 0.0), axis=1, keepdims=True)
        ahead = jnp.logical_or(s_mat > s_col, jnp.logical_and(s_mat == s_col, col_i < row_i))
        rank = jnp.sum(jnp.where(ahead, 1.0, 0.0), axis=1, keepdims=True)
        sel_col = jnp.where(jnp.logical_and(rank < float(min(NSA_TOPN, nb)), s_col >= 0.0), 1.0, 0.0)
        sel_row = jnp.sum(jnp.where(eye, jnp.broadcast_to(sel_col, (nbp, nbp)), 0.0), axis=0, keepdims=True)
        slot = jnp.sum(jnp.where(col_i < row_i, jnp.broadcast_to(sel_row, (nbp, nbp)), 0.0), axis=1, keepdims=True)
        k_i = lax.broadcasted_iota(jnp.int32, (nbp, LANES), 1).astype(F32)
        n_i = lax.broadcasted_iota(jnp.int32, (nbp, LANES), 0).astype(F32)
        hit = jnp.logical_and(sel_col > 0.5, slot == k_i)
        idx_ref[g:g + 1, :] = jnp.sum(jnp.where(hit, n_i, 0.0), axis=0, keepdims=True).astype(jnp.int32)


def _nsa_sample_select(q, kc, vc, nb, qpos):
    bd = q.shape[0]
    nbp = kc.shape[1]
    cmp_spec = pl.BlockSpec((None, nbp, NSA_KV), lambda b: (b, 0, 0))
    ocmp, idx = pl.pallas_call(
        functools.partial(_nsa_sample_select_kernel, nb=nb, qpos=qpos),
        grid=(bd,),
        in_specs=[pl.BlockSpec((None, 1, D_MODEL), lambda b: (b, 0, 0)), cmp_spec, cmp_spec],
        out_specs=[pl.BlockSpec((None, 1, D_MODEL), lambda b: (b, 0, 0)),
                   pl.BlockSpec((None, NSA_KV_GROUPS, LANES), lambda b: (b, 0, 0))],
        out_shape=[jax.ShapeDtypeStruct((bd, 1, D_MODEL), F32),
                   jax.ShapeDtypeStruct((bd, NSA_KV_GROUPS, LANES), jnp.int32)],
        compiler_params=_params(("arbitrary",), 32),
        name="nsa_sample_select",
    )(q.reshape(bd, 1, D_MODEL), kc, vc)
    return ocmp, idx[:, :, :NSA_TOPN]


def _nsa_sample_attend_kernel(idx_ref, pt_ref, q_ref, ocmp_ref, gate_ref, kv_ref, kwin_ref, win_ref, cache_ref,
                              o_ref, kbuf, vbuf, sem, *, n_pages, n_past_blocks):
    b = pl.program_id(0)
    g4 = NSA_KV_GROUPS
    bpp = PAGE_SIZE // NSA_BLOCK

    def block_copies(g, k):
        blk = jnp.minimum(idx_ref[(b * g4 + g) * NSA_TOPN + k], n_past_blocks - 1)
        page = pt_ref[b * n_pages + blk // bpp]
        rows = pl.ds((blk % bpp) * NSA_BLOCK, NSA_BLOCK)
        dst = pl.ds(k * NSA_BLOCK, NSA_BLOCK)
        return (pltpu.make_async_copy(cache_ref.at[page, rows, 2], kbuf.at[g % 2, dst], sem.at[0, g % 2, k]),
                pltpu.make_async_copy(cache_ref.at[page, rows, 3], vbuf.at[g % 2, dst], sem.at[1, g % 2, k]))

    def start_group(g):
        for k in range(NSA_TOPN):
            ck, cv = block_copies(g, k)
            ck.start()
            cv.start()

    def wait_group(g):
        for k in range(NSA_TOPN):
            ck, cv = block_copies(g, k)
            ck.wait()
            cv.wait()

    start_group(0)

    q = q_ref[...]
    ocmp = ocmp_ref[...]
    gates = gate_ref[...]
    kv_new = kv_ref[...]
    kwin_new = kwin_ref[...]
    n_sel = NSA_TOPN * NSA_BLOCK
    key_slot = lax.broadcasted_iota(jnp.int32, (1, n_sel), 1) // NSA_BLOCK
    lane = lax.broadcasted_iota(jnp.int32, (NSA_GROUP, LANES), 1)
    sub = lax.broadcasted_iota(jnp.int32, (NSA_GROUP, LANES), 0)
    bf = lambda x: x.astype(BF16).astype(F32)

    def attend(qg, keys, vals, valid, k_new, v_new, new_on):
        z = _dot_nt(qg, keys.astype(BF16)) * ATTN_SCALE
        z_new = jnp.sum(bf(qg) * bf(k_new), axis=1, keepdims=True) * ATTN_SCALE
        zm = z if valid is None else jnp.where(valid, z, NEG_INF)
        zn = jnp.where(new_on, z_new, NEG_INF)
        m = jnp.maximum(jnp.max(zm, axis=1, keepdims=True), zn)
        p = jnp.exp(zm - m) if valid is None else jnp.where(valid, jnp.exp(zm - m), 0.0)
        p_new = jnp.where(new_on, jnp.exp(zn - m), 0.0)
        den = jnp.sum(p, axis=1, keepdims=True) + p_new
        return (_dot(p.astype(BF16), vals.astype(BF16)) + bf(p_new) * bf(v_new)) / den

    for g in range(g4):
        if g + 1 < g4:
            start_group(g + 1)
        wait_group(g)
        qg = jnp.concatenate(
            [q[:, (g * NSA_GROUP + r) * HEAD_DIM:(g * NSA_GROUP + r + 1) * HEAD_DIM] for r in range(NSA_GROUP)],
            axis=0).astype(BF16)
        o_cmp = jnp.concatenate(
            [ocmp[:, (g * NSA_GROUP + r) * HEAD_DIM:(g * NSA_GROUP + r + 1) * HEAD_DIM] for r in range(NSA_GROUP)],
            axis=0)
        valid = jnp.zeros((1, n_sel), jnp.int32)
        has_new = jnp.int32(0)
        for k in range(NSA_TOPN):
            is_new = (idx_ref[(b * g4 + g) * NSA_TOPN + k] >= n_past_blocks).astype(jnp.int32)
            valid = jnp.where(key_slot == k, 1 - is_new, valid)
            has_new = jnp.maximum(has_new, is_new)
        lanes = lambda slot: slice((slot * g4 + g) * HEAD_DIM, (slot * g4 + g + 1) * HEAD_DIM)
        o_sel = attend(qg, kbuf[g % 2, :, g, :], vbuf[g % 2, :, g, :], valid > 0,
                       kv_new[:, lanes(2)], kv_new[:, lanes(3)], has_new > 0)
        o_win = attend(qg, win_ref[:, 0, g, :], win_ref[:, 1, g, :], None,
                       kwin_new[:, lanes(0)], kwin_new[:, lanes(1)], True)
        out = jnp.zeros((NSA_GROUP, HEAD_DIM), F32)
        for c, o_c in enumerate((o_cmp, o_sel, o_win)):
            pick = lane == c * N_HEADS + g * NSA_GROUP + sub
            gate = jnp.sum(jnp.where(pick, jnp.broadcast_to(gates, (NSA_GROUP, LANES)), 0.0), axis=1, keepdims=True)
            out = out + gate * o_c
        for r in range(NSA_GROUP):
            h = g * NSA_GROUP + r
            o_ref[:, h * HEAD_DIM:(h + 1) * HEAD_DIM] = out[r:r + 1, :]


def _nsa_sample_attend(idx, page_table, q, ocmp, gates, kv4, kwin, win_state, cache_kv):
    bd, n_pages = page_table.shape
    g4 = NSA_KV_GROUPS
    sel_buf = pltpu.VMEM((2, NSA_TOPN * NSA_BLOCK, g4, HEAD_DIM), F32)
    row = lambda w: pl.BlockSpec((None, 1, w), lambda b, *_: (b, 0, 0))
    out = pl.pallas_call(
        functools.partial(_nsa_sample_attend_kernel, n_pages=n_pages, n_past_blocks=n_pages * PAGE_SIZE // NSA_BLOCK),
        grid_spec=pltpu.PrefetchScalarGridSpec(
            num_scalar_prefetch=2,
            grid=(bd,),
            in_specs=[row(D_MODEL), row(D_MODEL), row(LANES), row(4 * NSA_KV), row(2 * NSA_KV),
                      pl.BlockSpec((None,) + win_state.shape[1:], lambda b, *_: (b, 0, 0, 0, 0)),
                      pl.BlockSpec(memory_space=pl.ANY)],
            out_specs=row(D_MODEL),
            scratch_shapes=[sel_buf, sel_buf, pltpu.SemaphoreType.DMA((2, 2, NSA_TOPN))]),
        out_shape=jax.ShapeDtypeStruct((bd, 1, D_MODEL), F32),
        compiler_params=_params(("arbitrary",), 40),
        name="nsa_sample_attend",
    )(idx.reshape(-1), page_table.reshape(-1), q.reshape(bd, 1, D_MODEL), ocmp, gates.reshape(bd, 1, LANES),
      kv4.reshape(bd, 1, 4 * NSA_KV), kwin.reshape(bd, 1, 2 * NSA_KV), win_state, cache_kv)
    return out.reshape(bd, D_MODEL)


def _topk_rows(s, k, ids=None):
    if ids is None:
        ids = lax.broadcasted_iota(jnp.int32, s.shape, 0).astype(F32)
    vals, picked = [], []
    for _ in range(k):
        m = jnp.max(s, axis=0, keepdims=True)
        first = jnp.min(jnp.where(s == m, ids, jnp.inf), axis=0, keepdims=True)
        s = jnp.where(ids == first, -jnp.inf, s)
        vals.append(m)
        picked.append(first)
    return jnp.concatenate(vals, axis=0), jnp.concatenate(picked, axis=0)


def _pair_candidates(v1, v2):
    tokens = v1.shape[1]
    row = lambda n: lax.broadcasted_iota(jnp.int32, (n, tokens), 0).astype(F32)
    vals, ids = [], []
    for k1 in range(4):
        n = 16 if k1 == 0 else 8
        k2 = row(n)
        vals.append(jnp.where((k1 + 1) * (k2 + 1.0) <= PEER_TOPK, v1[k1:k1 + 1, :] + v2[0:n, :], -jnp.inf))
        ids.append(k1 * PEER_TOPK + k2)
    for k2 in range(3):
        n = 16 if k2 == 0 else 8
        k1 = row(n)
        keep = jnp.logical_and(k1 >= 4.0, (k1 + 1.0) * (k2 + 1) <= PEER_TOPK)
        vals.append(jnp.where(keep, v1[0:n, :] + v2[k2:k2 + 1, :], -jnp.inf))
        ids.append(k1 * PEER_TOPK + k2)
    return jnp.concatenate(vals, axis=0), jnp.concatenate(ids, axis=0)


def _pick_rows(sel, table):
    out = jnp.zeros_like(sel)
    for k in range(table.shape[0]):
        out = jnp.where(sel == float(k), table[k:k + 1, :], out)
    return out


def _peer_route_kernel(h_ref, wq_ref, k1_ref, k2_ref, i1_ref, i2_ref, g_ref, qh_ref, i1_t, i2_t, g_t):
    qh_ref[...] = _dot(h_ref[...], wq_ref[...])
    k1 = k1_ref[...].astype(BF16)
    k2 = k2_ref[...].astype(BF16)
    half = PEER_NKEYS

    def head(h, carry):
        c0 = pl.multiple_of(h * 2 * half, 2 * half)
        s1 = _dot_nt(k1, qh_ref[:, pl.ds(c0, half)].astype(BF16))
        s2 = _dot_nt(k2, qh_ref[:, pl.ds(c0 + half, half)].astype(BF16))
        v1, i1 = _topk_rows(s1, PEER_TOPK)
        v2, i2 = _topk_rows(s2, PEER_TOPK)
        cand, cand_id = _pair_candidates(v1, v2)
        sc, j = _topk_rows(cand, PEER_TOPK, cand_id)
        ja = jnp.floor(j * (1.0 / PEER_TOPK))
        jb = j - ja * PEER_TOPK
        e = jnp.exp(sc - jnp.max(sc, axis=0, keepdims=True))
        r0 = pl.multiple_of(h * PEER_TOPK, PEER_TOPK)
        i1_t[pl.ds(r0, PEER_TOPK), :] = _pick_rows(ja, i1)
        i2_t[pl.ds(r0, PEER_TOPK), :] = _pick_rows(jb, i2)
        g_t[pl.ds(r0, PEER_TOPK), :] = e / jnp.sum(e, axis=0, keepdims=True)
        return carry

    lax.fori_loop(0, PEER_HEADS, head, 0)
    i1_ref[...] = jnp.transpose(i1_t[...])
    i2_ref[...] = jnp.transpose(i2_t[...])
    g_ref[...] = jnp.transpose(g_t[...])


def _peer_route(h, wq, k1, k2, tm):
    m, d = h.shape
    tm = min(tm, m)
    hk = PEER_HEADS * PEER_TOPK
    out = pl.BlockSpec((tm, hk), lambda i: (i, 0))
    keys = pl.BlockSpec(k1.shape, lambda i: (0, 0))
    return pl.pallas_call(
        _peer_route_kernel,
        grid=(m // tm,),
        in_specs=[pl.BlockSpec((tm, d), lambda i: (i, 0)), pl.BlockSpec(wq.shape, lambda i: (0, 0)), keys, keys],
        out_specs=[out, out, out],
        out_shape=[jax.ShapeDtypeStruct((m, hk), F32)] * 3,
        scratch_shapes=[pltpu.VMEM((tm, wq.shape[1]), F32)] + [pltpu.VMEM((hk, tm), F32)] * 3,
        compiler_params=_params(("arbitrary",), 40),
        name="peer_route",
    )(h, wq, k1, k2)


PEER_TOKENS_PER_TRIP = 16


def _peer_weights_kernel(i1_ref, i2_ref, g_ref, w_ref):
    tm = i1_ref.shape[0]
    hk = i1_ref.shape[1]
    key = lax.broadcasted_iota(jnp.int32, (PEER_NKEYS, hk), 0).astype(F32)

    def body(step, carry):
        for r in range(PEER_TOKENS_PER_TRIP):
            t = step * PEER_TOKENS_PER_TRIP + r
            i1 = jnp.broadcast_to(i1_ref[pl.ds(t, 1), :], (PEER_NKEYS, hk))
            i2 = jnp.broadcast_to(i2_ref[pl.ds(t, 1), :], (PEER_NKEYS, hk))
            gt = jnp.broadcast_to(g_ref[pl.ds(t, 1), :], (PEER_NKEYS, hk))
            p = jnp.where(i1 == key, 1.0, 0.0).astype(BF16)
            q = jnp.where(i2 == key, gt, 0.0).astype(BF16)
            w_ref[t] = _dot_nt(p, q)
        return carry

    lax.fori_loop(0, tm // PEER_TOKENS_PER_TRIP, body, 0)


def _peer_weights(i1, i2, g, tm):
    m, hk = i1.shape
    tm = min(tm, m)
    spec = pl.BlockSpec((tm, hk), lambda i: (i, 0))
    return pl.pallas_call(
        _peer_weights_kernel,
        grid=(m // tm,),
        in_specs=[spec, spec, spec],
        out_specs=pl.BlockSpec((tm, PEER_NKEYS, PEER_NKEYS), lambda i: (i, 0, 0)),
        out_shape=jax.ShapeDtypeStruct((m, PEER_NKEYS, PEER_NKEYS), F32),
        compiler_params=_params(("arbitrary",), 40),
        name="peer_weights",
    )(i1, i2, g)


def _peer_dense_kernel(h_ref, u_ref, v_ref, x_ref, gate_ref, lng_ref, lnb_ref, scale_ref, shift_ref, w_hbm,
                       xo_ref, ho_ref, acc_ref, wbuf, wsem, *, n_a):
    i = pl.program_id(0)
    c = pl.program_id(1)
    nc = pl.num_programs(1)
    step = i * nc + c
    tm = h_ref.shape[0]

    def map_copies(s):
        rows = pl.ds((s // nc) * tm, tm)
        return [pltpu.make_async_copy(w_hbm.at[rows, (s % nc) * n_a + k, :], wbuf.at[s % 2, k], wsem.at[s % 2, k])
                for k in range(n_a)]

    @pl.when(step == 0)
    def _():
        for cp in map_copies(step):
            cp.start()

    @pl.when(step + 1 < pl.num_programs(0) * nc)
    def _():
        for cp in map_copies(step + 1):
            cp.start()

    @pl.when(c == 0)
    def _():
        acc_ref[...] = jnp.zeros_like(acc_ref)

    act = _gelu(_dot_nt(h_ref[...], u_ref[...]))
    for cp in map_copies(step):
        cp.wait()
    weighted = [act[:, k * PEER_NKEYS:(k + 1) * PEER_NKEYS] * wbuf[step % 2, k] for k in range(n_a)]
    acc_ref[...] += _dot(jnp.concatenate(weighted, axis=1).astype(BF16), v_ref[...])

    @pl.when(c == pl.num_programs(1) - 1)
    def _():
        xn, h = _ln_res(acc_ref[...], x_ref[...], gate_ref[...], lng_ref[...], lnb_ref[...],
                        scale_ref[...], shift_ref[...])
        xo_ref[...] = xn
        ho_ref[...] = h


def _peer_dense(h, u, v, layer, w, xres, gate, lng, lnb, scale, shift, tm, te):
    m, d = h.shape
    n_exp = u.shape[1]
    tm = min(tm, m)
    bpb = (m // tm) // gate.shape[0]
    n_a = te // PEER_NKEYS
    row = pl.BlockSpec((tm, d), lambda i, c: (i, 0))
    tab = pl.BlockSpec((None, te, d), lambda i, c: (layer, c, 0))
    vec = pl.BlockSpec((1, d), lambda i, c: (0, 0))
    return pl.pallas_call(
        functools.partial(_peer_dense_kernel, n_a=n_a),
        grid=(m // tm, n_exp // te),
        in_specs=[row, tab, tab, row, _mod_spec(gate, bpb), vec, vec, _mod_spec(scale, bpb), _mod_spec(shift, bpb),
                  pl.BlockSpec(memory_space=pl.ANY)],
        out_specs=[row, row],
        out_shape=[jax.ShapeDtypeStruct((m, d), F32), jax.ShapeDtypeStruct((m, d), BF16)],
        scratch_shapes=[pltpu.VMEM((tm, d), F32), pltpu.VMEM((2, n_a, tm, PEER_NKEYS), F32),
                        pltpu.SemaphoreType.DMA((2, n_a))],
        compiler_params=_params(("arbitrary", "arbitrary"), 56),
        name="peer_dense",
    )(h, u, v, xres, gate, lng, lnb, scale, shift, w)


def _peer(h, xres, wq, k1, k2, u, v, layer, gate, lng, lnb, scale, shift, tm_route, tm_dense, te):
    i1, i2, g = _peer_route(h, wq, k1, k2, tm_route)
    w = _peer_weights(i1, i2, g, 128)
    return _peer_dense(h, u, v, layer, w, xres, gate, lng, lnb, scale, shift, tm_dense, te)


def _rope_tables(pos):
    half = ROPE_DIMS // 2
    inv = jnp.power(ROPE_THETA, -jnp.arange(half, dtype=F32) / half)
    ang = pos.astype(F32)[:, None] * inv
    cos, sin = jnp.cos(ang), jnp.sin(ang)
    n = pos.shape[0]
    ones = jnp.ones((n, HEAD_DIM - ROPE_DIMS), F32)
    zeros = jnp.zeros((n, HEAD_DIM - ROPE_DIMS), F32)
    zh = jnp.zeros((n, half), F32)
    return (jnp.concatenate([cos, cos, ones], axis=1),
            jnp.concatenate([-sin, zh, zeros], axis=1),
            jnp.concatenate([zh, sin, zeros], axis=1))


def _pad_rows(x, rows):
    return jnp.pad(x, ((0, rows - x.shape[0]),) + ((0, 0),) * (x.ndim - 1))


def kernel(x_prompt, x_sample, c_prompt, c_sample, cache_l0_kv, cache_l1_kv, state_l1_win, cache_l2_kv, cache_l3_kv,
           state_l3_win, page_table, ada_w, ada_b, ln1_g, ln1_b, ln2_g, ln2_b, sb_w_in, sb_w_o, nsa_w_in, nsa_w_o,
           nsa_cmp_pe, nsa_cmp_w1, nsa_cmp_b1, nsa_cmp_w2, nsa_cmp_b2, peer_wq, peer_k1, peer_k2, peer_u, peer_v):
    batch, seq, d = x_prompt.shape
    bd, dec_seq, _ = x_sample.shape
    assert dec_seq == 1 and d == D_MODEL
    depth = ada_w.shape[0]
    n_pages = page_table.shape[1]
    past = n_pages * PAGE_SIZE
    sb_caches = (cache_l0_kv, cache_l2_kv)
    nsa_caches = (cache_l1_kv, cache_l3_kv)
    nsa_wins = (state_l1_win, state_l3_win)
    win_buf = state_l1_win.shape[1]
    assert win_buf == NSA_WINDOW and past % NSA_BLOCK == 0 and seq >= win_buf
    mp_rows = batch * seq
    tm = 512

    n_c = batch + bd
    c_all = _pad_rows(jnp.concatenate([c_prompt, c_sample], axis=0), -(-n_c // 8) * 8)
    mod = _modulation(c_all, ada_w, ada_b).reshape(depth, c_all.shape[0], 6, d)
    mod_p = lambda i, k: mod[i, :batch, k].reshape(batch, 1, d)
    mod_s = lambda i, k: mod[i, batch:n_c, k].reshape(1, bd, d)

    xp = x_prompt.reshape(mp_rows, d)
    xs = x_sample.reshape(bd, d)
    hp = _modulate(xp, mod_p(0, 1), mod_p(0, 0), tm)
    hs = _modulate(xs, mod_s(0, 1), mod_s(0, 0), bd)

    rope_p = _rope_tables(jnp.arange(seq))
    rope_s = _rope_tables(jnp.full((bd,), past))
    nb_p = -(-seq // NSA_BLOCK)
    assert nb_p * NSA_BLOCK == seq
    nb_s = past // NSA_BLOCK + 1

    u_all = peer_u.astype(BF16)
    v_all = peer_v.astype(BF16)
    new = []
    for i in range(depth):
        j = i // 2
        vec = lambda a: a[i].reshape(1, d)
        if i % 2 == 0:
            w_in = sb_w_in[j].astype(BF16)
            w_o = sb_w_o[j].astype(BF16)
            sb_rows = (2, N_HEADS)
            q_p = _matmul(hp, w_in[:, :d], tm, 512, out_dtype=BF16, name="sb_q")
            kv_p, kv_rows_p = _matmul(hp, w_in[:, d:], tm, 1024, out_dtype=BF16, rows_out=sb_rows, name="sb_kv")
            o_p = _sb_prompt_attention(q_p, kv_p, batch, seq)
            q_s = _matmul(hs, w_in[:, :d], bd, 512, out_dtype=BF16, name="sb_q")
            _, kv_rows_s = _matmul(hs, w_in[:, d:], bd, 1024, out_dtype=BF16, rows_out=sb_rows, name="sb_kv")
            o_s = _sb_sample_attention(q_s, sb_caches[j], page_table)
            new.append((kv_rows_p.reshape(batch, seq, 2, N_HEADS, HEAD_DIM),
                        kv_rows_s.reshape(bd, 1, 2, N_HEADS, HEAD_DIM)))
        else:
            w_in = nsa_w_in[j].astype(BF16)
            w_o = nsa_w_o[j].astype(BF16)
            w_q = w_in[:, :d]
            w_kv4 = w_in[:, d:d + 4 * NSA_KV]
            w_kw = w_in[:, d + 4 * NSA_KV:d + 6 * NSA_KV]
            w_g = jnp.pad(w_in[:, d + 6 * NSA_KV:], ((0, 0), (0, LANES - 3 * N_HEADS)))
            cmp_w = (nsa_cmp_pe[j], nsa_cmp_w1[j].astype(BF16), nsa_cmp_b1[j], nsa_cmp_w2[j], nsa_cmp_b2[j])

            def project(h, rope, rows):
                kv4, kv4_rows = _matmul(h, w_kv4, rows, NSA_KV, epilogue="rope_even", rope=rope,
                                        rows_out=(4, NSA_KV_GROUPS), name="nsa_kv4")
                kwin, kwin_rows = _matmul(h, w_kw, rows, NSA_KV, epilogue="rope_even", rope=rope,
                                          rows_out=(2, NSA_KV_GROUPS), name="nsa_kwin")
                return (_matmul(h, w_q, rows, 512, epilogue="rope_all", rope=rope, name="nsa_q"), kv4, kwin,
                        _matmul(h, w_g, rows, LANES, epilogue="sigmoid", name="nsa_gates"), kv4_rows, kwin_rows)

            q_p, kv4_p, kwin_p, gates_p, kv4_rows_p, kwin_rows_p = project(hp, rope_p, tm)
            cmp_p = _compress(kv4_p.reshape(batch * nb_p, NSA_BLOCK, 4 * NSA_KV), *cmp_w, 256)
            nbp = -(-nb_p // LANES) * LANES
            cmp_p = jnp.pad(cmp_p.reshape(2, batch, nb_p, NSA_KV), ((0, 0), (0, 0), (0, nbp - nb_p), (0, 0)))
            o_p = _nsa_prompt_attention(q_p, kv4_p, kwin_p, gates_p, cmp_p[0], cmp_p[1], batch, seq, nb_p)
            win_p = kwin_rows_p.reshape(batch, seq, 2, NSA_KV_GROUPS, HEAD_DIM)[:, seq - win_buf:]
            q_s, kv4_s, kwin_s, gates_s, kv4_rows_s, kwin_rows_s = project(hs, rope_s, bd)
            x_past = _gather_cmp_pages(nsa_caches[j], page_table)
            cmp_past = _compress(x_past.reshape(bd * (nb_s - 1), NSA_BLOCK, 2 * NSA_KV), *cmp_w, 256)
            x_last = jnp.pad(kv4_s[:, None, :2 * NSA_KV], ((0, 0), (0, NSA_BLOCK - 1), (0, 0)))
            cmp_last = _compress(x_last, *cmp_w, bd)
            nbs = -(-nb_s // LANES) * LANES
            cmp_s = jnp.concatenate([cmp_past.reshape(2, bd, nb_s - 1, NSA_KV), cmp_last.reshape(2, bd, 1, NSA_KV)], 2)
            cmp_s = jnp.pad(cmp_s, ((0, 0), (0, 0), (0, nbs - nb_s), (0, 0)))
            ocmp_s, idx_s = _nsa_sample_select(q_s, cmp_s[0], cmp_s[1], nb_s, past)
            o_s = _nsa_sample_attend(idx_s, page_table, q_s, ocmp_s, gates_s, kv4_s, kwin_s, nsa_wins[j], nsa_caches[j])
            win_s = jnp.concatenate([nsa_wins[j][:, 1:], kwin_rows_s.reshape(bd, 1, 2, NSA_KV_GROUPS, HEAD_DIM)], axis=1)
            new.append((kv4_rows_p.reshape(batch, seq, 4, NSA_KV_GROUPS, HEAD_DIM),
                        kv4_rows_s.reshape(bd, 1, 4, NSA_KV_GROUPS, HEAD_DIM), win_p, win_s))

        xp, hp = _matmul_ln(o_p, w_o, xp, mod_p(i, 2), vec(ln1_g), vec(ln1_b), mod_p(i, 4), mod_p(i, 3), tm)
        xs, hs = _matmul_ln(o_s, w_o, xs, mod_s(i, 2), vec(ln1_g), vec(ln1_b), mod_s(i, 4), mod_s(i, 3), bd)

        nxt = min(i + 1, depth - 1)
        wq = peer_wq[i].astype(BF16)
        xp, hp = _peer(hp, xp, wq, peer_k1[i], peer_k2[i], u_all, v_all, i, mod_p(i, 5), vec(ln2_g), vec(ln2_b),
                       mod_p(nxt, 1), mod_p(nxt, 0), 256, 512, 1024)
        pad = LANES
        hs_pad = _pad_rows(hs, pad)
        gate_s = _pad_rows(mod_s(i, 5)[0], pad)[None]
        scale_s = _pad_rows(mod_s(nxt, 1)[0], pad)[None]
        shift_s = _pad_rows(mod_s(nxt, 0)[0], pad)[None]
        xs_pad, hs_pad = _peer(hs_pad, _pad_rows(xs, pad), wq, peer_k1[i], peer_k2[i], u_all, v_all, i, gate_s,
                               vec(ln2_g), vec(ln2_b), scale_s, shift_s, pad, pad, 512)
        xs, hs = xs_pad[:bd], hs_pad[:bd]

    return (xp.reshape(batch, seq, d), xs.reshape(bd, 1, d), new[0][0], new[0][1], new[1][0], new[1][1], new[1][2],
            new[1][3], new[2][0], new[2][1], new[3][0], new[3][1], new[3][2], new[3][3])
```

```python
import functools

import jax
import jax.numpy as jnp
import numpy as np
from jax import lax
from jax.experimental import pallas as pl
from jax.experimental.pallas import tpu as pltpu

F32 = jnp.float32
BF16 = jnp.bfloat16

D_MODEL = 2048
HEAD_DIM = 128
N_HEADS = D_MODEL // HEAD_DIM
PAGE_SIZE = 128
NSA_KV_GROUPS = 4
NSA_GROUP = N_HEADS // NSA_KV_GROUPS
NSA_KV = NSA_KV_GROUPS * HEAD_DIM
NSA_BLOCK = 64
NSA_TOPN = 16
NSA_WINDOW = 512
ROPE_THETA = 500000.0
ROPE_DIMS = HEAD_DIM // 4
PEER_HEADS = 8
PEER_NKEYS = 128
PEER_TOPK = 16
DEPTH = 4
DEEPNORM_ALPHA = (2 * DEPTH) ** 0.25
LN_EPS = 1e-5
NEG_INF = -1e30
FORCE_SCORE = 1e9
ATTN_SCALE = HEAD_DIM ** -0.5
SQRT_HALF = float(np.sqrt(0.5))

LANES = 128
VMEM_LIMIT_CAP = 56 * 1024 * 1024
SB_LOG_UNDERFLOW = -104.0


def _params(sem, vmem_mb):
    return pltpu.CompilerParams(
        dimension_semantics=sem, vmem_limit_bytes=min(vmem_mb * 1024 * 1024, VMEM_LIMIT_CAP))


def _dot(a, b):
    return jnp.dot(a, b, preferred_element_type=F32)


def _dot_nt(a, b):
    return lax.dot_general(a, b, (((1,), (1,)), ((), ())), preferred_element_type=F32)


def _gelu(x):
    return 0.5 * x * (1.0 + lax.erf(x * SQRT_HALF))


def _log_sigmoid_pair(z):
    l1p = jnp.log(1.0 + jnp.exp(-jnp.abs(z)))
    return jnp.minimum(z, 0.0) - l1p, jnp.minimum(-z, 0.0) - l1p


def _ln_res(y, xres, gate, lng, lnb, scale, shift):
    v = DEEPNORM_ALPHA * xres + gate * y
    mu = jnp.mean(v, axis=-1, keepdims=True)
    d = v - mu
    var = jnp.mean(d * d, axis=-1, keepdims=True)
    xn = d * lax.rsqrt(var + LN_EPS) * lng + lnb
    return xn, (xn * (1.0 + scale) + shift).astype(BF16)


def _mod_spec(arr, blocks_per_batch):
    return pl.BlockSpec((None,) + arr.shape[1:], lambda i, *_: (i // blocks_per_batch, 0, 0))


def _modulation_kernel(c_ref, w_ref, b_ref, o_ref):
    c = c_ref[...]
    s = (c * jax.nn.sigmoid(c)).astype(BF16)
    o_ref[...] = _dot(s, w_ref[...].astype(BF16)) + b_ref[...]


def _modulation(c_all, ada_w, ada_b):
    depth, d, n = ada_w.shape
    r = c_all.shape[0]
    tn = 1024
    return pl.pallas_call(
        _modulation_kernel,
        grid=(depth, n // tn),
        in_specs=[
            pl.BlockSpec((r, d), lambda l, j: (0, 0)),
            pl.BlockSpec((None, d, tn), lambda l, j: (l, 0, j)),
            pl.BlockSpec((None, 1, tn), lambda l, j: (l, 0, j)),
        ],
        out_specs=pl.BlockSpec((None, r, tn), lambda l, j: (l, 0, j)),
        out_shape=jax.ShapeDtypeStruct((depth, r, n), F32),
        compiler_params=_params(("arbitrary", "arbitrary"), 32),
        name="modulation",
    )(c_all, ada_w, ada_b.reshape(depth, 1, n))


def _modulate_kernel(x_ref, scale_ref, shift_ref, o_ref):
    o_ref[...] = (x_ref[...] * (1.0 + scale_ref[...]) + shift_ref[...]).astype(BF16)


def _modulate(x, scale, shift, tm):
    m, d = x.shape
    bpb = (m // tm) // scale.shape[0]
    return pl.pallas_call(
        _modulate_kernel,
        grid=(m // tm,),
        in_specs=[pl.BlockSpec((tm, d), lambda i: (i, 0)), _mod_spec(scale, bpb), _mod_spec(shift, bpb)],
        out_specs=pl.BlockSpec((tm, d), lambda i: (i, 0)),
        out_shape=jax.ShapeDtypeStruct((m, d), BF16),
        compiler_params=_params(("arbitrary",), 32),
        name="modulate",
    )(x, scale, shift)


def _store_tile(y, o_refs):
    o_refs[0][...] = y.astype(o_refs[0].dtype)
    if len(o_refs) > 1:
        for h in range(o_refs[1].shape[1]):
            o_refs[1][:, h, :] = y[:, h * HEAD_DIM:(h + 1) * HEAD_DIM]


def _mm_kernel(a_ref, w_ref, *o_refs):
    _store_tile(_dot(a_ref[...], w_ref[...]), o_refs)


def _mm_sigmoid_kernel(a_ref, w_ref, o_ref):
    o_ref[...] = jax.nn.sigmoid(_dot(a_ref[...], w_ref[...]))


def _mm_rope_kernel(a_ref, w_ref, cos_ref, sa_ref, sb_ref, *o_refs, even_only):
    acc = _dot(a_ref[...], w_ref[...])
    tn = acc.shape[1]
    reps = tn // HEAD_DIM
    cos = jnp.concatenate([cos_ref[...]] * reps, axis=1)
    sa = jnp.concatenate([sa_ref[...]] * reps, axis=1)
    sb = jnp.concatenate([sb_ref[...]] * reps, axis=1)
    half = ROPE_DIMS // 2
    rot = acc * cos + pltpu.roll(acc, tn - half, 1) * sa + pltpu.roll(acc, half, 1) * sb
    if even_only:
        rot = jnp.where(pl.program_id(1) % 2 == 0, rot, acc)
    _store_tile(rot, o_refs)


def _matmul(a, w, tm, tn, *, epilogue=None, rope=None, out_dtype=F32, rows_out=None, name="matmul"):
    m, k = a.shape
    n = w.shape[1]
    tm = min(tm, m)
    in_specs = [pl.BlockSpec((tm, k), lambda i, j: (i, 0)), pl.BlockSpec((k, tn), lambda i, j: (0, j))]
    args = [a, w]
    if epilogue in ("rope_all", "rope_even"):
        assert epilogue == "rope_all" or tn == NSA_KV
        t_tab = rope[0].shape[0]
        nt = t_tab // tm
        in_specs += [pl.BlockSpec((tm, HEAD_DIM), lambda i, j: (i % nt, 0))] * 3
        args += list(rope)
        kern = functools.partial(_mm_rope_kernel, even_only=(epilogue == "rope_even"))
    elif epilogue == "sigmoid":
        kern = _mm_sigmoid_kernel
    else:
        kern = _mm_kernel
    out_specs = pl.BlockSpec((tm, tn), lambda i, j: (i, j))
    out_shape = jax.ShapeDtypeStruct((m, n), out_dtype)
    if rows_out is not None:
        slots, heads = rows_out
        tiles_per_slot = heads * HEAD_DIM // tn
        out_specs = [out_specs, pl.BlockSpec((tm, None, heads // tiles_per_slot, HEAD_DIM),
                                             lambda i, j: (i, j // tiles_per_slot, j % tiles_per_slot, 0))]
        out_shape = [out_shape, jax.ShapeDtypeStruct((m, slots, heads, HEAD_DIM), F32)]
    return pl.pallas_call(
        kern,
        grid=(m // tm, n // tn),
        in_specs=in_specs,
        out_specs=out_specs,
        out_shape=out_shape,
        compiler_params=_params(("arbitrary", "arbitrary"), 40),
        name=name,
    )(*args)


def _mm_ln_kernel(a_ref, w_ref, x_ref, gate_ref, lng_ref, lnb_ref, scale_ref, shift_ref, xo_ref, ho_ref):
    y = _dot(a_ref[...].astype(BF16), w_ref[...])
    xn, h = _ln_res(y, x_ref[...], gate_ref[...], lng_ref[...], lnb_ref[...], scale_ref[...], shift_ref[...])
    xo_ref[...] = xn
    ho_ref[...] = h


def _matmul_ln(a, w, xres, gate, lng, lnb, scale, shift, tm):
    m, k = a.shape
    d = w.shape[1]
    tm = min(tm, m)
    bpb = (m // tm) // gate.shape[0]
    row = pl.BlockSpec((tm, d), lambda i: (i, 0))
    vec = pl.BlockSpec((1, d), lambda i: (0, 0))
    return pl.pallas_call(
        _mm_ln_kernel,
        grid=(m // tm,),
        in_specs=[pl.BlockSpec((tm, k), lambda i: (i, 0)), pl.BlockSpec((k, d), lambda i: (0, 0)), row,
                  _mod_spec(gate, bpb), vec, vec, _mod_spec(scale, bpb), _mod_spec(shift, bpb)],
        out_specs=[row, row],
        out_shape=[jax.ShapeDtypeStruct((m, d), F32), jax.ShapeDtypeStruct((m, d), BF16)],
        compiler_params=_params(("arbitrary",), 48),
        name="matmul_ln",
    )(a, w, xres, gate, lng, lnb, scale, shift)


def _suffix_matrix(n):
    return (lax.broadcasted_iota(jnp.int32, (n, n), 0) > lax.broadcasted_iota(jnp.int32, (n, n), 1)).astype(BF16)


def _suffix_sum(x, lmat):
    hi = x.astype(BF16)
    lo = (x - hi.astype(F32)).astype(BF16)
    return _dot(hi, lmat) + _dot(lo, lmat)


SB_HEADS_PER_STEP = 8


def _sb_prompt_kernel(q_ref, k_ref, v_ref, o_ref, carry_ref, acc_ref, *, tq, tk):
    i = pl.program_id(2)
    carry_ref[...] = jnp.zeros_like(carry_ref)
    acc_ref[...] = jnp.zeros_like(acc_ref)
    qpos = i * tq + lax.broadcasted_iota(jnp.int32, (tq, 1), 0)
    lmat = jnp.concatenate([_suffix_matrix(tk), jnp.ones((tk, tk), BF16)], axis=1)

    def body(state):
        j, _ = state
        k0 = pl.multiple_of(j * tk, tk)
        allowed = (k0 + lax.broadcasted_iota(jnp.int32, (1, tk), 1)) < qpos
        heads = range(SB_HEADS_PER_STEP)
        lanes = [slice(hh * HEAD_DIM, (hh + 1) * HEAD_DIM) for hh in heads]
        zs = [_dot_nt(q_ref[:, lanes[hh]].astype(BF16), k_ref[pl.ds(k0, tk), lanes[hh]].astype(BF16)) * ATTN_SCALE
              for hh in heads]
        pairs = [_log_sigmoid_pair(z) for z in zs]
        sums = [_suffix_sum(jnp.where(allowed, lsn, 0.0), lmat) for _, lsn in pairs]
        carries = [carry_ref[hh] for hh in heads]
        probs = [jnp.where(allowed, jnp.exp(pairs[hh][0] + sums[hh][:, :tk] + carries[hh]), 0.0).astype(BF16)
                 for hh in heads]
        pvs = [_dot(probs[hh], v_ref[pl.ds(k0, tk), lanes[hh]].astype(BF16)) for hh in heads]
        top = jnp.full((tq, tk), -jnp.inf, F32)
        for hh in heads:
            acc_ref[:, lanes[hh]] += pvs[hh]
            carry = carries[hh] + sums[hh][:, tk:]
            carry_ref[hh] = carry
            top = jnp.maximum(top, carry)
        return j - 1, (jnp.max(top) > SB_LOG_UNDERFLOW).astype(jnp.int32)

    lax.while_loop(lambda s: jnp.logical_and(s[0] >= 0, s[1] > 0), body, ((i * tq + tq - 1) // tk, jnp.int32(1)))
    o_ref[...] = acc_ref[...]


def _sb_prompt_attention(q, kv, batch, seq, tq=256, tk=128):
    nq = seq // tq
    hb = SB_HEADS_PER_STEP
    width = hb * HEAD_DIM
    n_hb = N_HEADS // hb
    return pl.pallas_call(
        functools.partial(_sb_prompt_kernel, tq=tq, tk=tk),
        grid=(batch, n_hb, nq),
        in_specs=[
            pl.BlockSpec((tq, width), lambda b, h, i: (b * nq + i, h)),
            pl.BlockSpec((seq, width), lambda b, h, i: (b, h)),
            pl.BlockSpec((seq, width), lambda b, h, i: (b, n_hb + h)),
        ],
        out_specs=pl.BlockSpec((tq, width), lambda b, h, i: (b * nq + i, h)),
        out_shape=jax.ShapeDtypeStruct(q.shape, F32),
        scratch_shapes=[pltpu.VMEM((hb, tq, tk), F32), pltpu.VMEM((tq, width), F32)],
        compiler_params=_params(("arbitrary", "arbitrary", "arbitrary"), 48),
        name="sb_prompt_attention",
    )(q, kv, kv)


SB_ROW_REP = 8


def _sb_sample_kernel(pt_ref, q_ref, cache_ref, o_ref, kbuf, vbuf, sem, carry_ref, acc_ref, *, n_pages):
    b = pl.program_id(0)
    rep = SB_ROW_REP
    q = q_ref[...]
    qh = [jnp.broadcast_to(q[:, h * HEAD_DIM:(h + 1) * HEAD_DIM], (rep, HEAD_DIM)).astype(BF16)
          for h in range(N_HEADS)]
    lmat = _suffix_matrix(PAGE_SIZE)
    carry_ref[...] = jnp.zeros_like(carry_ref)
    acc_ref[...] = jnp.zeros_like(acc_ref)

    def page_copies(page):
        return (pltpu.make_async_copy(cache_ref.at[page, :, 0], kbuf, sem.at[0]),
                pltpu.make_async_copy(cache_ref.at[page, :, 1], vbuf, sem.at[1]))

    def body(state):
        p, _ = state
        ck, cv = page_copies(pt_ref[b * n_pages + p])
        ck.start()
        cv.start()
        ck.wait()
        cv.wait()
        z = jnp.concatenate([_dot_nt(qh[h], kbuf[:, h, :].astype(BF16)) for h in range(N_HEADS)], axis=0)
        ls, lsn = _log_sigmoid_pair(z * ATTN_SCALE)
        later = _suffix_sum(lsn, lmat) + carry_ref[...]
        a = jnp.exp(ls + later)
        pv = [_dot(a[h * rep:(h + 1) * rep, :].astype(BF16), vbuf[:, h, :].astype(BF16)) for h in range(N_HEADS)]
        acc_ref[...] += jnp.concatenate(pv, axis=0)
        carry = carry_ref[...] + jnp.sum(lsn, axis=1, keepdims=True)
        carry_ref[...] = carry
        return p - 1, (jnp.max(carry) > SB_LOG_UNDERFLOW).astype(jnp.int32)

    lax.while_loop(lambda s: jnp.logical_and(s[0] >= 0, s[1] > 0), body, (jnp.int32(n_pages - 1), jnp.int32(1)))
    for h in range(N_HEADS):
        o_ref[:, h * HEAD_DIM:(h + 1) * HEAD_DIM] = acc_ref[h * rep:h * rep + 1, :]


def _sb_sample_attention(q, cache_kv, page_table):
    bd, n_pages = page_table.shape
    page_buf = pltpu.VMEM((PAGE_SIZE, N_HEADS, HEAD_DIM), F32)
    out = pl.pallas_call(
        functools.partial(_sb_sample_kernel, n_pages=n_pages),
        grid_spec=pltpu.PrefetchScalarGridSpec(
            num_scalar_prefetch=1,
            grid=(bd,),
            in_specs=[pl.BlockSpec((None, 1, D_MODEL), lambda b, pt: (b, 0, 0)),
                      pl.BlockSpec(memory_space=pl.ANY)],
            out_specs=pl.BlockSpec((None, 1, D_MODEL), lambda b, pt: (b, 0, 0)),
            scratch_shapes=[page_buf, page_buf, pltpu.SemaphoreType.DMA((2,)),
                            pltpu.VMEM((N_HEADS * SB_ROW_REP, 1), F32),
                            pltpu.VMEM((N_HEADS * SB_ROW_REP, HEAD_DIM), F32)]),
        out_shape=jax.ShapeDtypeStruct((bd, 1, D_MODEL), F32),
        compiler_params=_params(("arbitrary",), 16),
        name="sb_sample_attention",
    )(page_table.reshape(-1), q.reshape(bd, 1, D_MODEL), cache_kv)
    return out.reshape(bd, D_MODEL)


def _compress_kernel(x_hbm, pe_ref, w1_ref, b1_ref, w2_ref, b2_ref, o_ref, xbuf, sem, *, tr):
    n_i = pl.num_programs(2)
    step = (pl.program_id(0) * pl.num_programs(1) + pl.program_id(1)) * n_i + pl.program_id(2)
    total = pl.num_programs(0) * pl.num_programs(1) * n_i

    def block_copies(s):
        rows = pl.ds((s % n_i) * tr, tr)
        lanes = pl.ds((s // n_i) * HEAD_DIM, HEAD_DIM)
        return [pltpu.make_async_copy(x_hbm.at[rows, t, lanes], xbuf.at[s % 2, t], sem.at[s % 2, t])
                for t in range(NSA_BLOCK)]

    @pl.when(step == 0)
    def _():
        for cp in block_copies(step):
            cp.start()

    @pl.when(step + 1 < total)
    def _():
        for cp in block_copies(step + 1):
            cp.start()

    for cp in block_copies(step):
        cp.wait()
    acc = jnp.zeros((tr, w1_ref.shape[1]), F32)
    for t in range(NSA_BLOCK):
        xt = (xbuf[step % 2, t] + pe_ref[t:t + 1, :]).astype(BF16)
        acc = acc + _dot(xt, w1_ref[t * HEAD_DIM:(t + 1) * HEAD_DIM, :])
    hid = _gelu(acc + b1_ref[...])
    o_ref[...] = _dot(hid.astype(BF16), w2_ref[...].astype(BF16)) + b2_ref[...]


def _compress(x, pe, w1, b1, w2, b2, tr):
    r = x.shape[0]
    tr = min(tr, r)
    hidden = w1.shape[2]
    return pl.pallas_call(
        functools.partial(_compress_kernel, tr=tr),
        grid=(2, NSA_KV_GROUPS, r // tr),
        in_specs=[
            pl.BlockSpec(memory_space=pl.ANY),
            pl.BlockSpec((None, NSA_BLOCK, HEAD_DIM), lambda s, g, i: (s, 0, 0)),
            pl.BlockSpec((None, NSA_BLOCK * HEAD_DIM, hidden), lambda s, g, i: (s, 0, 0)),
            pl.BlockSpec((None, 1, hidden), lambda s, g, i: (s, 0, 0)),
            pl.BlockSpec((None, hidden, HEAD_DIM), lambda s, g, i: (s, 0, 0)),
            pl.BlockSpec((None, 1, HEAD_DIM), lambda s, g, i: (s, 0, 0)),
        ],
        out_specs=pl.BlockSpec((None, tr, HEAD_DIM), lambda s, g, i: (s, i, g)),
        out_shape=jax.ShapeDtypeStruct((2, r, NSA_KV), F32),
        scratch_shapes=[pltpu.VMEM((2, NSA_BLOCK, tr, HEAD_DIM), F32), pltpu.SemaphoreType.DMA((2, NSA_BLOCK))],
        compiler_params=_params(("arbitrary", "arbitrary", "arbitrary"), 48),
        name="nsa_compress",
    )(x, pe, w1, b1.reshape(2, 1, hidden), w2, b2.reshape(2, 1, HEAD_DIM))


def _page_gather_kernel(pt_ref, *refs):
    o_ref = refs[-1]
    for k, r in enumerate(refs[:-1]):
        for s in range(2):
            for g in range(NSA_KV_GROUPS):
                lanes = slice((s * NSA_KV_GROUPS + g) * HEAD_DIM, (s * NSA_KV_GROUPS + g + 1) * HEAD_DIM)
                o_ref[k * PAGE_SIZE:(k + 1) * PAGE_SIZE, lanes] = r[:, s, g, :]


def _gather_cmp_pages(cache_kv, page_table, pages_per_step=8):
    bd, n_pages = page_table.shape
    width = 2 * NSA_KV
    pps = pages_per_step

    def in_map(k):
        return lambda b, c, pt: (pt[b * n_pages + c * pps + k], 0, 0, 0, 0)

    return pl.pallas_call(
        _page_gather_kernel,
        grid_spec=pltpu.PrefetchScalarGridSpec(
            num_scalar_prefetch=1,
            grid=(bd, n_pages // pps),
            in_specs=[pl.BlockSpec((None, PAGE_SIZE, 2, NSA_KV_GROUPS, HEAD_DIM), in_map(k)) for k in range(pps)],
            out_specs=pl.BlockSpec((None, pps * PAGE_SIZE, width), lambda b, c, pt: (b, c, 0))),
        out_shape=jax.ShapeDtypeStruct((bd, n_pages * PAGE_SIZE, width), F32),
        compiler_params=_params(("arbitrary", "arbitrary"), 40),
        name="nsa_page_gather",
    )(page_table.reshape(-1), *([cache_kv] * pps))


def _online_softmax_step(z, v_t, m_ref, l_ref, acc_ref):
    m_new = jnp.maximum(m_ref[...], jnp.max(z, axis=0, keepdims=True))
    alpha = jnp.exp(m_ref[...] - m_new)
    p = jnp.exp(z - m_new)
    l_ref[...] = alpha * l_ref[...] + jnp.sum(p, axis=0, keepdims=True)
    acc_ref[...] = alpha * acc_ref[...] + _dot(v_t, p.astype(BF16))
    m_ref[...] = m_new


def _nsa_prompt_kernel(q_ref, kc_ref, vc_ref, ks_ref, vs_ref, kw_ref, vw_ref, gate_ref, o_ref,
                       m_ref, l_ref, acc_ref, osel_ref, gate_t_ref, *, nb, tq, tk):
    g = pl.program_id(1)
    i = pl.program_id(2)
    nq = NSA_GROUP * tq
    nbp = kc_ref.shape[0]
    qblk = q_ref[...]
    qg = jnp.concatenate([qblk[:, r * HEAD_DIM:(r + 1) * HEAD_DIM] for r in range(NSA_GROUP)], axis=0).astype(BF16)
    qpos = i * tq + lax.broadcasted_iota(jnp.int32, (1, tq), 1)
    tile4 = lambda x: jnp.concatenate([x] * NSA_GROUP, axis=1)

    blk = lax.broadcasted_iota(jnp.int32, (nbp, 1), 0)
    zc = _dot_nt(kc_ref[...].astype(BF16), qg) * ATTN_SCALE
    avail = tile4(jnp.logical_and((blk + 1) * NSA_BLOCK - 1 <= qpos, blk < nb))
    zm = jnp.where(avail, zc, NEG_INF)
    e = jnp.where(avail, jnp.exp(zm - jnp.max(zm, axis=0, keepdims=True)), 0.0)
    den = jnp.sum(e, axis=0, keepdims=True)
    pc = e / jnp.where(den > 0.0, den, 1.0)
    o_cmp = _dot(jnp.transpose(vc_ref[...]).astype(BF16), pc.astype(BF16))

    score = pc[:, 0:tq]
    for r in range(1, NSA_GROUP):
        score = score + pc[:, r * tq:(r + 1) * tq]
    nbr = -(-nb // 8) * 8
    blk = blk[0:nbr]
    cur = qpos // NSA_BLOCK
    forced = jnp.logical_or(blk == 0, jnp.logical_or(blk == cur, blk == cur - 1))
    s = jnp.where(blk <= cur, jnp.where(forced, FORCE_SCORE, score[0:nbr]), -1.0)
    rank = jnp.zeros((nbr, tq), F32)
    for mth in range(nb):
        sm = s[mth:mth + 1, :]
        ahead = jnp.logical_or(sm > s, jnp.logical_and(sm == s, blk > mth))
        rank = rank + jnp.where(ahead, 1.0, 0.0)
    sel = jnp.where(jnp.logical_and(rank < float(min(NSA_TOPN, nb)), s >= 0.0), 1.0, 0.0).astype(BF16)

    def reset():
        m_ref[...] = jnp.full_like(m_ref, NEG_INF)
        l_ref[...] = jnp.zeros_like(l_ref)
        acc_ref[...] = jnp.zeros_like(acc_ref)

    reset()
    bpt = tk // NSA_BLOCK

    def sel_body(kt, carry):
        k0 = pl.multiple_of(kt * tk, tk)
        z = _dot_nt(ks_ref[pl.ds(k0, tk), :].astype(BF16), qg) * ATTN_SCALE
        key = lax.broadcasted_iota(jnp.int32, (tk, nbr), 0)
        col = lax.broadcasted_iota(jnp.int32, (tk, nbr), 1)
        expand = (col == kt * bpt + key // NSA_BLOCK).astype(BF16)
        chosen = _dot(expand, sel) > 0.5
        kpos = k0 + lax.broadcasted_iota(jnp.int32, (tk, 1), 0)
        bias = jnp.where(jnp.logical_and(chosen, kpos <= qpos), 0.0, NEG_INF)
        v_t = jnp.transpose(vs_ref[pl.ds(k0, tk), :]).astype(BF16)
        _online_softmax_step(z + tile4(bias), v_t, m_ref, l_ref, acc_ref)
        return carry

    lax.fori_loop(0, (i * tq + tq - 1) // tk + 1, sel_body, 0)
    osel_ref[...] = acc_ref[...] / l_ref[...]

    reset()

    def win_body(kt, carry):
        k0 = pl.multiple_of(kt * tq, tq)
        z = _dot_nt(kw_ref[pl.ds(k0, tq), :].astype(BF16), qg) * ATTN_SCALE
        dist = qpos - (k0 + lax.broadcasted_iota(jnp.int32, (tq, 1), 0))
        bias = jnp.where(jnp.logical_and(dist >= 0, dist <= NSA_WINDOW), 0.0, NEG_INF)
        v_t = jnp.transpose(vw_ref[pl.ds(k0, tq), :]).astype(BF16)
        _online_softmax_step(z + tile4(bias), v_t, m_ref, l_ref, acc_ref)
        return carry

    lax.fori_loop(jnp.maximum(i - NSA_WINDOW // tq, 0), i + 1, win_body, 0)
    o_win = acc_ref[...] / l_ref[...]
    o_sel = osel_ref[...]

    gate_t_ref[...] = jnp.transpose(gate_ref[...])
    for r in range(NSA_GROUP):
        lanes = slice(r * tq, (r + 1) * tq)
        out_t = jnp.zeros((HEAD_DIM, tq), F32)
        for c, o_c in enumerate((o_cmp, o_sel, o_win)):
            gate = gate_t_ref[pl.ds(c * N_HEADS + g * NSA_GROUP + r, 1), :]
            out_t = out_t + gate * o_c[:, lanes]
        o_ref[:, r * HEAD_DIM:(r + 1) * HEAD_DIM] = jnp.transpose(out_t)


def _nsa_prompt_attention(q, kv4, kwin, gates, kc, vc, batch, seq, nb):
    tq = 512
    tk = 512
    nq = seq // tq
    g4 = NSA_KV_GROUPS
    col = lambda off: pl.BlockSpec((seq, HEAD_DIM), lambda b, g, i: (b, off + g))
    cmp_spec = pl.BlockSpec((None, kc.shape[1], HEAD_DIM), lambda b, g, i: (b, 0, g))
    return pl.pallas_call(
        functools.partial(_nsa_prompt_kernel, nb=nb, tq=tq, tk=tk),
        grid=(batch, g4, nq),
        in_specs=[
            pl.BlockSpec((tq, NSA_KV), lambda b, g, i: (b * nq + i, g)),
            cmp_spec, cmp_spec,
            col(2 * g4), col(3 * g4),
            col(0), col(g4),
            pl.BlockSpec((tq, LANES), lambda b, g, i: (b * nq + i, 0)),
        ],
        out_specs=pl.BlockSpec((tq, NSA_KV), lambda b, g, i: (b * nq + i, g)),
        out_shape=jax.ShapeDtypeStruct(q.shape, F32),
        scratch_shapes=[pltpu.VMEM((1, NSA_GROUP * tq), F32), pltpu.VMEM((1, NSA_GROUP * tq), F32),
                        pltpu.VMEM((HEAD_DIM, NSA_GROUP * tq), F32), pltpu.VMEM((HEAD_DIM, NSA_GROUP * tq), F32),
                        pltpu.VMEM((LANES, tq), F32)],
        compiler_params=_params(("arbitrary", "arbitrary", "arbitrary"), 40),
        name="nsa_prompt_attention",
    )(q, kc, vc, kv4, kv4, kwin, kwin, gates)


def _nsa_sample_select_kernel(q_ref, kc_ref, vc_ref, ocmp_ref, idx_ref, *, nb, qpos):
    nbp = kc_ref.shape[0]
    q = q_ref[...]
    lane_blk = lax.broadcasted_iota(jnp.int32, (1, nbp), 1)
    row_i = lax.broadcasted_iota(jnp.int32, (nbp, nbp), 0)
    col_i = lax.broadcasted_iota(jnp.int32, (nbp, nbp), 1)
    eye = row_i == col_i
    cur = qpos // NSA_BLOCK
    avail = jnp.logical_and((lane_blk + 1) * NSA_BLOCK - 1 <= qpos, lane_blk < nb)
    forced = jnp.logical_or(lane_blk == 0, jnp.logical_or(lane_blk == cur, lane_blk == cur - 1))
    for g in range(NSA_KV_GROUPS):
        qg = jnp.concatenate(
            [q[:, (g * NSA_GROUP + r) * HEAD_DIM:(g * NSA_GROUP + r + 1) * HEAD_DIM] for r in range(NSA_GROUP)],
            axis=0).astype(BF16)
        lanes = slice(g * HEAD_DIM, (g + 1) * HEAD_DIM)
        zc = _dot_nt(qg, kc_ref[:, lanes].astype(BF16)) * ATTN_SCALE
        zm = jnp.where(avail, zc, NEG_INF)
        e = jnp.where(avail, jnp.exp(zm - jnp.max(zm, axis=1, keepdims=True)), 0.0)
        den = jnp.sum(e, axis=1, keepdims=True)
        pc = e / jnp.where(den > 0.0, den, 1.0)
        o = _dot(pc.astype(BF16), vc_ref[:, lanes].astype(BF16))
        for r in range(NSA_GROUP):
            h = g * NSA_GROUP + r
            ocmp_ref[:, h * HEAD_DIM:(h + 1) * HEAD_DIM] = o[r:r + 1, :]
        score = jnp.sum(pc, axis=0, keepdims=True)
        s_row = jnp.where(lane_blk <= cur, jnp.where(forced, FORCE_SCORE, score), -1.0)
        s_mat = jnp.broadcast_to(s_row, (nbp, nbp))
        s_col = jnp.sum(jnp.where(eye, s_mat, 0.0), axis=1, keepdims=True)
        ahead = jnp.logical_or(s_mat > s_col, jnp.logical_and(s_mat == s_col, col_i < row_i))
        rank = jnp.sum(jnp.where(ahead, 1.0, 0.0), axis=1, keepdims=True)
        sel_col = jnp.where(jnp.logical_and(rank < float(min(NSA_TOPN, nb)), s_col >= 0.0), 1.0, 0.0)
        sel_row = jnp.sum(jnp.where(eye, jnp.broadcast_to(sel_col, (nbp, nbp)), 0.0), axis=0, keepdims=True)
        slot = jnp.sum(jnp.where(col_i < row_i, jnp.broadcast_to(sel_row, (nbp, nbp)), 0.0), axis=1, keepdims=True)
        k_i = lax.broadcasted_iota(jnp.int32, (nbp, LANES), 1).astype(F32)
        n_i = lax.broadcasted_iota(jnp.int32, (nbp, LANES), 0).astype(F32)
        hit = jnp.logical_and(sel_col > 0.5, slot == k_i)
        idx_ref[g:g + 1, :] = jnp.sum(jnp.where(hit, n_i, 0.0), axis=0, keepdims=True).astype(jnp.int32)


def _nsa_sample_select(q, kc, vc, nb, qpos):
    bd = q.shape[0]
    nbp = kc.shape[1]
    cmp_spec = pl.BlockSpec((None, nbp, NSA_KV), lambda b: (b, 0, 0))
    ocmp, idx = pl.pallas_call(
        functools.partial(_nsa_sample_select_kernel, nb=nb, qpos=qpos),
        grid=(bd,),
        in_specs=[pl.BlockSpec((None, 1, D_MODEL), lambda b: (b, 0, 0)), cmp_spec, cmp_spec],
        out_specs=[pl.BlockSpec((None, 1, D_MODEL), lambda b: (b, 0, 0)),
                   pl.BlockSpec((None, NSA_KV_GROUPS, LANES), lambda b: (b, 0, 0))],
        out_shape=[jax.ShapeDtypeStruct((bd, 1, D_MODEL), F32),
                   jax.ShapeDtypeStruct((bd, NSA_KV_GROUPS, LANES), jnp.int32)],
        compiler_params=_params(("arbitrary",), 32),
        name="nsa_sample_select",
    )(q.reshape(bd, 1, D_MODEL), kc, vc)
    return ocmp, idx[:, :, :NSA_TOPN]


def _nsa_sample_attend_kernel(idx_ref, pt_ref, q_ref, ocmp_ref, gate_ref, kv_ref, kwin_ref, win_ref, cache_ref,
                              o_ref, kbuf, vbuf, sem, *, n_pages, n_past_blocks):
    b = pl.program_id(0)
    g4 = NSA_KV_GROUPS
    bpp = PAGE_SIZE // NSA_BLOCK

    def block_copies(g, k):
        blk = jnp.minimum(idx_ref[(b * g4 + g) * NSA_TOPN + k], n_past_blocks - 1)
        page = pt_ref[b * n_pages + blk // bpp]
        rows = pl.ds((blk % bpp) * NSA_BLOCK, NSA_BLOCK)
        dst = pl.ds(k * NSA_BLOCK, NSA_BLOCK)
        return (pltpu.make_async_copy(cache_ref.at[page, rows, 2], kbuf.at[g % 2, dst], sem.at[0, g % 2, k]),
                pltpu.make_async_copy(cache_ref.at[page, rows, 3], vbuf.at[g % 2, dst], sem.at[1, g % 2, k]))

    def start_group(g):
        for k in range(NSA_TOPN):
            ck, cv = block_copies(g, k)
            ck.start()
            cv.start()

    def wait_group(g):
        for k in range(NSA_TOPN):
            ck, cv = block_copies(g, k)
            ck.wait()
            cv.wait()

    start_group(0)

    q = q_ref[...]
    ocmp = ocmp_ref[...]
    gates = gate_ref[...]
    kv_new = kv_ref[...]
    kwin_new = kwin_ref[...]
    n_sel = NSA_TOPN * NSA_BLOCK
    key_slot = lax.broadcasted_iota(jnp.int32, (1, n_sel), 1) // NSA_BLOCK
    lane = lax.broadcasted_iota(jnp.int32, (NSA_GROUP, LANES), 1)
    sub = lax.broadcasted_iota(jnp.int32, (NSA_GROUP, LANES), 0)
    bf = lambda x: x.astype(BF16).astype(F32)

    def attend(qg, keys, vals, valid, k_new, v_new, new_on):
        z = _dot_nt(qg, keys.astype(BF16)) * ATTN_SCALE
        z_new = jnp.sum(bf(qg) * bf(k_new), axis=1, keepdims=True) * ATTN_SCALE
        zm = z if valid is None else jnp.where(valid, z, NEG_INF)
        zn = jnp.where(new_on, z_new, NEG_INF)
        m = jnp.maximum(jnp.max(zm, axis=1, keepdims=True), zn)
        p = jnp.exp(zm - m) if valid is None else jnp.where(valid, jnp.exp(zm - m), 0.0)
        p_new = jnp.where(new_on, jnp.exp(zn - m), 0.0)
        den = jnp.sum(p, axis=1, keepdims=True) + p_new
        return (_dot(p.astype(BF16), vals.astype(BF16)) + bf(p_new) * bf(v_new)) / den

    for g in range(g4):
        if g + 1 < g4:
            start_group(g + 1)
        wait_group(g)
        qg = jnp.concatenate(
            [q[:, (g * NSA_GROUP + r) * HEAD_DIM:(g * NSA_GROUP + r + 1) * HEAD_DIM] for r in range(NSA_GROUP)],
            axis=0).astype(BF16)
        o_cmp = jnp.concatenate(
            [ocmp[:, (g * NSA_GROUP + r) * HEAD_DIM:(g * NSA_GROUP + r + 1) * HEAD_DIM] for r in range(NSA_GROUP)],
            axis=0)
        valid = jnp.zeros((1, n_sel), jnp.int32)
        has_new = jnp.int32(0)
        for k in range(NSA_TOPN):
            is_new = (idx_ref[(b * g4 + g) * NSA_TOPN + k] >= n_past_blocks).astype(jnp.int32)
            valid = jnp.where(key_slot == k, 1 - is_new, valid)
            has_new = jnp.maximum(has_new, is_new)
        lanes = lambda slot: slice((slot * g4 + g) * HEAD_DIM, (slot * g4 + g + 1) * HEAD_DIM)
        o_sel = attend(qg, kbuf[g % 2, :, g, :], vbuf[g % 2, :, g, :], valid > 0,
                       kv_new[:, lanes(2)], kv_new[:, lanes(3)], has_new > 0)
        o_win = attend(qg, win_ref[:, 0, g, :], win_ref[:, 1, g, :], None,
                       kwin_new[:, lanes(0)], kwin_new[:, lanes(1)], True)
        out = jnp.zeros((NSA_GROUP, HEAD_DIM), F32)
        for c, o_c in enumerate((o_cmp, o_sel, o_win)):
            pick = lane == c * N_HEADS + g * NSA_GROUP + sub
            gate = jnp.sum(jnp.where(pick, jnp.broadcast_to(gates, (NSA_GROUP, LANES)), 0.0), axis=1, keepdims=True)
            out = out + gate * o_c
        for r in range(NSA_GROUP):
            h = g * NSA_GROUP + r
            o_ref[:, h * HEAD_DIM:(h + 1) * HEAD_DIM] = out[r:r + 1, :]


def _nsa_sample_attend(idx, page_table, q, ocmp, gates, kv4, kwin, win_state, cache_kv):
    bd, n_pages = page_table.shape
    g4 = NSA_KV_GROUPS
    sel_buf = pltpu.VMEM((2, NSA_TOPN * NSA_BLOCK, g4, HEAD_DIM), F32)
    row = lambda w: pl.BlockSpec((None, 1, w), lambda b, *_: (b, 0, 0))
    out = pl.pallas_call(
        functools.partial(_nsa_sample_attend_kernel, n_pages=n_pages, n_past_blocks=n_pages * PAGE_SIZE // NSA_BLOCK),
        grid_spec=pltpu.PrefetchScalarGridSpec(
            num_scalar_prefetch=2,
            grid=(bd,),
            in_specs=[row(D_MODEL), row(D_MODEL), row(LANES), row(4 * NSA_KV), row(2 * NSA_KV),
                      pl.BlockSpec((None,) + win_state.shape[1:], lambda b, *_: (b, 0, 0, 0, 0)),
                      pl.BlockSpec(memory_space=pl.ANY)],
            out_specs=row(D_MODEL),
            scratch_shapes=[sel_buf, sel_buf, pltpu.SemaphoreType.DMA((2, 2, NSA_TOPN))]),
        out_shape=jax.ShapeDtypeStruct((bd, 1, D_MODEL), F32),
        compiler_params=_params(("arbitrary",), 40),
        name="nsa_sample_attend",
    )(idx.reshape(-1), page_table.reshape(-1), q.reshape(bd, 1, D_MODEL), ocmp, gates.reshape(bd, 1, LANES),
      kv4.reshape(bd, 1, 4 * NSA_KV), kwin.reshape(bd, 1, 2 * NSA_KV), win_state, cache_kv)
    return out.reshape(bd, D_MODEL)


def _topk_rows(s, k, ids=None):
    if ids is None:
        ids = lax.broadcasted_iota(jnp.int32, s.shape, 0).astype(F32)
    vals, picked = [], []
    for _ in range(k):
        m = jnp.max(s, axis=0, keepdims=True)
        first = jnp.min(jnp.where(s == m, ids, jnp.inf), axis=0, keepdims=True)
        s = jnp.where(ids == first, -jnp.inf, s)
        vals.append(m)
        picked.append(first)
    return jnp.concatenate(vals, axis=0), jnp.concatenate(picked, axis=0)


def _pair_candidates(v1, v2):
    tokens = v1.shape[1]
    row = lambda n: lax.broadcasted_iota(jnp.int32, (n, tokens), 0).astype(F32)
    vals, ids = [], []
    for k1 in range(4):
        n = 16 if k1 == 0 else 8
        k2 = row(n)
        vals.append(jnp.where((k1 + 1) * (k2 + 1.0) <= PEER_TOPK, v1[k1:k1 + 1, :] + v2[0:n, :], -jnp.inf))
        ids.append(k1 * PEER_TOPK + k2)
    for k2 in range(3):
        n = 16 if k2 == 0 else 8
        k1 = row(n)
        keep = jnp.logical_and(k1 >= 4.0, (k1 + 1.0) * (k2 + 1) <= PEER_TOPK)
        vals.append(jnp.where(keep, v1[0:n, :] + v2[k2:k2 + 1, :], -jnp.inf))
        ids.append(k1 * PEER_TOPK + k2)
    return jnp.concatenate(vals, axis=0), jnp.concatenate(ids, axis=0)


def _pick_rows(sel, table):
    out = jnp.zeros_like(sel)
    for k in range(table.shape[0]):
        out = jnp.where(sel == float(k), table[k:k + 1, :], out)
    return out


def _peer_route_kernel(h_ref, wq_ref, k1_ref, k2_ref, i1_ref, i2_ref, g_ref, qh_ref, i1_t, i2_t, g_t):
    qh_ref[...] = _dot(h_ref[...], wq_ref[...])
    k1 = k1_ref[...].astype(BF16)
    k2 = k2_ref[...].astype(BF16)
    half = PEER_NKEYS

    def head(h, carry):
        c0 = pl.multiple_of(h * 2 * half, 2 * half)
        s1 = _dot_nt(k1, qh_ref[:, pl.ds(c0, half)].astype(BF16))
        s2 = _dot_nt(k2, qh_ref[:, pl.ds(c0 + half, half)].astype(BF16))
        v1, i1 = _topk_rows(s1, PEER_TOPK)
        v2, i2 = _topk_rows(s2, PEER_TOPK)
        cand, cand_id = _pair_candidates(v1, v2)
        sc, j = _topk_rows(cand, PEER_TOPK, cand_id)
        ja = jnp.floor(j * (1.0 / PEER_TOPK))
        jb = j - ja * PEER_TOPK
        e = jnp.exp(sc - jnp.max(sc, axis=0, keepdims=True))
        r0 = pl.multiple_of(h * PEER_TOPK, PEER_TOPK)
        i1_t[pl.ds(r0, PEER_TOPK), :] = _pick_rows(ja, i1)
        i2_t[pl.ds(r0, PEER_TOPK), :] = _pick_rows(jb, i2)
        g_t[pl.ds(r0, PEER_TOPK), :] = e / jnp.sum(e, axis=0, keepdims=True)
        return carry

    lax.fori_loop(0, PEER_HEADS, head, 0)
    i1_ref[...] = jnp.transpose(i1_t[...])
    i2_ref[...] = jnp.transpose(i2_t[...])
    g_ref[...] = jnp.transpose(g_t[...])


def _peer_route(h, wq, k1, k2, tm):
    m, d = h.shape
    tm = min(tm, m)
    hk = PEER_HEADS * PEER_TOPK
    out = pl.BlockSpec((tm, hk), lambda i: (i, 0))
    keys = pl.BlockSpec(k1.shape, lambda i: (0, 0))
    return pl.pallas_call(
        _peer_route_kernel,
        grid=(m // tm,),
        in_specs=[pl.BlockSpec((tm, d), lambda i: (i, 0)), pl.BlockSpec(wq.shape, lambda i: (0, 0)), keys, keys],
        out_specs=[out, out, out],
        out_shape=[jax.ShapeDtypeStruct((m, hk), F32)] * 3,
        scratch_shapes=[pltpu.VMEM((tm, wq.shape[1]), F32)] + [pltpu.VMEM((hk, tm), F32)] * 3,
        compiler_params=_params(("arbitrary",), 40),
        name="peer_route",
    )(h, wq, k1, k2)


PEER_TOKENS_PER_TRIP = 16


def _peer_weights_kernel(i1_ref, i2_ref, g_ref, w_ref):
    tm = i1_ref.shape[0]
    hk = i1_ref.shape[1]
    key = lax.broadcasted_iota(jnp.int32, (PEER_NKEYS, hk), 0).astype(F32)

    def body(step, carry):
        for r in range(PEER_TOKENS_PER_TRIP):
            t = step * PEER_TOKENS_PER_TRIP + r
            i1 = jnp.broadcast_to(i1_ref[pl.ds(t, 1), :], (PEER_NKEYS, hk))
            i2 = jnp.broadcast_to(i2_ref[pl.ds(t, 1), :], (PEER_NKEYS, hk))
            gt = jnp.broadcast_to(g_ref[pl.ds(t, 1), :], (PEER_NKEYS, hk))
            p = jnp.where(i1 == key, 1.0, 0.0).astype(BF16)
            q = jnp.where(i2 == key, gt, 0.0).astype(BF16)
            w_ref[t] = _dot_nt(p, q)
        return carry

    lax.fori_loop(0, tm // PEER_TOKENS_PER_TRIP, body, 0)


def _peer_weights(i1, i2, g, tm):
    m, hk = i1.shape
    tm = min(tm, m)
    spec = pl.BlockSpec((tm, hk), lambda i: (i, 0))
    return pl.pallas_call(
        _peer_weights_kernel,
        grid=(m // tm,),
        in_specs=[spec, spec, spec],
        out_specs=pl.BlockSpec((tm, PEER_NKEYS, PEER_NKEYS), lambda i: (i, 0, 0)),
        out_shape=jax.ShapeDtypeStruct((m, PEER_NKEYS, PEER_NKEYS), F32),
        compiler_params=_params(("arbitrary",), 40),
        name="peer_weights",
    )(i1, i2, g)


def _peer_dense_kernel(h_ref, u_ref, v_ref, x_ref, gate_ref, lng_ref, lnb_ref, scale_ref, shift_ref, w_hbm,
                       xo_ref, ho_ref, acc_ref, wbuf, wsem, *, n_a):
    i = pl.program_id(0)
    c = pl.program_id(1)
    nc = pl.num_programs(1)
    step = i * nc + c
    tm = h_ref.shape[0]

    def map_copies(s):
        rows = pl.ds((s // nc) * tm, tm)
        return [pltpu.make_async_copy(w_hbm.at[rows, (s % nc) * n_a + k, :], wbuf.at[s % 2, k], wsem.at[s % 2, k])
                for k in range(n_a)]

    @pl.when(step == 0)
    def _():
        for cp in map_copies(step):
            cp.start()

    @pl.when(step + 1 < pl.num_programs(0) * nc)
    def _():
        for cp in map_copies(step + 1):
            cp.start()

    @pl.when(c == 0)
    def _():
        acc_ref[...] = jnp.zeros_like(acc_ref)

    act = _gelu(_dot_nt(h_ref[...], u_ref[...]))
    for cp in map_copies(step):
        cp.wait()
    weighted = [act[:, k * PEER_NKEYS:(k + 1) * PEER_NKEYS] * wbuf[step % 2, k] for k in range(n_a)]
    acc_ref[...] += _dot(jnp.concatenate(weighted, axis=1).astype(BF16), v_ref[...])

    @pl.when(c == pl.num_programs(1) - 1)
    def _():
        xn, h = _ln_res(acc_ref[...], x_ref[...], gate_ref[...], lng_ref[...], lnb_ref[...],
                        scale_ref[...], shift_ref[...])
        xo_ref[...] = xn
        ho_ref[...] = h


def _peer_dense(h, u, v, layer, w, xres, gate, lng, lnb, scale, shift, tm, te):
    m, d = h.shape
    n_exp = u.shape[1]
    tm = min(tm, m)
    bpb = (m // tm) // gate.shape[0]
    n_a = te // PEER_NKEYS
    row = pl.BlockSpec((tm, d), lambda i, c: (i, 0))
    tab = pl.BlockSpec((None, te, d), lambda i, c: (layer, c, 0))
    vec = pl.BlockSpec((1, d), lambda i, c: (0, 0))
    return pl.pallas_call(
        functools.partial(_peer_dense_kernel, n_a=n_a),
        grid=(m // tm, n_exp // te),
        in_specs=[row, tab, tab, row, _mod_spec(gate, bpb), vec, vec, _mod_spec(scale, bpb), _mod_spec(shift, bpb),
                  pl.BlockSpec(memory_space=pl.ANY)],
        out_specs=[row, row],
        out_shape=[jax.ShapeDtypeStruct((m, d), F32), jax.ShapeDtypeStruct((m, d), BF16)],
        scratch_shapes=[pltpu.VMEM((tm, d), F32), pltpu.VMEM((2, n_a, tm, PEER_NKEYS), F32),
                        pltpu.SemaphoreType.DMA((2, n_a))],
        compiler_params=_params(("arbitrary", "arbitrary"), 56),
        name="peer_dense",
    )(h, u, v, xres, gate, lng, lnb, scale, shift, w)


def _peer(h, xres, wq, k1, k2, u, v, layer, gate, lng, lnb, scale, shift, tm_route, tm_dense, te):
    i1, i2, g = _peer_route(h, wq, k1, k2, tm_route)
    w = _peer_weights(i1, i2, g, 128)
    return _peer_dense(h, u, v, layer, w, xres, gate, lng, lnb, scale, shift, tm_dense, te)


def _rope_tables(pos):
    half = ROPE_DIMS // 2
    inv = jnp.power(ROPE_THETA, -jnp.arange(half, dtype=F32) / half)
    ang = pos.astype(F32)[:, None] * inv
    cos, sin = jnp.cos(ang), jnp.sin(ang)
    n = pos.shape[0]
    ones = jnp.ones((n, HEAD_DIM - ROPE_DIMS), F32)
    zeros = jnp.zeros((n, HEAD_DIM - ROPE_DIMS), F32)
    zh = jnp.zeros((n, half), F32)
    return (jnp.concatenate([cos, cos, ones], axis=1),
            jnp.concatenate([-sin, zh, zeros], axis=1),
            jnp.concatenate([zh, sin, zeros], axis=1))


def _pad_rows(x, rows):
    return jnp.pad(x, ((0, rows - x.shape[0]),) + ((0, 0),) * (x.ndim - 1))


def kernel(x_prompt, x_sample, c_prompt, c_sample, cache_l0_kv, cache_l1_kv, state_l1_win, cache_l2_kv, cache_l3_kv,
           state_l3_win, page_table, ada_w, ada_b, ln1_g, ln1_b, ln2_g, ln2_b, sb_w_in, sb_w_o, nsa_w_in, nsa_w_o,
           nsa_cmp_pe, nsa_cmp_w1, nsa_cmp_b1, nsa_cmp_w2, nsa_cmp_b2, peer_wq, peer_k1, peer_k2, peer_u, peer_v):
    batch, seq, d = x_prompt.shape
    bd, dec_seq, _ = x_sample.shape
    assert dec_seq == 1 and d == D_MODEL
    depth = ada_w.shape[0]
    n_pages = page_table.shape[1]
    past = n_pages * PAGE_SIZE
    sb_caches = (cache_l0_kv, cache_l2_kv)
    nsa_caches = (cache_l1_kv, cache_l3_kv)
    nsa_wins = (state_l1_win, state_l3_win)
    win_buf = state_l1_win.shape[1]
    assert win_buf == NSA_WINDOW and past % NSA_BLOCK == 0 and seq >= win_buf
    mp_rows = batch * seq
    tm = 512

    n_c = batch + bd
    c_all = _pad_rows(jnp.concatenate([c_prompt, c_sample], axis=0), -(-n_c // 8) * 8)
    mod = _modulation(c_all, ada_w, ada_b).reshape(depth, c_all.shape[0], 6, d)
    mod_p = lambda i, k: mod[i, :batch, k].reshape(batch, 1, d)
    mod_s = lambda i, k: mod[i, batch:n_c, k].reshape(1, bd, d)

    xp = x_prompt.reshape(mp_rows, d)
    xs = x_sample.reshape(bd, d)
    hp = _modulate(xp, mod_p(0, 1), mod_p(0, 0), tm)
    hs = _modulate(xs, mod_s(0, 1), mod_s(0, 0), bd)

    rope_p = _rope_tables(jnp.arange(seq))
    rope_s = _rope_tables(jnp.full((bd,), past))
    nb_p = -(-seq // NSA_BLOCK)
    assert nb_p * NSA_BLOCK == seq
    nb_s = past // NSA_BLOCK + 1

    u_all = peer_u.astype(BF16)
    v_all = peer_v.astype(BF16)
    new = []
    for i in range(depth):
        j = i // 2
        vec = lambda a: a[i].reshape(1, d)
        if i % 2 == 0:
            w_in = sb_w_in[j].astype(BF16)
            w_o = sb_w_o[j].astype(BF16)
            sb_rows = (2, N_HEADS)
            q_p = _matmul(hp, w_in[:, :d], tm, 512, out_dtype=BF16, name="sb_q")
            kv_p, kv_rows_p = _matmul(hp, w_in[:, d:], tm, 1024, out_dtype=BF16, rows_out=sb_rows, name="sb_kv")
            o_p = _sb_prompt_attention(q_p, kv_p, batch, seq)
            q_s = _matmul(hs, w_in[:, :d], bd, 512, out_dtype=BF16, name="sb_q")
            _, kv_rows_s = _matmul(hs, w_in[:, d:], bd, 1024, out_dtype=BF16, rows_out=sb_rows, name="sb_kv")
            o_s = _sb_sample_attention(q_s, sb_caches[j], page_table)
            new.append((kv_rows_p.reshape(batch, seq, 2, N_HEADS, HEAD_DIM),
                        kv_rows_s.reshape(bd, 1, 2, N_HEADS, HEAD_DIM)))
        else:
            w_in = nsa_w_in[j].astype(BF16)
            w_o = nsa_w_o[j].astype(BF16)
            w_q = w_in[:, :d]
            w_kv4 = w_in[:, d:d + 4 * NSA_KV]
            w_kw = w_in[:, d + 4 * NSA_KV:d + 6 * NSA_KV]
            w_g = jnp.pad(w_in[:, d + 6 * NSA_KV:], ((0, 0), (0, LANES - 3 * N_HEADS)))
            cmp_w = (nsa_cmp_pe[j], nsa_cmp_w1[j].astype(BF16), nsa_cmp_b1[j], nsa_cmp_w2[j], nsa_cmp_b2[j])

            def project(h, rope, rows):
                kv4, kv4_rows = _matmul(h, w_kv4, rows, NSA_KV, epilogue="rope_even", rope=rope,
                                        rows_out=(4, NSA_KV_GROUPS), name="nsa_kv4")
                kwin, kwin_rows = _matmul(h, w_kw, rows, NSA_KV, epilogue="rope_even", rope=rope,
                                          rows_out=(2, NSA_KV_GROUPS), name="nsa_kwin")
                return (_matmul(h, w_q, rows, 512, epilogue="rope_all", rope=rope, name="nsa_q"), kv4, kwin,
                        _matmul(h, w_g, rows, LANES, epilogue="sigmoid", name="nsa_gates"), kv4_rows, kwin_rows)

            q_p, kv4_p, kwin_p, gates_p, kv4_rows_p, kwin_rows_p = project(hp, rope_p, tm)
            cmp_p = _compress(kv4_p.reshape(batch * nb_p, NSA_BLOCK, 4 * NSA_KV), *cmp_w, 256)
            nbp = -(-nb_p // LANES) * LANES
            cmp_p = jnp.pad(cmp_p.reshape(2, batch, nb_p, NSA_KV), ((0, 0), (0, 0), (0, nbp - nb_p), (0, 0)))
            o_p = _nsa_prompt_attention(q_p, kv4_p, kwin_p, gates_p, cmp_p[0], cmp_p[1], batch, seq, nb_p)
            win_p = kwin_rows_p.reshape(batch, seq, 2, NSA_KV_GROUPS, HEAD_DIM)[:, seq - win_buf:]
            q_s, kv4_s, kwin_s, gates_s, kv4_rows_s, kwin_rows_s = project(hs, rope_s, bd)
            x_past = _gather_cmp_pages(nsa_caches[j], page_table)
            cmp_past = _compress(x_past.reshape(bd * (nb_s - 1), NSA_BLOCK, 2 * NSA_KV), *cmp_w, 256)
            x_last = jnp.pad(kv4_s[:, None, :2 * NSA_KV], ((0, 0), (0, NSA_BLOCK - 1), (0, 0)))
            cmp_last = _compress(x_last, *cmp_w, bd)
            nbs = -(-nb_s // LANES) * LANES
            cmp_s = jnp.concatenate([cmp_past.reshape(2, bd, nb_s - 1, NSA_KV), cmp_last.reshape(2, bd, 1, NSA_KV)], 2)
            cmp_s = jnp.pad(cmp_s, ((0, 0), (0, 0), (0, nbs - nb_s), (0, 0)))
            ocmp_s, idx_s = _nsa_sample_select(q_s, cmp_s[0], cmp_s[1], nb_s, past)
            o_s = _nsa_sample_attend(idx_s, page_table, q_s, ocmp_s, gates_s, kv4_s, kwin_s, nsa_wins[j], nsa_caches[j])
            win_s = jnp.concatenate([nsa_wins[j][:, 1:], kwin_rows_s.reshape(bd, 1, 2, NSA_KV_GROUPS, HEAD_DIM)], axis=1)
            new.append((kv4_rows_p.reshape(batch, seq, 4, NSA_KV_GROUPS, HEAD_DIM),
                        kv4_rows_s.reshape(bd, 1, 4, NSA_KV_GROUPS, HEAD_DIM), win_p, win_s))

        xp, hp = _matmul_ln(o_p, w_o, xp, mod_p(i, 2), vec(ln1_g), vec(ln1_b), mod_p(i, 4), mod_p(i, 3), tm)
        xs, hs = _matmul_ln(o_s, w_o, xs, mod_s(i, 2), vec(ln1_g), vec(ln1_b), mod_s(i, 4), mod_s(i, 3), bd)

        nxt = min(i + 1, depth - 1)
        wq = peer_wq[i].astype(BF16)
        xp, hp = _peer(hp, xp, wq, peer_k1[i], peer_k2[i], u_all, v_all, i, mod_p(i, 5), vec(ln2_g), vec(ln2_b),
                       mod_p(nxt, 1), mod_p(nxt, 0), 256, 512, 1024)
        pad = LANES
        hs_pad = _pad_rows(hs, pad)
        gate_s = _pad_rows(mod_s(i, 5)[0], pad)[None]
        scale_s = _pad_rows(mod_s(nxt, 1)[0], pad)[None]
        shift_s = _pad_rows(mod_s(nxt, 0)[0], pad)[None]
        xs_pad, hs_pad = _peer(hs_pad, _pad_rows(xs, pad), wq, peer_k1[i], peer_k2[i], u_all, v_all, i, gate_s,
                               vec(ln2_g), vec(ln2_b), scale_s, shift_s, pad, pad, 512)
        xs, hs = xs_pad[:bd], hs_pad[:bd]

    return (xp.reshape(batch, seq, d), xs.reshape(bd, 1, d), new[0][0], new[0][1], new[1][0], new[1][1], new[1][2],
            new[1][3], new[2][0], new[2][1], new[3][0], new[3][1], new[3][2], new[3][3])
```
